```python
import jax, jax.numpy as jnp
from jax import lax
import numpy as np

D_MODEL = 1024
BATCH = 16
SEQ = 256
DEPTH = 4
DEC_BATCH = 2
DEC_SEQ = 2048
PAST_LEN = 256

GRID_W = 64
N_MIXERS = 3
N_HEADS = 8
N_KV_HEADS = 2
HEAD_DIM = 128
Q_PER_KV = N_HEADS // N_KV_HEADS
ROPE_THETA = 10000.0
CONV_W = 3
CHUNK = 128
GMLP_DIM = 2 * D_MODEL
GMLP_GROUPS = 8
GMLP_GROUP_DIM = GMLP_DIM // GMLP_GROUPS
D_FF = 2816
N_EXPERTS = 8
TOP_K = 2
D_FF_EXPERT = 3584
Q_BLOCK = 128
EPS = 1e-6
N_CONV_LAYERS = (DEPTH + 2) // 3
N_GMLP_LAYERS = (DEPTH + 1) // 3
N_ATTN_LAYERS = DEPTH // 3
N_DENSE_LAYERS = (DEPTH + 1) // 2
N_MOE_LAYERS = DEPTH // 2

kernel_name = "hybrid_diffusion_prefix_trunk_step"


def rmsnorm(x, g):
    xf = x.astype(jnp.float32)
    y = xf * lax.rsqrt(jnp.mean(xf * xf, axis=-1, keepdims=True) + EPS)
    return y.astype(x.dtype) * g


def layernorm(x, g, b):
    xf = x.astype(jnp.float32)
    mu = jnp.mean(xf, axis=-1, keepdims=True)
    var = jnp.mean(jnp.square(xf - mu), axis=-1, keepdims=True)
    return ((xf - mu) * lax.rsqrt(var + EPS)).astype(x.dtype) * g + b


def short_conv_mixer(h, w_in, conv_w, w_out):
    L = h.shape[1]
    b_gate, c_gate, xv = jnp.split(h @ w_in, 3, axis=-1)
    z = jnp.pad(c_gate * xv, ((0, 0), (CONV_W // 2, CONV_W // 2), (0, 0)))
    conv = z[:, 0:L] * conv_w[0]
    for j in range(1, CONV_W):
        conv = conv + z[:, j:j + L] * conv_w[j]
    return (b_gate * conv) @ w_out


def chunk_gmlp_mixer(h, w_in, b_in, ln_g, ln_b, w_s, b_s, w_out):
    B, L, _ = h.shape
    z = jax.nn.gelu(h @ w_in + b_in)
    u, v = jnp.split(z, 2, axis=-1)
    v = layernorm(v, ln_g, ln_b)
    vg = v.reshape(B, L // CHUNK, CHUNK, GMLP_GROUPS, GMLP_GROUP_DIM)
    mixed = jnp.einsum('gpq,bnqgc->bnpgc', w_s, vg) + b_s.T[:, :, None]
    return (u * mixed.reshape(B, L, GMLP_DIM)) @ w_out


def axial_rope_tables(L):
    rows = L // GRID_W
    row = jnp.repeat(jnp.arange(rows, dtype=jnp.float32), GRID_W)
    col = jnp.tile(jnp.arange(GRID_W, dtype=jnp.float32), rows)
    n_freq = HEAD_DIM // 4
    inv_freq = jnp.power(jnp.float32(ROPE_THETA), -jnp.arange(n_freq, dtype=jnp.float32) / n_freq)
    ang = jnp.concatenate([row[:, None] * inv_freq, col[:, None] * inv_freq], axis=-1)
    return jnp.cos(ang), jnp.sin(ang)


def apply_rope(x, cos, sin):
    xf = x.astype(jnp.float32)
    x1, x2 = xf[..., 0::2], xf[..., 1::2]
    c, s = cos[None, :, None, :], sin[None, :, None, :]
    out = jnp.stack([x1 * c - x2 * s, x1 * s + x2 * c], axis=-1).reshape(x.shape)
    return out.astype(x.dtype)


def attn_project(h, w_qkv, q_g, k_g):
    B, L, _ = h.shape
    qkv = h @ w_qkv
    nq, nk = N_HEADS * HEAD_DIM, N_KV_HEADS * HEAD_DIM
    q = qkv[..., :nq].reshape(B, L, N_HEADS, HEAD_DIM)
    k = qkv[..., nq:nq + nk].reshape(B, L, N_KV_HEADS, HEAD_DIM)
    v = qkv[..., nq + nk:].reshape(B, L, N_KV_HEADS, HEAD_DIM)
    return rmsnorm(q, q_g), rmsnorm(k, k_g), v


def block_attention(q, k, v):
    B, Lq = q.shape[:2]
    nb = Lq // Q_BLOCK
    scale = HEAD_DIM ** -0.5
    qb = q.reshape(B, nb, Q_BLOCK, N_KV_HEADS, Q_PER_KV, HEAD_DIM).transpose(1, 0, 2, 3, 4, 5)

    def one_block(qblk):
        s = jnp.einsum('bqkgd,bskd->bkgqs', qblk, k).astype(jnp.float32) * scale
        p = jax.nn.softmax(s, axis=-1).astype(v.dtype)
        return jnp.einsum('bkgqs,bskd->bqkgd', p, v)

    out = lax.map(one_block, qb)
    return out.transpose(1, 0, 2, 3, 4, 5).reshape(B, Lq, N_HEADS * HEAD_DIM)


def swiglu(h, w_gate, w_up, w_down):
    return (jax.nn.silu(h @ w_gate) * (h @ w_up)) @ w_down


def moe_swiglu(h, w_router, w_gate, w_up, w_down):
    logits = (h @ w_router).astype(jnp.float32)
    top_val, top_idx = lax.top_k(logits, TOP_K)
    top_w = jax.nn.softmax(top_val, axis=-1)
    gates = jnp.sum(jax.nn.one_hot(top_idx, N_EXPERTS, dtype=jnp.float32) * top_w[..., None], axis=-2)
    gates = gates.astype(h.dtype)
    out = jnp.zeros_like(h)
    for e in range(N_EXPERTS):
        out = out + gates[..., e:e + 1] * swiglu(h, w_gate[e], w_up[e], w_down[e])
    return out


def setup_inputs(seed: int = 0) -> dict:
    key = jax.random.key(seed)
    keys = jax.random.split(key, 40)
    counter = [0]

    def nrm(shape, scale):
        k = keys[counter[0]]
        counter[0] += 1
        return jax.random.normal(k, shape, jnp.float32) * scale

    D = D_MODEL
    qkv_out = (N_HEADS + 2 * N_KV_HEADS) * HEAD_DIM
    return {
        "x_prompt": nrm((BATCH, SEQ, D), 1.0),
        "x_sample": nrm((DEC_BATCH, DEC_SEQ, D), 1.0),
        "cache_k": nrm((DEC_BATCH, N_ATTN_LAYERS, PAST_LEN, N_KV_HEADS, HEAD_DIM), 1.0),
        "cache_v": nrm((DEC_BATCH, N_ATTN_LAYERS, PAST_LEN, N_KV_HEADS, HEAD_DIM), 1.0),
        "c": nrm((DEC_BATCH, D), 1.0),
        "c_ctx": nrm((D,), 1.0),
        "w_mod": nrm((DEPTH, D, 6 * D), 0.5 * D ** -0.5),
        "b_mod": nrm((DEPTH, 6 * D), 0.01),
        "g_mix": 1.0 + nrm((DEPTH, D), 0.01),
        "g_ffn": 1.0 + nrm((DEPTH, D), 0.01),
        "conv_w_in": nrm((N_CONV_LAYERS, D, 3 * D), D ** -0.5),
        "conv_w": nrm((N_CONV_LAYERS, CONV_W, D), CONV_W ** -0.5),
        "conv_w_out": nrm((N_CONV_LAYERS, D, D), D ** -0.5),
        "gmlp_w_in": nrm((N_GMLP_LAYERS, D, 2 * GMLP_DIM), D ** -0.5),
        "gmlp_b_in": nrm((N_GMLP_LAYERS, 2 * GMLP_DIM), 0.01),
        "gmlp_ln_g": 1.0 + nrm((N_GMLP_LAYERS, GMLP_DIM), 0.01),
        "gmlp_ln_b": nrm((N_GMLP_LAYERS, GMLP_DIM), 0.01),
        "gmlp_w_s": nrm((N_GMLP_LAYERS, GMLP_GROUPS, CHUNK, CHUNK), CHUNK ** -0.5),
        "gmlp_b_s": nrm((N_GMLP_LAYERS, GMLP_GROUPS, CHUNK), 0.01),
        "gmlp_w_out": nrm((N_GMLP_LAYERS, GMLP_DIM, D), GMLP_DIM ** -0.5),
        "attn_w_qkv": nrm((N_ATTN_LAYERS, D, qkv_out), D ** -0.5),
        "attn_q_g": 1.0 + nrm((N_ATTN_LAYERS, HEAD_DIM), 0.01),
        "attn_k_g": 1.0 + nrm((N_ATTN_LAYERS, HEAD_DIM), 0.01),
        "attn_w_o": nrm((N_ATTN_LAYERS, N_HEADS * HEAD_DIM, D), (N_HEADS * HEAD_DIM) ** -0.5),
        "ffn_w_gate": nrm((N_DENSE_LAYERS, D, D_FF), D ** -0.5),
        "ffn_w_up": nrm((N_DENSE_LAYERS, D, D_FF), D ** -0.5),
        "ffn_w_down": nrm((N_DENSE_LAYERS, D_FF, D), D_FF ** -0.5),
        "moe_w_router": nrm((N_MOE_LAYERS, D, N_EXPERTS), D ** -0.5),
        "moe_w_gate": nrm((N_MOE_LAYERS, N_EXPERTS, D, D_FF_EXPERT), D ** -0.5),
        "moe_w_up": nrm((N_MOE_LAYERS, N_EXPERTS, D, D_FF_EXPERT), D ** -0.5),
        "moe_w_down": nrm((N_MOE_LAYERS, N_EXPERTS, D_FF_EXPERT, D), D_FF_EXPERT ** -0.5),
    }


def reference(x_prompt, x_sample, cache_k, cache_v, c, c_ctx, w_mod, b_mod, g_mix, g_ffn,
              conv_w_in, conv_w, conv_w_out,
              gmlp_w_in, gmlp_b_in, gmlp_ln_g, gmlp_ln_b, gmlp_w_s, gmlp_b_s, gmlp_w_out,
              attn_w_qkv, attn_q_g, attn_k_g, attn_w_o,
              ffn_w_gate, ffn_w_up, ffn_w_down,
              moe_w_router, moe_w_gate, moe_w_up, moe_w_down):

    def run(x, cond, ctx_k=None, ctx_v=None):
        latent = ctx_k is not None
        B, L, _ = x.shape
        if latent:
            cos, sin = axial_rope_tables(L)
        new_k, new_v = [], []
        for i in range(DEPTH):
            mod = (jax.nn.silu(cond) @ w_mod[i] + b_mod[i])[:, None, :]
            sh1, sc1, gt1, sh2, sc2, gt2 = jnp.split(mod, 6, axis=-1)
            h = rmsnorm(x, g_mix[i]) * (1 + sc1) + sh1
            kind, j = i % N_MIXERS, i // N_MIXERS
            if kind == 0:
                y = short_conv_mixer(h, conv_w_in[j], conv_w[j], conv_w_out[j])
            elif kind == 1:
                y = chunk_gmlp_mixer(h, gmlp_w_in[j], gmlp_b_in[j], gmlp_ln_g[j], gmlp_ln_b[j],
                                     gmlp_w_s[j], gmlp_b_s[j], gmlp_w_out[j])
            else:
                q, k, v = attn_project(h, attn_w_qkv[j], attn_q_g[j], attn_k_g[j])
                if latent:
                    q = apply_rope(q, cos, sin)
                    k = apply_rope(k, cos, sin)
                    k = jnp.concatenate([ctx_k[:, j], k], axis=1)
                    v = jnp.concatenate([ctx_v[:, j], v], axis=1)
                else:
                    new_k.append(k)
                    new_v.append(v)
                y = block_attention(q, k, v) @ attn_w_o[j]
            x = x + gt1 * y
            h = rmsnorm(x, g_ffn[i]) * (1 + sc2) + sh2
            if i % 2 == 0:
                y = swiglu(h, ffn_w_gate[i // 2], ffn_w_up[i // 2], ffn_w_down[i // 2])
            else:
                y = moe_swiglu(h, moe_w_router[i // 2], moe_w_gate[i // 2],
                               moe_w_up[i // 2], moe_w_down[i // 2])
            x = x + gt2 * y
        return x, new_k, new_v

    y_prompt, ks, vs = run(x_prompt, c_ctx[None, :])
    new_k = jnp.stack(ks, axis=1)
    new_v = jnp.stack(vs, axis=1)
    y_sample, _, _ = run(x_sample, c, cache_k, cache_v)
    return (y_prompt, y_sample, new_k, new_v)
```

```python
import functools

import jax
import jax.numpy as jnp
from jax import lax
from jax.experimental import pallas as pl
from jax.experimental.pallas import tpu as pltpu

F32 = jnp.float32
BF16 = jnp.bfloat16

D_MODEL = 1024
BATCH = 16
SEQ = 256
DEPTH = 4
DEC_BATCH = 2
DEC_SEQ = 2048
PAST_LEN = 256
GRID_W = 64
N_MIXERS = 3
N_HEADS = 8
N_KV_HEADS = 2
HEAD_DIM = 128
Q_PER_KV = N_HEADS // N_KV_HEADS
ROPE_THETA = 10000.0
CONV_W = 3
CHUNK = 128
GMLP_DIM = 2 * D_MODEL
GMLP_GROUPS = 8
GMLP_GROUP_DIM = GMLP_DIM // GMLP_GROUPS
D_FF = 2816
N_EXPERTS = 8
D_FF_EXPERT = 3584
EPS = 1e-6
QKV_OUT = (N_HEADS + 2 * N_KV_HEADS) * HEAD_DIM

N_PROMPT_ROWS = BATCH * SEQ
N_SAMPLE_ROWS = DEC_BATCH * DEC_SEQ
N_ROWS = N_PROMPT_ROWS + N_SAMPLE_ROWS
N_COND_ROWS = 8

V7X_VMEM_BYTES = 64 * 1024 * 1024
VMEM_LIMIT_BYTES = V7X_VMEM_BYTES - 8 * 1024 * 1024
NORM_ROWS = 256


def _params(n_axes):
    return pltpu.CompilerParams(
        dimension_semantics=("arbitrary",) * n_axes,
        vmem_limit_bytes=VMEM_LIMIT_BYTES)


def _dot(a, b):
    return jnp.dot(a, b, preferred_element_type=F32)


def _cond_row(i, tm):
    n_prompt = N_PROMPT_ROWS // tm
    per_seq = DEC_SEQ // tm
    return jnp.where(i < n_prompt, 0, 1 + (i - n_prompt) // per_seq)


def _mod_spec(layer, tm):
    return pl.BlockSpec((None, None, 6, D_MODEL),
                        lambda i, *_: (layer, _cond_row(i, tm), 0, 0))


def _gain_spec(layer):
    return pl.BlockSpec((None, 1, D_MODEL), lambda *_: (layer, 0, 0))


def _norm_mod_to(x_ref, g_ref, scale, shift, h_ref):
    gs = g_ref[...] * (1.0 + scale)
    n_chunks = x_ref.shape[0] // NORM_ROWS

    def body(r, carry):
        rows = pl.ds(pl.multiple_of(r * NORM_ROWS, NORM_ROWS), NORM_ROWS)
        x = x_ref[rows, :]
        ms = jnp.mean(x * x, axis=-1, keepdims=True)
        h_ref[rows, :] = (x * lax.rsqrt(ms + EPS) * gs + shift).astype(h_ref.dtype)
        return carry

    lax.fori_loop(0, n_chunks, body, 0)


def _mod_kernel(c_ref, w_ref, b_ref, o_ref):
    cnd = c_ref[...]
    s = (cnd * jax.nn.sigmoid(cnd)).astype(BF16)
    o_ref[...] = _dot(s, w_ref[...].astype(BF16)) + b_ref[...]


def _modulation(cond, w_mod, b_mod):
    tn = 1536
    n_out = 6 * D_MODEL
    out = pl.pallas_call(
        _mod_kernel,
        out_shape=jax.ShapeDtypeStruct((DEPTH, N_COND_ROWS, n_out), F32),
        grid=(DEPTH, n_out // tn),
        in_specs=[
            pl.BlockSpec((N_COND_ROWS, D_MODEL), lambda l, j: (0, 0)),
            pl.BlockSpec((None, D_MODEL, tn), lambda l, j: (l, 0, j)),
            pl.BlockSpec((None, 1, tn), lambda l, j: (l, 0, j)),
        ],
        out_specs=pl.BlockSpec((None, N_COND_ROWS, tn), lambda l, j: (l, 0, j)),
        compiler_params=_params(2),
        name="modulation",
    )(cond, w_mod, b_mod.reshape(DEPTH, 1, n_out))
    return out.reshape(DEPTH, N_COND_ROWS, 6, D_MODEL)


CONV_TM = 2048
CONV_TN = 256


def _conv_in_kernel(x_ref, mod_ref, g_ref, wb_ref, wc_ref, wx_ref, cw_ref, o_ref, h_ref):
    i = pl.program_id(0)

    @pl.when(pl.program_id(1) == 0)
    def _():
        _norm_mod_to(x_ref, g_ref, mod_ref[1:2, :], mod_ref[0:1, :], h_ref)

    h = h_ref[...]
    b = _dot(h, wb_ref[...].astype(BF16))
    z = _dot(h, wc_ref[...].astype(BF16)) * _dot(h, wx_ref[...].astype(BF16))
    seq_len = jnp.where(i < N_PROMPT_ROWS // CONV_TM, SEQ, DEC_SEQ)
    pos = lax.broadcasted_iota(jnp.int32, z.shape, 0) & (seq_len - 1)
    z_prev = jnp.where(pos == 0, 0.0, pltpu.roll(z, 1, 0))
    z_next = jnp.where(pos == seq_len - 1, 0.0, pltpu.roll(z, CONV_TM - 1, 0))
    conv = z_prev * cw_ref[0:1, :] + z * cw_ref[1:2, :] + z_next * cw_ref[2:3, :]
    o_ref[...] = (b * conv).astype(o_ref.dtype)


def _conv_in(x, mod, g_mix, conv_w_in, conv_w, layer, j):
    tm, tn = CONV_TM, CONV_TN
    nb = D_MODEL // tn
    w_spec = lambda part: pl.BlockSpec((None, D_MODEL, tn),
                                       lambda i, n: (j, 0, part * nb + n))
    return pl.pallas_call(
        _conv_in_kernel,
        out_shape=jax.ShapeDtypeStruct((N_ROWS, D_MODEL), BF16),
        grid=(N_ROWS // tm, nb),
        in_specs=[
            pl.BlockSpec((tm, D_MODEL), lambda i, n: (i, 0)),
            _mod_spec(layer, tm),
            _gain_spec(layer),
            w_spec(0), w_spec(1), w_spec(2),
            pl.BlockSpec((None, CONV_W, tn), lambda i, n: (j, 0, n)),
        ],
        out_specs=pl.BlockSpec((tm, tn), lambda i, n: (i, n)),
        scratch_shapes=[pltpu.VMEM((tm, D_MODEL), BF16)],
        compiler_params=_params(2),
        name="conv_in",
    )(x, mod, g_mix, conv_w_in, conv_w_in, conv_w_in, conv_w)


PROJ_TM = 1024


def _proj_res_kernel(a_ref, w_ref, x_ref, mod_ref, o_ref):
    y = _dot(a_ref[...], w_ref[...].astype(BF16))
    o_ref[...] = x_ref[...] + mod_ref[2:3, :] * y


def _proj_res(a, w, j, x, mod, layer):
    tm = PROJ_TM
    k = a.shape[1]
    return pl.pallas_call(
        _proj_res_kernel,
        out_shape=jax.ShapeDtypeStruct((N_ROWS, D_MODEL), F32),
        grid=(N_ROWS // tm,),
        in_specs=[
            pl.BlockSpec((tm, k), lambda i: (i, 0)),
            pl.BlockSpec((None, k, D_MODEL), lambda i: (j, 0, 0)),
            pl.BlockSpec((tm, D_MODEL), lambda i: (i, 0)),
            _mod_spec(layer, tm),
        ],
        out_specs=pl.BlockSpec((tm, D_MODEL), lambda i: (i, 0)),
        compiler_params=_params(1),
        name="proj_res",
    )(a, w, x, mod)


GMLP_TM = 1024
GMLP_TN = 1024
GMLP_OUT_TM = 512


def _gelu_tanh(x):
    c = 0.7978845608028654
    return 0.5 * x * (1.0 + jnp.tanh(c * (x + 0.044715 * (x * x * x))))


def _gmlp_in_kernel(x_ref, mod_ref, g_ref, w_ref, b_ref, o_ref, h_ref):
    @pl.when(pl.program_id(1) == 0)
    def _():
        _norm_mod_to(x_ref, g_ref, mod_ref[1:2, :], mod_ref[0:1, :], h_ref)

    z = _dot(h_ref[...], w_ref[...].astype(BF16)) + b_ref[...]
    o_ref[...] = _gelu_tanh(z).astype(o_ref.dtype)


def _gmlp_in(x, mod, g_mix, w_in, b_in, layer, j):
    tm, tn = GMLP_TM, GMLP_TN
    n_out = 2 * GMLP_DIM
    return pl.pallas_call(
        _gmlp_in_kernel,
        out_shape=jax.ShapeDtypeStruct((N_ROWS, n_out), BF16),
        grid=(N_ROWS // tm, n_out // tn),
        in_specs=[
            pl.BlockSpec((tm, D_MODEL), lambda i, n: (i, 0)),
            _mod_spec(layer, tm),
            _gain_spec(layer),
            pl.BlockSpec((None, D_MODEL, tn), lambda i, n: (j, 0, n)),
            pl.BlockSpec((None, 1, tn), lambda i, n: (j, 0, n)),
        ],
        out_specs=pl.BlockSpec((tm, tn), lambda i, n: (i, n)),
        scratch_shapes=[pltpu.VMEM((tm, D_MODEL), BF16)],
        compiler_params=_params(2),
        name="gmlp_in",
    )(x, mod, g_mix, w_in, b_in.reshape(-1, 1, n_out))


def _gmlp_out_kernel(u_ref, v_ref, lng_ref, lnb_ref, ws_ref, bs_ref, w_ref, x_ref, mod_ref,
                     o_ref, a_ref):
    lng = lng_ref[...]
    lnb = lnb_ref[...]

    def chunk(c, carry):
        rows = pl.ds(pl.multiple_of(c * CHUNK, CHUNK), CHUNK)
        v = v_ref[rows, :].astype(F32)
        mu = jnp.mean(v, axis=-1, keepdims=True)
        vc = v - mu
        var = jnp.mean(vc * vc, axis=-1, keepdims=True)
        vn = (vc * lax.rsqrt(var + EPS) * lng + lnb).astype(BF16)
        for g in range(GMLP_GROUPS):
            cols = slice(g * GMLP_GROUP_DIM, (g + 1) * GMLP_GROUP_DIM)
            mixed = _dot(ws_ref[g].astype(BF16), vn[:, cols]) + bs_ref[:, g:g + 1]
            a_ref[rows, cols] = (u_ref[rows, cols].astype(F32) * mixed).astype(a_ref.dtype)
        return carry

    lax.fori_loop(0, u_ref.shape[0] // CHUNK, chunk, 0)
    y = _dot(a_ref[...], w_ref[...].astype(BF16))
    o_ref[...] = x_ref[...] + mod_ref[2:3, :] * y


def _gmlp_out(z, ln_g, ln_b, w_s, b_s, w_out, x, mod, layer, j):
    tm = GMLP_OUT_TM
    return pl.pallas_call(
        _gmlp_out_kernel,
        out_shape=jax.ShapeDtypeStruct((N_ROWS, D_MODEL), F32),
        grid=(N_ROWS // tm,),
        in_specs=[
            pl.BlockSpec((tm, GMLP_DIM), lambda i: (i, 0)),
            pl.BlockSpec((tm, GMLP_DIM), lambda i: (i, 1)),
            pl.BlockSpec((None, 1, GMLP_DIM), lambda i: (j, 0, 0)),
            pl.BlockSpec((None, 1, GMLP_DIM), lambda i: (j, 0, 0)),
            pl.BlockSpec((None, GMLP_GROUPS, CHUNK, CHUNK), lambda i: (j, 0, 0, 0)),
            pl.BlockSpec((None, CHUNK, GMLP_GROUPS), lambda i: (j, 0, 0)),
            pl.BlockSpec((None, GMLP_DIM, D_MODEL), lambda i: (j, 0, 0)),
            pl.BlockSpec((tm, D_MODEL), lambda i: (i, 0)),
            _mod_spec(layer, tm),
        ],
        out_specs=pl.BlockSpec((tm, D_MODEL), lambda i: (i, 0)),
        scratch_shapes=[pltpu.VMEM((tm, GMLP_DIM), BF16)],
        compiler_params=_params(1),
        name="gmlp_out",
    )(z, z, ln_g.reshape(-1, 1, GMLP_DIM), ln_b.reshape(-1, 1, GMLP_DIM), w_s,
      jnp.swapaxes(b_s, 1, 2), w_out, x, mod)


QKV_TM = 512
ATTN_TQ = 256


def _rope_tables(tm):
    pos = jnp.arange(DEC_SEQ)
    row = (pos // GRID_W).astype(F32)
    col = (pos % GRID_W).astype(F32)
    n_freq = HEAD_DIM // 4
    inv_freq = jnp.power(jnp.float32(ROPE_THETA), -jnp.arange(n_freq, dtype=F32) / n_freq)
    ang = jnp.concatenate([row[:, None] * inv_freq, col[:, None] * inv_freq], axis=-1)
    cos = jnp.repeat(jnp.cos(ang), 2, axis=-1)
    sign = jnp.tile(jnp.array([-1.0, 1.0], F32), HEAD_DIM // 2)
    sin = jnp.repeat(jnp.sin(ang), 2, axis=-1) * sign
    cos = jnp.concatenate([jnp.ones((tm, HEAD_DIM), F32), cos], axis=0)
    sin = jnp.concatenate([jnp.zeros((tm, HEAD_DIM), F32), sin], axis=0)
    return cos, sin


def _qkv_kernel(x_ref, mod_ref, g_ref, w_ref, qg_ref, kg_ref, cos_ref, sin_ref, o_ref, h_ref):
    _norm_mod_to(x_ref, g_ref, mod_ref[1:2, :], mod_ref[0:1, :], h_ref)
    y = _dot(h_ref[...], w_ref[...].astype(BF16))
    cos = cos_ref[...]
    sin = sin_ref[...]
    even = (lax.broadcasted_iota(jnp.int32, cos.shape, 1) & 1) == 0
    for hd in range(N_HEADS + N_KV_HEADS):
        cols = slice(hd * HEAD_DIM, (hd + 1) * HEAD_DIM)
        yh = y[:, cols]
        gain = qg_ref[...] if hd < N_HEADS else kg_ref[...]
        ms = jnp.mean(yh * yh, axis=-1, keepdims=True)
        yn = yh * lax.rsqrt(ms + EPS) * gain
        partner = jnp.where(even, pltpu.roll(yn, HEAD_DIM - 1, 1), pltpu.roll(yn, 1, 1))
        o_ref[:, cols] = yn * cos + partner * sin
    v_cols = slice((N_HEADS + N_KV_HEADS) * HEAD_DIM, QKV_OUT)
    o_ref[:, v_cols] = y[:, v_cols]


def _qkv(x, mod, g_mix, w_qkv, q_g, k_g, layer, j):
    tm = QKV_TM
    cos, sin = _rope_tables(tm)
    n_prompt = N_PROMPT_ROWS // tm
    per_seq = DEC_SEQ // tm
    tbl = lambda i: (jnp.where(i < n_prompt, 0, 1 + (i - n_prompt) % per_seq), 0)
    return pl.pallas_call(
        _qkv_kernel,
        out_shape=jax.ShapeDtypeStruct((N_ROWS, QKV_OUT), F32),
        grid=(N_ROWS // tm,),
        in_specs=[
            pl.BlockSpec((tm, D_MODEL), lambda i: (i, 0)),
            _mod_spec(layer, tm),
            _gain_spec(layer),
            pl.BlockSpec((None, D_MODEL, QKV_OUT), lambda i: (j, 0, 0)),
            pl.BlockSpec((None, 1, HEAD_DIM), lambda i: (j, 0, 0)),
            pl.BlockSpec((None, 1, HEAD_DIM), lambda i: (j, 0, 0)),
            pl.BlockSpec((tm, HEAD_DIM), tbl),
            pl.BlockSpec((tm, HEAD_DIM), tbl),
        ],
        out_specs=pl.BlockSpec((tm, QKV_OUT), lambda i: (i, 0)),
        scratch_shapes=[pltpu.VMEM((tm, D_MODEL), BF16)],
        compiler_params=_params(1),
        name="qkv",
    )(x, mod, g_mix, w_qkv, q_g.reshape(-1, 1, HEAD_DIM), k_g.reshape(-1, 1, HEAD_DIM), cos, sin)


def _attn_kernel(*refs, has_ctx):
    if has_ctx:
        q_ref, k_ref, v_ref, ck_ref, cv_ref, o_ref = refs
    else:
        q_ref, k_ref, v_ref, o_ref = refs
    tq = q_ref.shape[0]
    q = q_ref[...] * (HEAD_DIM ** -0.5)
    qs = jnp.concatenate([q[:, g * HEAD_DIM:(g + 1) * HEAD_DIM] for g in range(Q_PER_KV)],
                         axis=0).astype(BF16)
    nt = (((1,), (1,)), ((), ()))
    s = lax.dot_general(qs, k_ref[...].astype(BF16), nt, preferred_element_type=F32)
    m = jnp.max(s, axis=-1, keepdims=True)
    if has_ctx:
        sc = lax.dot_general(qs, ck_ref[...].astype(BF16), nt, preferred_element_type=F32)
        m = jnp.maximum(m, jnp.max(sc, axis=-1, keepdims=True))
    p = jnp.exp(s - m)
    den = jnp.sum(p, axis=-1, keepdims=True)
    o = _dot(p.astype(BF16), v_ref[...].astype(BF16))
    if has_ctx:
        pc = jnp.exp(sc - m)
        den = den + jnp.sum(pc, axis=-1, keepdims=True)
        o = o + _dot(pc.astype(BF16), cv_ref[...].astype(BF16))
    o = o / den
    for g in range(Q_PER_KV):
        o_ref[:, g * HEAD_DIM:(g + 1) * HEAD_DIM] = o[g * tq:(g + 1) * tq].astype(o_ref.dtype)


def _attention(qkv, ctx_k, ctx_v):
    group_w = Q_PER_KV * HEAD_DIM
    k_col = N_HEADS
    v_col = N_HEADS + N_KV_HEADS

    def call(n_batch, seq, tq, row0, ctx):
        q_blocks = seq // tq
        o_row = lambda b, qi: b * q_blocks + qi
        kv_row = lambda b: row0 // seq + b
        in_specs = [
            pl.BlockSpec((tq, group_w), lambda b, kh, qi: (row0 // tq + o_row(b, qi), kh)),
            pl.BlockSpec((seq, HEAD_DIM), lambda b, kh, qi: (kv_row(b), k_col + kh)),
            pl.BlockSpec((seq, HEAD_DIM), lambda b, kh, qi: (kv_row(b), v_col + kh)),
        ]
        args = [qkv, qkv, qkv]
        if ctx is not None:
            ctx_spec = pl.BlockSpec((None, PAST_LEN, HEAD_DIM), lambda b, kh, qi: (b, 0, kh))
            in_specs += [ctx_spec, ctx_spec]
            args += list(ctx)
        return pl.pallas_call(
            functools.partial(_attn_kernel, has_ctx=ctx is not None),
            out_shape=jax.ShapeDtypeStruct((n_batch * seq, N_HEADS * HEAD_DIM), BF16),
            grid=(n_batch, N_KV_HEADS, q_blocks),
            in_specs=in_specs,
            out_specs=pl.BlockSpec((tq, group_w), lambda b, kh, qi: (o_row(b, qi), kh)),
            compiler_params=_params(3),
            name="attention",
        )(*args)

    o_prompt = call(BATCH, SEQ, SEQ, 0, None)
    o_sample = call(DEC_BATCH, DEC_SEQ, ATTN_TQ, N_PROMPT_ROWS, (ctx_k, ctx_v))
    return jnp.concatenate([o_prompt, o_sample], axis=0)


FFN_TM = 1024
FFN_TF = 256
MOE_TF = 512
ROUTER_TM = 1024
GATE_LANES = 128


def _ffn_kernel(*refs, gated):
    if gated:
        x_ref, mod_ref, g_ref, gates_ref, wg_ref, wu_ref, wd_ref, o_ref, h_ref, acc_ref = refs
    else:
        x_ref, mod_ref, g_ref, wg_ref, wu_ref, wd_ref, o_ref, h_ref, acc_ref = refs
    e = pl.program_id(1)
    f = pl.program_id(2)

    @pl.when((e == 0) & (f == 0))
    def _():
        _norm_mod_to(x_ref, g_ref, mod_ref[4:5, :], mod_ref[3:4, :], h_ref)
        acc_ref[...] = jnp.zeros_like(acc_ref)

    h = h_ref[...]
    gate = _dot(h, wg_ref[...].astype(BF16))
    up = _dot(h, wu_ref[...].astype(BF16))
    a = gate * jax.nn.sigmoid(gate) * up
    if gated:
        gates = gates_ref[...]
        lane = lax.broadcasted_iota(jnp.int32, gates.shape, 1)
        a = a * jnp.sum(jnp.where(lane == e, gates, 0.0), axis=-1, keepdims=True)
    acc_ref[...] += _dot(a.astype(BF16), wd_ref[...].astype(BF16))

    @pl.when((e == pl.num_programs(1) - 1) & (f == pl.num_programs(2) - 1))
    def _():
        o_ref[...] = x_ref[...] + mod_ref[5:6, :] * acc_ref[...]


def _ffn(x, mod, g_ffn, w_gate, w_up, w_down, layer, li, gates=None):
    tm = FFN_TM
    n_exp, d_ff = w_gate.shape[1], w_gate.shape[3]
    tf = FFN_TF if gates is None else MOE_TF
    in_specs = [
        pl.BlockSpec((tm, D_MODEL), lambda i, e, f: (i, 0)),
        _mod_spec(layer, tm),
        _gain_spec(layer),
    ]
    args = [x, mod, g_ffn]
    if gates is not None:
        in_specs.append(pl.BlockSpec((tm, GATE_LANES), lambda i, e, f: (i, 0)))
        args.append(gates)
    in_specs += [
        pl.BlockSpec((None, None, D_MODEL, tf), lambda i, e, f: (li, e, 0, f)),
        pl.BlockSpec((None, None, D_MODEL, tf), lambda i, e, f: (li, e, 0, f)),
        pl.BlockSpec((None, None, tf, D_MODEL), lambda i, e, f: (li, e, f, 0)),
    ]
    args += [w_gate, w_up, w_down]
    return pl.pallas_call(
        functools.partial(_ffn_kernel, gated=gates is not None),
        out_shape=jax.ShapeDtypeStruct((N_ROWS, D_MODEL), F32),
        grid=(N_ROWS // tm, n_exp, d_ff // tf),
        in_specs=in_specs,
        out_specs=pl.BlockSpec((tm, D_MODEL), lambda i, e, f: (i, 0)),
        scratch_shapes=[pltpu.VMEM((tm, D_MODEL), BF16), pltpu.VMEM((tm, D_MODEL), F32)],
        compiler_params=_params(3),
        name="ffn",
    )(*args)


def _split_bf16(x):
    hi = x.astype(BF16)
    return hi, (x - hi.astype(F32)).astype(BF16)


def _router_kernel(x_ref, mod_ref, g_ref, w_ref, o_ref, h_ref):
    _norm_mod_to(x_ref, g_ref, mod_ref[4:5, :], mod_ref[3:4, :], h_ref)
    h_hi, h_lo = _split_bf16(h_ref[...])
    w_hi, w_lo = _split_bf16(w_ref[...])
    logits = _dot(h_hi, w_hi) + (_dot(h_hi, w_lo) + _dot(h_lo, w_hi))
    lane = lax.broadcasted_iota(jnp.int32, logits.shape, 1)
    logits = jnp.where(lane < N_EXPERTS, logits, -jnp.inf)
    m1 = jnp.max(logits, axis=-1, keepdims=True)
    i1 = jnp.min(jnp.where(logits == m1, lane, N_EXPERTS), axis=-1, keepdims=True)
    rest = jnp.where(lane == i1, -jnp.inf, logits)
    m2 = jnp.max(rest, axis=-1, keepdims=True)
    i2 = jnp.min(jnp.where(rest == m2, lane, N_EXPERTS), axis=-1, keepdims=True)
    e2 = jnp.exp(m2 - m1)
    w1 = 1.0 / (1.0 + e2)
    o_ref[...] = jnp.where(lane == i1, w1, 0.0) + jnp.where(lane == i2, e2 * w1, 0.0)


def _router(x, mod, g_ffn, w_router, layer, li):
    tm = ROUTER_TM
    w_pad = jnp.pad(w_router, ((0, 0), (0, 0), (0, GATE_LANES - N_EXPERTS)))
    return pl.pallas_call(
        _router_kernel,
        out_shape=jax.ShapeDtypeStruct((N_ROWS, GATE_LANES), F32),
        grid=(N_ROWS // tm,),
        in_specs=[
            pl.BlockSpec((tm, D_MODEL), lambda i: (i, 0)),
            _mod_spec(layer, tm),
            _gain_spec(layer),
            pl.BlockSpec((None, D_MODEL, GATE_LANES), lambda i: (li, 0, 0)),
        ],
        out_specs=pl.BlockSpec((tm, GATE_LANES), lambda i: (i, 0)),
        scratch_shapes=[pltpu.VMEM((tm, D_MODEL), F32)],
        compiler_params=_params(1),
        name="router",
    )(x, mod, g_ffn, w_pad)


def kernel(x_prompt, x_sample, cache_k, cache_v, c, c_ctx, w_mod, b_mod, g_mix, g_ffn,
           conv_w_in, conv_w, conv_w_out,
           gmlp_w_in, gmlp_b_in, gmlp_ln_g, gmlp_ln_b, gmlp_w_s, gmlp_b_s, gmlp_w_out,
           attn_w_qkv, attn_q_g, attn_k_g, attn_w_o,
           ffn_w_gate, ffn_w_up, ffn_w_down,
           moe_w_router, moe_w_gate, moe_w_up, moe_w_down):
    x = jnp.concatenate([x_prompt.reshape(N_PROMPT_ROWS, D_MODEL),
                         x_sample.reshape(N_SAMPLE_ROWS, D_MODEL)], axis=0)
    cond = jnp.zeros((N_COND_ROWS, D_MODEL), F32).at[0].set(c_ctx).at[1:1 + DEC_BATCH].set(c)
    mod = _modulation(cond, w_mod, b_mod)
    g_mix3 = g_mix.reshape(DEPTH, 1, D_MODEL)
    g_ffn3 = g_ffn.reshape(DEPTH, 1, D_MODEL)
    n_attn = cache_k.shape[1]
    ctx_k = cache_k.reshape(DEC_BATCH, n_attn, PAST_LEN, N_KV_HEADS * HEAD_DIM)
    ctx_v = cache_v.reshape(DEC_BATCH, n_attn, PAST_LEN, N_KV_HEADS * HEAD_DIM)
    dense_gate, dense_up, dense_down = (w[:, None] for w in (ffn_w_gate, ffn_w_up, ffn_w_down))

    new_k, new_v = [], []
    for layer in range(DEPTH):
        kind, j = layer % N_MIXERS, layer // N_MIXERS
        if kind == 0:
            a = _conv_in(x, mod, g_mix3, conv_w_in, conv_w, layer, j)
            x = _proj_res(a, conv_w_out, j, x, mod, layer)
        elif kind == 1:
            z = _gmlp_in(x, mod, g_mix3, gmlp_w_in, gmlp_b_in, layer, j)
            x = _gmlp_out(z, gmlp_ln_g, gmlp_ln_b, gmlp_w_s, gmlp_b_s, gmlp_w_out, x, mod,
                          layer, j)
        else:
            qkv = _qkv(x, mod, g_mix3, attn_w_qkv, attn_q_g, attn_k_g, layer, j)
            k_cols = slice(N_HEADS * HEAD_DIM, (N_HEADS + N_KV_HEADS) * HEAD_DIM)
            v_cols = slice((N_HEADS + N_KV_HEADS) * HEAD_DIM, QKV_OUT)
            new_k.append(qkv[:N_PROMPT_ROWS, k_cols].reshape(BATCH, SEQ, N_KV_HEADS, HEAD_DIM))
            new_v.append(qkv[:N_PROMPT_ROWS, v_cols].reshape(BATCH, SEQ, N_KV_HEADS, HEAD_DIM))
            a = _attention(qkv, ctx_k[:, j], ctx_v[:, j])
            x = _proj_res(a, attn_w_o, j, x, mod, layer)
        li = layer // 2
        if layer % 2 == 0:
            x = _ffn(x, mod, g_ffn3, dense_gate, dense_up, dense_down, layer, li)
        else:
            gates = _router(x, mod, g_ffn3, moe_w_router, layer, li)
            x = _ffn(x, mod, g_ffn3, moe_w_gate, moe_w_up, moe_w_down, layer, li, gates=gates)

    y_prompt = x[:N_PROMPT_ROWS].reshape(BATCH, SEQ, D_MODEL)
    y_sample = x[N_PROMPT_ROWS:].reshape(DEC_BATCH, DEC_SEQ, D_MODEL)
    return (y_prompt, y_sample, jnp.stack(new_k, axis=1), jnp.stack(new_v, axis=1))
```

```python
import functools

import jax
import jax.numpy as jnp
from jax import lax
from jax.experimental import pallas as pl
from jax.experimental.pallas import tpu as pltpu

F32 = jnp.float32
BF16 = jnp.bfloat16

D_MODEL = 1024
BATCH = 16
SEQ = 256
DEPTH = 4
DEC_BATCH = 2
DEC_SEQ = 2048
PAST_LEN = 256
GRID_W = 64
N_MIXERS = 3
N_HEADS = 8
N_KV_HEADS = 2
HEAD_DIM = 128
Q_PER_KV = N_HEADS // N_KV_HEADS
ROPE_THETA = 10000.0
CONV_W = 3
CHUNK = 128
GMLP_DIM = 2 * D_MODEL
GMLP_GROUPS = 8
GMLP_GROUP_DIM = GMLP_DIM // GMLP_GROUPS
D_FF = 2816
N_EXPERTS = 8
D_FF_EXPERT = 3584
EPS = 1e-6
QKV_OUT = (N_HEADS + 2 * N_KV_HEADS) * HEAD_DIM

N_PROMPT_ROWS = BATCH * SEQ
N_SAMPLE_ROWS = DEC_BATCH * DEC_SEQ
N_ROWS = N_PROMPT_ROWS + N_SAMPLE_ROWS
N_COND_ROWS = 8

V7X_VMEM_BYTES = 64 * 1024 * 1024
VMEM_LIMIT_BYTES = V7X_VMEM_BYTES - 8 * 1024 * 1024
NORM_ROWS = 256


def _params(n_axes):
    return pltpu.CompilerParams(
        dimension_semantics=("arbitrary",) * n_axes,
        vmem_limit_bytes=VMEM_LIMIT_BYTES)


def _dot(a, b):
    return jnp.dot(a, b, preferred_element_type=F32)


def _cond_row(i, tm):
    n_prompt = N_PROMPT_ROWS // tm
    per_seq = DEC_SEQ // tm
    return jnp.where(i < n_prompt, 0, 1 + (i - n_prompt) // per_seq)


def _mod_spec(layer, tm):
    return pl.BlockSpec((None, None, 6, D_MODEL),
                        lambda i, *_: (layer, _cond_row(i, tm), 0, 0))


def _gain_spec(layer):
    return pl.BlockSpec((None, 1, D_MODEL), lambda *_: (layer, 0, 0))


def _norm_mod_to(x_ref, g_ref, scale, shift, h_ref):
    gs = g_ref[...] * (1.0 + scale)
    n_chunks = x_ref.shape[0] // NORM_ROWS

    def body(r, carry):
        rows = pl.ds(pl.multiple_of(r * NORM_ROWS, NORM_ROWS), NORM_ROWS)
        x = x_ref[rows, :]
        ms = jnp.mean(x * x, axis=-1, keepdims=True)
        h_ref[rows, :] = (x * lax.rsqrt(ms + EPS) * gs + shift).astype(h_ref.dtype)
        return carry

    lax.fori_loop(0, n_chunks, body, 0)


def _mod_kernel(c_ref, w_ref, b_ref, o_ref):
    cnd = c_ref[...]
    s = (cnd * jax.nn.sigmoid(cnd)).astype(BF16)
    o_ref[...] = _dot(s, w_ref[...].astype(BF16)) + b_ref[...]


def _modulation(cond, w_mod, b_mod):
    tn = 1536
    n_out = 6 * D_MODEL
    out = pl.pallas_call(
        _mod_kernel,
        out_shape=jax.ShapeDtypeStruct((DEPTH, N_COND_ROWS, n_out), F32),
        grid=(DEPTH, n_out // tn),
        in_specs=[
            pl.BlockSpec((N_COND_ROWS, D_MODEL), lambda l, j: (0, 0)),
            pl.BlockSpec((None, D_MODEL, tn), lambda l, j: (l, 0, j)),
            pl.BlockSpec((None, 1, tn), lambda l, j: (l, 0, j)),
        ],
        out_specs=pl.BlockSpec((None, N_COND_ROWS, tn), lambda l, j: (l, 0, j)),
        compiler_params=_params(2),
        name="modulation",
    )(cond, w_mod, b_mod.reshape(DEPTH, 1, n_out))
    return out.reshape(DEPTH, N_COND_ROWS, 6, D_MODEL)


CONV_TM = 2048
CONV_TN = 256


def _conv_in_kernel(x_ref, mod_ref, g_ref, wb_ref, wc_ref, wx_ref, cw_ref, o_ref, h_ref):
    i = pl.program_id(0)

    @pl.when(pl.program_id(1) == 0)
    def _():
        _norm_mod_to(x_ref, g_ref, mod_ref[1:2, :], mod_ref[0:1, :], h_ref)

    h = h_ref[...]
    b = _dot(h, wb_ref[...].astype(BF16))
    z = _dot(h, wc_ref[...].astype(BF16)) * _dot(h, wx_ref[...].astype(BF16))
    seq_len = jnp.where(i < N_PROMPT_ROWS // CONV_TM, SEQ, DEC_SEQ)
    pos = lax.broadcasted_iota(jnp.int32, z.shape, 0) & (seq_len - 1)
    z_prev = jnp.where(pos == 0, 0.0, pltpu.roll(z, 1, 0))
    z_next = jnp.where(pos == seq_len - 1, 0.0, pltpu.roll(z, CONV_TM - 1, 0))
    conv = z_prev * cw_ref[0:1, :] + z * cw_ref[1:2, :] + z_next * cw_ref[2:3, :]
    o_ref[...] = (b * conv).astype(o_ref.dtype)


def _conv_in(x, mod, g_mix, conv_w_in, conv_w, layer, j):
    tm, tn = CONV_TM, CONV_TN
    nb = D_MODEL // tn
    w_spec = lambda part: pl.BlockSpec((None, D_MODEL, tn),
                                       lambda i, n: (j, 0, part * nb + n))
    return pl.pallas_call(
        _conv_in_kernel,
        out_shape=jax.ShapeDtypeStruct((N_ROWS, D_MODEL), BF16),
        grid=(N_ROWS // tm, nb),
        in_specs=[
            pl.BlockSpec((tm, D_MODEL), lambda i, n: (i, 0)),
            _mod_spec(layer, tm),
            _gain_spec(layer),
            w_spec(0), w_spec(1), w_spec(2),
            pl.BlockSpec((None, CONV_W, tn), lambda i, n: (j, 0, n)),
        ],
        out_specs=pl.BlockSpec((tm, tn), lambda i, n: (i, n)),
        scratch_shapes=[pltpu.VMEM((tm, D_MODEL), BF16)],
        compiler_params=_params(2),
        name="conv_in",
    )(x, mod, g_mix, conv_w_in, conv_w_in, conv_w_in, conv_w)


PROJ_TM = 1024


def _proj_res_kernel(a_ref, w_ref, x_ref, mod_ref, o_ref):
    y = _dot(a_ref[...], w_ref[...].astype(BF16))
    o_ref[...] = x_ref[...] + mod_ref[2:3, :] * y


def _proj_res(a, w, j, x, mod, layer):
    tm = PROJ_TM
    k = a.shape[1]
    return pl.pallas_call(
        _proj_res_kernel,
        out_shape=jax.ShapeDtypeStruct((N_ROWS, D_MODEL), F32),
        grid=(N_ROWS // tm,),
        in_specs=[
            pl.BlockSpec((tm, k), lambda i: (i, 0)),
            pl.BlockSpec((None, k, D_MODEL), lambda i: (j, 0, 0)),
            pl.BlockSpec((tm, D_MODEL), lambda i: (i, 0)),
            _mod_spec(layer, tm),
        ],
        out_specs=pl.BlockSpec((tm, D_MODEL), lambda i: (i, 0)),
        compiler_params=_params(1),
        name="proj_res",
    )(a, w, x, mod)


GMLP_TM = 1024
GMLP_TN = 1024
GMLP_OUT_TM = 512


def _gelu_tanh(x):
    c = 0.7978845608028654
    return 0.5 * x * (1.0 + jnp.tanh(c * (x + 0.044715 * (x * x * x))))


def _gmlp_in_kernel(x_ref, mod_ref, g_ref, w_ref, b_ref, o_ref, h_ref):
    @pl.when(pl.program_id(1) == 0)
    def _():
        _norm_mod_to(x_ref, g_ref, mod_ref[1:2, :], mod_ref[0:1, :], h_ref)

    z = _dot(h_ref[...], w_ref[...].astype(BF16)) + b_ref[...]
    o_ref[...] = _gelu_tanh(z).astype(o_ref.dtype)


def _gmlp_in(x, mod, g_mix, w_in, b_in, layer, j):
    tm, tn = GMLP_TM, GMLP_TN
    n_out = 2 * GMLP_DIM
    return pl.pallas_call(
        _gmlp_in_kernel,
        out_shape=jax.ShapeDtypeStruct((N_ROWS, n_out), BF16),
        grid=(N_ROWS // tm, n_out // tn),
        in_specs=[
            pl.BlockSpec((tm, D_MODEL), lambda i, n: (i, 0)),
            _mod_spec(layer, tm),
            _gain_spec(layer),
            pl.BlockSpec((None, D_MODEL, tn), lambda i, n: (j, 0, n)),
            pl.BlockSpec((None, 1, tn), lambda i, n: (j, 0, n)),
        ],
        out_specs=pl.BlockSpec((tm, tn), lambda i, n: (i, n)),
        scratch_shapes=[pltpu.VMEM((tm, D_MODEL), BF16)],
        compiler_params=_params(2),
        name="gmlp_in",
    )(x, mod, g_mix, w_in, b_in.reshape(-1, 1, n_out))


def _gmlp_out_kernel(u_ref, v_ref, lng_ref, lnb_ref, ws_ref, bs_ref, w_ref, x_ref, mod_ref,
                     o_ref, a_ref):
    lng = lng_ref[...]
    lnb = lnb_ref[...]

    def chunk(c, carry):
        rows = pl.ds(pl.multiple_of(c * CHUNK, CHUNK), CHUNK)
        v = v_ref[rows, :].astype(F32)
        mu = jnp.mean(v, axis=-1, keepdims=True)
        vc = v - mu
        var = jnp.mean(vc * vc, axis=-1, keepdims=True)
        vn = (vc * lax.rsqrt(var + EPS) * lng + lnb).astype(BF16)
        for g in range(GMLP_GROUPS):
            cols = slice(g * GMLP_GROUP_DIM, (g + 1) * GMLP_GROUP_DIM)
            mixed = _dot(ws_ref[g].astype(BF16), vn[:, cols]) + bs_ref[:, g:g + 1]
            a_ref[rows, cols] = (u_ref[rows, cols].astype(F32) * mixed).astype(a_ref.dtype)
        return carry

    lax.fori_loop(0, u_ref.shape[0] // CHUNK, chunk, 0)
    y = _dot(a_ref[...], w_ref[...].astype(BF16))
    o_ref[...] = x_ref[...] + mod_ref[2:3, :] * y


def _gmlp_out(z, ln_g, ln_b, w_s, b_s, w_out, x, mod, layer, j):
    tm = GMLP_OUT_TM
    return pl.pallas_call(
        _gmlp_out_kernel,
        out_shape=jax.ShapeDtypeStruct((N_ROWS, D_MODEL), F32),
        grid=(N_ROWS // tm,),
        in_specs=[
            pl.BlockSpec((tm, GMLP_DIM), lambda i: (i, 0)),
            pl.BlockSpec((tm, GMLP_DIM), lambda i: (i, 1)),
            pl.BlockSpec((None, 1, GMLP_DIM), lambda i: (j, 0, 0)),
            pl.BlockSpec((None, 1, GMLP_DIM), lambda i: (j, 0, 0)),
            pl.BlockSpec((None, GMLP_GROUPS, CHUNK, CHUNK), lambda i: (j, 0, 0, 0)),
            pl.BlockSpec((None, CHUNK, GMLP_GROUPS), lambda i: (j, 0, 0)),
            pl.BlockSpec((None, GMLP_DIM, D_MODEL), lambda i: (j, 0, 0)),
            pl.BlockSpec((tm, D_MODEL), lambda i: (i, 0)),
            _mod_spec(layer, tm),
        ],
        out_specs=pl.BlockSpec((tm, D_MODEL), lambda i: (i, 0)),
        scratch_shapes=[pltpu.VMEM((tm, GMLP_DIM), BF16)],
        compiler_params=_params(1),
        name="gmlp_out",
    )(z, z, ln_g.reshape(-1, 1, GMLP_DIM), ln_b.reshape(-1, 1, GMLP_DIM), w_s,
      jnp.swapaxes(b_s, 1, 2), w_out, x, mod)


QKV_TM = 512
ATTN_TQ = 256


def _rope_tables(tm):
    pos = jnp.arange(DEC_SEQ)
    row = (pos // GRID_W).astype(F32)
    col = (pos % GRID_W).astype(F32)
    n_freq = HEAD_DIM // 4
    inv_freq = jnp.power(jnp.float32(ROPE_THETA), -jnp.arange(n_freq, dtype=F32) / n_freq)
    ang = jnp.concatenate([row[:, None] * inv_freq, col[:, None] * inv_freq], axis=-1)
    cos = jnp.repeat(jnp.cos(ang), 2, axis=-1)
    sign = jnp.tile(jnp.array([-1.0, 1.0], F32), HEAD_DIM // 2)
    sin = jnp.repeat(jnp.sin(ang), 2, axis=-1) * sign
    cos = jnp.concatenate([jnp.ones((tm, HEAD_DIM), F32), cos], axis=0)
    sin = jnp.concatenate([jnp.zeros((tm, HEAD_DIM), F32), sin], axis=0)
    return cos, sin


def _qkv_kernel(x_ref, mod_ref, g_ref, w_ref, qg_ref, kg_ref, cos_ref, sin_ref, o_ref, h_ref):
    _norm_mod_to(x_ref, g_ref, mod_ref[1:2, :], mod_ref[0:1, :], h_ref)
    y = _dot(h_ref[...], w_ref[...].astype(BF16))
    cos = cos_ref[...]
    sin = sin_ref[...]
    even = (lax.broadcasted_iota(jnp.int32, cos.shape, 1) & 1) == 0
    for hd in range(N_HEADS + N_KV_HEADS):
        cols = slice(hd * HEAD_DIM, (hd + 1) * HEAD_DIM)
        yh = y[:, cols]
        gain = qg_ref[...] if hd < N_HEADS else kg_ref[...]
        ms = jnp.mean(yh * yh, axis=-1, keepdims=True)
        yn = yh * lax.rsqrt(ms + EPS) * gain
        partner = jnp.where(even, pltpu.roll(yn, HEAD_DIM - 1, 1), pltpu.roll(yn, 1, 1))
        o_ref[:, cols] = yn * cos + partner * sin
    v_cols = slice((N_HEADS + N_KV_HEADS) * HEAD_DIM, QKV_OUT)
    o_ref[:, v_cols] = y[:, v_cols]


def _qkv(x, mod, g_mix, w_qkv, q_g, k_g, layer, j):
    tm = QKV_TM
    cos, sin = _rope_tables(tm)
    n_prompt = N_PROMPT_ROWS // tm
    per_seq = DEC_SEQ // tm
    tbl = lambda i: (jnp.where(i < n_prompt, 0, 1 + (i - n_prompt) % per_seq), 0)
    return pl.pallas_call(
        _qkv_kernel,
        out_shape=jax.ShapeDtypeStruct((N_ROWS, QKV_OUT), F32),
        grid=(N_ROWS // tm,),
        in_specs=[
            pl.BlockSpec((tm, D_MODEL), lambda i: (i, 0)),
            _mod_spec(layer, tm),
            _gain_spec(layer),
            pl.BlockSpec((None, D_MODEL, QKV_OUT), lambda i: (j, 0, 0)),
            pl.BlockSpec((None, 1, HEAD_DIM), lambda i: (j, 0, 0)),
            pl.BlockSpec((None, 1, HEAD_DIM), lambda i: (j, 0, 0)),
            pl.BlockSpec((tm, HEAD_DIM), tbl),
            pl.BlockSpec((tm, HEAD_DIM), tbl),
        ],
        out_specs=pl.BlockSpec((tm, QKV_OUT), lambda i: (i, 0)),
        scratch_shapes=[pltpu.VMEM((tm, D_MODEL), BF16)],
        compiler_params=_params(1),
        name="qkv",
    )(x, mod, g_mix, w_qkv, q_g.reshape(-1, 1, HEAD_DIM), k_g.reshape(-1, 1, HEAD_DIM), cos, sin)


def _attn_kernel(*refs, has_ctx):
    if has_ctx:
        q_ref, k_ref, v_ref, ck_ref, cv_ref, o_ref = refs
    else:
        q_ref, k_ref, v_ref, o_ref = refs
    tq = q_ref.shape[0]
    q = q_ref[...] * (HEAD_DIM ** -0.5)
    qs = jnp.concatenate([q[:, g * HEAD_DIM:(g + 1) * HEAD_DIM] for g in range(Q_PER_KV)],
                         axis=0).astype(BF16)
    nt = (((1,), (1,)), ((), ()))
    s = lax.dot_general(qs, k_ref[...].astype(BF16), nt, preferred_element_type=F32)
    m = jnp.max(s, axis=-1, keepdims=True)
    if has_ctx:
        sc = lax.dot_general(qs, ck_ref[...].astype(BF16), nt, preferred_element_type=F32)
        m = jnp.maximum(m, jnp.max(sc, axis=-1, keepdims=True))
    p = jnp.exp(s - m)
    den = jnp.sum(p, axis=-1, keepdims=True)
    o = _dot(p.astype(BF16), v_ref[...].astype(BF16))
    if has_ctx:
        pc = jnp.exp(sc - m)
        den = den + jnp.sum(pc, axis=-1, keepdims=True)
        o = o + _dot(pc.astype(BF16), cv_ref[...].astype(BF16))
    o = o / den
    for g in range(Q_PER_KV):
        o_ref[:, g * HEAD_DIM:(g + 1) * HEAD_DIM] = o[g * tq:(g + 1) * tq].astype(o_ref.dtype)


def _attention(qkv, ctx_k, ctx_v):
    group_w = Q_PER_KV * HEAD_DIM
    k_col = N_HEADS
    v_col = N_HEADS + N_KV_HEADS

    def call(n_batch, seq, tq, row0, ctx):
        q_blocks = seq // tq
        o_row = lambda b, qi: b * q_blocks + qi
        kv_row = lambda b: row0 // seq + b
        in_specs = [
            pl.BlockSpec((tq, group_w), lambda b, kh, qi: (row0 // tq + o_row(b, qi), kh)),
            pl.BlockSpec((seq, HEAD_DIM), lambda b, kh, qi: (kv_row(b), k_col + kh)),
            pl.BlockSpec((seq, HEAD_DIM), lambda b, kh, qi: (kv_row(b), v_col + kh)),
        ]
        args = [qkv, qkv, qkv]
        if ctx is not None:
            ctx_spec = pl.BlockSpec((None, PAST_LEN, HEAD_DIM), lambda b, kh, qi: (b, 0, kh))
            in_specs += [ctx_spec, ctx_spec]
            args += list(ctx)
        return pl.pallas_call(
            functools.partial(_attn_kernel, has_ctx=ctx is not None),
            out_shape=jax.ShapeDtypeStruct((n_batch * seq, N_HEADS * HEAD_DIM), BF16),
            grid=(n_batch, N_KV_HEADS, q_blocks),
            in_specs=in_specs,
            out_specs=pl.BlockSpec((tq, group_w), lambda b, kh, qi: (o_row(b, qi), kh)),
            compiler_params=_params(3),
            name="attention",
        )(*args)

    o_prompt = call(BATCH, SEQ, SEQ, 0, None)
    o_sample = call(DEC_BATCH, DEC_SEQ, ATTN_TQ, N_PROMPT_ROWS, (ctx_k, ctx_v))
    return jnp.concatenate([o_prompt, o_sample], axis=0)


FFN_TM = 1024
FFN_TF = 256
TOP_K = 2
ROUTER_TM = 1024
ROUTE_LANES = 128
MOE_TF = 512
MOE_TILE = 1024
MOE_SUB = 256
MOE_MAX_TILES = TOP_K * N_ROWS // MOE_TILE + N_EXPERTS
MOE_ROWS = MOE_MAX_TILES * MOE_TILE
SCATTER_TM = 512
COMBINE_TM = 256


def _swiglu_hidden(h, wg, wu):
    gate = _dot(h, wg)
    return (gate * jax.nn.sigmoid(gate) * _dot(h, wu)).astype(BF16)


def _ffn_kernel(x_ref, mod_ref, g_ref, wg_ref, wu_ref, wd_ref, o_ref, h_ref, acc_ref):
    f = pl.program_id(1)

    @pl.when(f == 0)
    def _():
        _norm_mod_to(x_ref, g_ref, mod_ref[4:5, :], mod_ref[3:4, :], h_ref)
        acc_ref[...] = jnp.zeros_like(acc_ref)

    a = _swiglu_hidden(h_ref[...], wg_ref[...].astype(BF16), wu_ref[...].astype(BF16))
    acc_ref[...] += _dot(a, wd_ref[...].astype(BF16))

    @pl.when(f == pl.num_programs(1) - 1)
    def _():
        o_ref[...] = x_ref[...] + mod_ref[5:6, :] * acc_ref[...]


def _ffn(x, mod, g_ffn, w_gate, w_up, w_down, layer, li):
    tm, tf = FFN_TM, FFN_TF
    return pl.pallas_call(
        _ffn_kernel,
        out_shape=jax.ShapeDtypeStruct((N_ROWS, D_MODEL), F32),
        grid=(N_ROWS // tm, D_FF // tf),
        in_specs=[
            pl.BlockSpec((tm, D_MODEL), lambda i, f: (i, 0)),
            _mod_spec(layer, tm),
            _gain_spec(layer),
            pl.BlockSpec((None, D_MODEL, tf), lambda i, f: (li, 0, f)),
            pl.BlockSpec((None, D_MODEL, tf), lambda i, f: (li, 0, f)),
            pl.BlockSpec((None, tf, D_MODEL), lambda i, f: (li, f, 0)),
        ],
        out_specs=pl.BlockSpec((tm, D_MODEL), lambda i, f: (i, 0)),
        scratch_shapes=[pltpu.VMEM((tm, D_MODEL), BF16), pltpu.VMEM((tm, D_MODEL), F32)],
        compiler_params=_params(2),
        name="ffn",
    )(x, mod, g_ffn, w_gate, w_up, w_down)


def _split_bf16(x):
    hi = x.astype(BF16)
    return hi, (x - hi.astype(F32)).astype(BF16)


def _router_kernel(x_ref, mod_ref, g_ref, w_ref, route_ref, cnt_ref, h_ref, seen_ref):
    @pl.when(pl.program_id(0) == 0)
    def _():
        seen_ref[...] = jnp.zeros_like(seen_ref)

    _norm_mod_to(x_ref, g_ref, mod_ref[4:5, :], mod_ref[3:4, :], h_ref)
    h_hi, h_lo = _split_bf16(h_ref[...])
    w_hi, w_lo = _split_bf16(w_ref[...])
    logits = _dot(h_hi, w_hi) + (_dot(h_hi, w_lo) + _dot(h_lo, w_hi))
    lane = lax.broadcasted_iota(jnp.int32, logits.shape, 1)
    logits = jnp.where(lane < N_EXPERTS, logits, -jnp.inf)
    m1 = jnp.max(logits, axis=-1, keepdims=True)
    i1 = jnp.min(jnp.where(logits == m1, lane, N_EXPERTS), axis=-1, keepdims=True)
    rest = jnp.where(lane == i1, -jnp.inf, logits)
    m2 = jnp.max(rest, axis=-1, keepdims=True)
    i2 = jnp.min(jnp.where(rest == m2, lane, N_EXPERTS), axis=-1, keepdims=True)
    e2 = jnp.exp(m2 - m1)
    w1 = 1.0 / (1.0 + e2)
    chosen = jnp.where((lane == i1) | (lane == i2), 1.0, 0.0)
    tm = chosen.shape[0]
    earlier = (lax.broadcasted_iota(jnp.int32, (tm, tm), 1)
               < lax.broadcasted_iota(jnp.int32, (tm, tm), 0))
    prior = _dot(jnp.where(earlier, 1.0, 0.0).astype(BF16), chosen.astype(BF16)) + seen_ref[...]
    rank1 = jnp.sum(jnp.where(lane == i1, prior, 0.0), axis=-1, keepdims=True)
    rank2 = jnp.sum(jnp.where(lane == i2, prior, 0.0), axis=-1, keepdims=True)
    seen_ref[...] += jnp.sum(chosen, axis=0, keepdims=True)
    route = jnp.zeros_like(logits)
    for k, field in enumerate((w1, e2 * w1, i1.astype(F32), i2.astype(F32), rank1, rank2)):
        route = jnp.where(lane == k, field, route)
    route_ref[...] = route
    cnt_ref[...] = jnp.broadcast_to(seen_ref[...], cnt_ref.shape)


def _router(x, mod, g_ffn, w_router, layer, li):
    tm = ROUTER_TM
    w_pad = jnp.pad(w_router, ((0, 0), (0, 0), (0, ROUTE_LANES - N_EXPERTS)))
    return pl.pallas_call(
        _router_kernel,
        out_shape=(jax.ShapeDtypeStruct((N_ROWS, ROUTE_LANES), F32),
                   jax.ShapeDtypeStruct((8, ROUTE_LANES), F32)),
        grid=(N_ROWS // tm,),
        in_specs=[
            pl.BlockSpec((tm, D_MODEL), lambda i: (i, 0)),
            _mod_spec(layer, tm),
            _gain_spec(layer),
            pl.BlockSpec((None, D_MODEL, ROUTE_LANES), lambda i: (li, 0, 0)),
        ],
        out_specs=(pl.BlockSpec((tm, ROUTE_LANES), lambda i: (i, 0)),
                   pl.BlockSpec((8, ROUTE_LANES), lambda i: (0, 0))),
        scratch_shapes=[pltpu.VMEM((tm, D_MODEL), F32), pltpu.VMEM((1, ROUTE_LANES), F32)],
        compiler_params=_params(1),
        name="router",
    )(x, mod, g_ffn, w_pad)


def _dispatch_plan(route, cnt):
    counts = cnt[0, :N_EXPERTS].astype(jnp.int32)
    e1, e2, r1, r2 = (route[:, k].astype(jnp.int32) for k in (2, 3, 4, 5))
    n_tiles = (counts + MOE_TILE - 1) // MOE_TILE
    tile_end = jnp.cumsum(n_tiles)
    tile_start = tile_end - n_tiles
    row_start = tile_start * MOE_TILE
    pos = jnp.concatenate([row_start[e1] + r1, row_start[e2] + r2])
    n_active = tile_end[-1]
    t = jnp.arange(MOE_MAX_TILES, dtype=jnp.int32)
    tile_block = jnp.minimum(t, n_active - 1)
    tile_expert = jnp.sum(tile_block[:, None] >= tile_end[None, :], axis=1).astype(jnp.int32)
    valid = counts[tile_expert] - (tile_block - tile_start[tile_expert]) * MOE_TILE
    n_sub = (jnp.clip(valid, 0, MOE_TILE) + MOE_SUB - 1) // MOE_SUB
    tile_nsub = jnp.where(t < n_active, n_sub, 0).astype(jnp.int32)
    pad = jnp.concatenate([row_start + counts, (-counts) % MOE_SUB]).astype(jnp.int32)
    return dict(pos=pos, pad=pad, tile_expert=tile_expert, tile_nsub=tile_nsub,
                tile_block=tile_block)


def _scatter_kernel(pos_ref, pad_ref, tn_ref, x_ref, mod_ref, g_ref, xs_ref, h_ref, zero_ref,
                    sem, fill_sem):
    i = pl.program_id(0)
    ts = x_ref.shape[0]
    _norm_mod_to(x_ref, g_ref, mod_ref[4:5, :], mod_ref[3:4, :], h_ref)

    def row_copy(r, k):
        dst = pos_ref[k * N_ROWS + i * ts + r]
        return pltpu.make_async_copy(h_ref.at[pl.ds(r, 1)], xs_ref.at[pl.ds(dst, 1)], sem)

    def start_row(r, carry):
        for k in range(TOP_K):
            row_copy(r, k).start()
        return carry

    def wait_row(r, carry):
        for k in range(TOP_K):
            row_copy(r, k).wait()
        return carry

    lax.fori_loop(0, ts, start_row, 0)

    @pl.when(i == 0)
    def _():
        zero_ref[...] = jnp.zeros_like(zero_ref)

        def fills(act):
            for e in range(N_EXPERTS):
                def row_body(r, carry, e=e):
                    act(pltpu.make_async_copy(zero_ref.at[pl.ds(0, 1)],
                                              xs_ref.at[pl.ds(pad_ref[e] + r, 1)], fill_sem))
                    return carry
                lax.fori_loop(0, pad_ref[N_EXPERTS + e], row_body, 0)

            def tile_body(t, carry):
                def sub_body(s, carry):
                    row = pl.multiple_of(t * MOE_TILE + s * MOE_SUB, MOE_SUB)
                    act(pltpu.make_async_copy(zero_ref, xs_ref.at[pl.ds(row, MOE_SUB)], fill_sem))
                    return carry
                return lax.fori_loop(tn_ref[t], MOE_TILE // MOE_SUB, sub_body, carry)
            lax.fori_loop(0, MOE_MAX_TILES, tile_body, 0)

        fills(lambda cp: cp.start())
        fills(lambda cp: cp.wait())

    lax.fori_loop(0, ts, wait_row, 0)


def _moe_scatter(x, mod, g_ffn, plan, layer):
    ts = SCATTER_TM
    grid_spec = pltpu.PrefetchScalarGridSpec(
        num_scalar_prefetch=3,
        grid=(N_ROWS // ts,),
        in_specs=[
            pl.BlockSpec((ts, D_MODEL), lambda i, *_: (i, 0)),
            _mod_spec(layer, ts),
            _gain_spec(layer),
        ],
        out_specs=pl.BlockSpec(memory_space=pl.ANY),
        scratch_shapes=[pltpu.VMEM((ts, D_MODEL), F32),
                        pltpu.VMEM((MOE_SUB, D_MODEL), F32),
                        pltpu.SemaphoreType.DMA(()), pltpu.SemaphoreType.DMA(())],
    )
    return pl.pallas_call(
        _scatter_kernel,
        out_shape=jax.ShapeDtypeStruct((MOE_ROWS, D_MODEL), F32),
        grid_spec=grid_spec,
        compiler_params=_params(1),
        name="moe_scatter",
    )(plan["pos"], plan["pad"], plan["tile_nsub"], x, mod, g_ffn)


def _moe_kernel(te_ref, tn_ref, tb_ref, x_ref, wg_ref, wu_ref, wd_ref, o_ref,
                xb_ref, wgb_ref, wub_ref, wdb_ref):
    n_sub = tn_ref[pl.program_id(0)]

    def sub_rows(s):
        return pl.ds(pl.multiple_of(s * MOE_SUB, MOE_SUB), MOE_SUB)

    @pl.when(pl.program_id(1) == 0)
    def _():
        o_ref[...] = jnp.zeros_like(o_ref)

    @pl.when(n_sub > 0)
    def _():
        @pl.when(pl.program_id(1) == 0)
        def _():
            def cast(s, carry):
                xb_ref[sub_rows(s), :] = x_ref[sub_rows(s), :].astype(BF16)
                return carry

            lax.fori_loop(0, n_sub, cast, 0)

        wgb_ref[...] = wg_ref[...].astype(BF16)
        wub_ref[...] = wu_ref[...].astype(BF16)
        wdb_ref[...] = wd_ref[...].astype(BF16)

        def block(s, carry):
            a = _swiglu_hidden(xb_ref[sub_rows(s), :], wgb_ref[...], wub_ref[...])
            o_ref[sub_rows(s), :] += _dot(a, wdb_ref[...])
            return carry

        lax.fori_loop(0, n_sub, block, 0)


def _moe_experts(xs, plan, w_gate, w_up, w_down, li):
    tile, tf = MOE_TILE, MOE_TF
    nf = D_FF_EXPERT // tf
    frozen_f = lambda i, f, tn: jnp.where(tn[i] > 0, f, nf - 1)
    grid_spec = pltpu.PrefetchScalarGridSpec(
        num_scalar_prefetch=3,
        grid=(MOE_MAX_TILES, nf),
        in_specs=[
            pl.BlockSpec((tile, D_MODEL), lambda i, f, te, tn, tb: (tb[i], 0)),
            pl.BlockSpec((None, None, D_MODEL, tf),
                         lambda i, f, te, tn, tb: (li, te[i], 0, frozen_f(i, f, tn))),
            pl.BlockSpec((None, None, D_MODEL, tf),
                         lambda i, f, te, tn, tb: (li, te[i], 0, frozen_f(i, f, tn))),
            pl.BlockSpec((None, None, tf, D_MODEL),
                         lambda i, f, te, tn, tb: (li, te[i], frozen_f(i, f, tn), 0)),
        ],
        out_specs=pl.BlockSpec((tile, D_MODEL), lambda i, f, te, tn, tb: (i, 0)),
        scratch_shapes=[pltpu.VMEM((tile, D_MODEL), BF16),
                        pltpu.VMEM((D_MODEL, tf), BF16), pltpu.VMEM((D_MODEL, tf), BF16),
                        pltpu.VMEM((tf, D_MODEL), BF16)],
    )
    return pl.pallas_call(
        _moe_kernel,
        out_shape=jax.ShapeDtypeStruct((MOE_ROWS, D_MODEL), F32),
        grid_spec=grid_spec,
        compiler_params=_params(2),
        name="moe_experts",
    )(plan["tile_expert"], plan["tile_nsub"], plan["tile_block"], xs, w_gate, w_up, w_down)


def _combine_kernel(pos_ref, x_ref, mod_ref, route_ref, ys_ref, o_ref, buf_ref, sem):
    i = pl.program_id(0)
    tc = x_ref.shape[0]

    def row_copy(tile, r, k):
        src = pos_ref[k * N_ROWS + tile * tc + r]
        slot = tile % 2
        return pltpu.make_async_copy(ys_ref.at[pl.ds(src, 1)],
                                     buf_ref.at[slot, k, pl.ds(r, 1)], sem.at[slot])

    def for_rows(tile, act):
        def body(r, carry):
            for k in range(TOP_K):
                act(row_copy(tile, r, k))
            return carry
        lax.fori_loop(0, tc, body, 0)

    @pl.when(i == 0)
    def _():
        for_rows(i, lambda cp: cp.start())

    @pl.when(i + 1 < pl.num_programs(0))
    def _():
        for_rows(i + 1, lambda cp: cp.start())

    for_rows(i, lambda cp: cp.wait())
    slot = i % 2
    y = buf_ref[slot, 0] * route_ref[:, 0:1] + buf_ref[slot, 1] * route_ref[:, 1:2]
    o_ref[...] = x_ref[...] + mod_ref[5:6, :] * y


def _moe_combine(x, mod, route, ys, plan, layer):
    tc = COMBINE_TM
    grid_spec = pltpu.PrefetchScalarGridSpec(
        num_scalar_prefetch=1,
        grid=(N_ROWS // tc,),
        in_specs=[
            pl.BlockSpec((tc, D_MODEL), lambda i, *_: (i, 0)),
            _mod_spec(layer, tc),
            pl.BlockSpec((tc, ROUTE_LANES), lambda i, *_: (i, 0)),
            pl.BlockSpec(memory_space=pl.ANY),
        ],
        out_specs=pl.BlockSpec((tc, D_MODEL), lambda i, *_: (i, 0)),
        scratch_shapes=[pltpu.VMEM((2, TOP_K, tc, D_MODEL), F32),
                        pltpu.SemaphoreType.DMA((2,))],
    )
    return pl.pallas_call(
        _combine_kernel,
        out_shape=jax.ShapeDtypeStruct((N_ROWS, D_MODEL), F32),
        grid_spec=grid_spec,
        compiler_params=_params(1),
        name="moe_combine",
    )(plan["pos"], x, mod, route, ys)


def kernel(x_prompt, x_sample, cache_k, cache_v, c, c_ctx, w_mod, b_mod, g_mix, g_ffn,
           conv_w_in, conv_w, conv_w_out,
           gmlp_w_in, gmlp_b_in, gmlp_ln_g, gmlp_ln_b, gmlp_w_s, gmlp_b_s, gmlp_w_out,
           attn_w_qkv, attn_q_g, attn_k_g, attn_w_o,
           ffn_w_gate, ffn_w_up, ffn_w_down,
           moe_w_router, moe_w_gate, moe_w_up, moe_w_down):
    x = jnp.concatenate([x_prompt.reshape(N_PROMPT_ROWS, D_MODEL),
                         x_sample.reshape(N_SAMPLE_ROWS, D_MODEL)], axis=0)
    cond = jnp.zeros((N_COND_ROWS, D_MODEL), F32).at[0].set(c_ctx).at[1:1 + DEC_BATCH].set(c)
    mod = _modulation(cond, w_mod, b_mod)
    g_mix3 = g_mix.reshape(DEPTH, 1, D_MODEL)
    g_ffn3 = g_ffn.reshape(DEPTH, 1, D_MODEL)
    n_attn = cache_k.shape[1]
    ctx_k = cache_k.reshape(DEC_BATCH, n_attn, PAST_LEN, N_KV_HEADS * HEAD_DIM)
    ctx_v = cache_v.reshape(DEC_BATCH, n_attn, PAST_LEN, N_KV_HEADS * HEAD_DIM)

    new_k, new_v = [], []
    for layer in range(DEPTH):
        kind, j = layer % N_MIXERS, layer // N_MIXERS
        if kind == 0:
            a = _conv_in(x, mod, g_mix3, conv_w_in, conv_w, layer, j)
            x = _proj_res(a, conv_w_out, j, x, mod, layer)
        elif kind == 1:
            z = _gmlp_in(x, mod, g_mix3, gmlp_w_in, gmlp_b_in, layer, j)
            x = _gmlp_out(z, gmlp_ln_g, gmlp_ln_b, gmlp_w_s, gmlp_b_s, gmlp_w_out, x, mod,
                          layer, j)
        else:
            qkv = _qkv(x, mod, g_mix3, attn_w_qkv, attn_q_g, attn_k_g, layer, j)
            k_cols = slice(N_HEADS * HEAD_DIM, (N_HEADS + N_KV_HEADS) * HEAD_DIM)
            v_cols = slice((N_HEADS + N_KV_HEADS) * HEAD_DIM, QKV_OUT)
            new_k.append(qkv[:N_PROMPT_ROWS, k_cols].reshape(BATCH, SEQ, N_KV_HEADS, HEAD_DIM))
            new_v.append(qkv[:N_PROMPT_ROWS, v_cols].reshape(BATCH, SEQ, N_KV_HEADS, HEAD_DIM))
            a = _attention(qkv, ctx_k[:, j], ctx_v[:, j])
            x = _proj_res(a, attn_w_o, j, x, mod, layer)
        li = layer // 2
        if layer % 2 == 0:
            x = _ffn(x, mod, g_ffn3, ffn_w_gate, ffn_w_up, ffn_w_down, layer, li)
        else:
            route, cnt = _router(x, mod, g_ffn3, moe_w_router, layer, li)
            plan = _dispatch_plan(route, cnt)
            xs = _moe_scatter(x, mod, g_ffn3, plan, layer)
            ys = _moe_experts(xs, plan, moe_w_gate, moe_w_up, moe_w_down, li)
            x = _moe_combine(x, mod, route, ys, plan, layer)

    y_prompt = x[:N_PROMPT_ROWS].reshape(BATCH, SEQ, D_MODEL)
    y_sample = x[N_PROMPT_ROWS:].reshape(DEC_BATCH, DEC_SEQ, D_MODEL)
    return (y_prompt, y_sample, jnp.stack(new_k, axis=1), jnp.stack(new_v, axis=1))
```

```python
import functools

import jax
import jax.numpy as jnp
from jax import lax
from jax.experimental import pallas as pl
from jax.experimental.pallas import tpu as pltpu

F32 = jnp.float32
BF16 = jnp.bfloat16

D_MODEL = 1024
BATCH = 16
SEQ = 256
DEPTH = 4
DEC_BATCH = 2
DEC_SEQ = 2048
PAST_LEN = 256
GRID_W = 64
N_MIXERS = 3
N_HEADS = 8
N_KV_HEADS = 2
HEAD_DIM = 128
Q_PER_KV = N_HEADS // N_KV_HEADS
ROPE_THETA = 10000.0
CONV_W = 3
CHUNK = 128
GMLP_DIM = 2 * D_MODEL
GMLP_GROUPS = 8
GMLP_GROUP_DIM = GMLP_DIM // GMLP_GROUPS
D_FF = 2816
N_EXPERTS = 8
D_FF_EXPERT = 3584
EPS = 1e-6
QKV_OUT = (N_HEADS + 2 * N_KV_HEADS) * HEAD_DIM

N_PROMPT_ROWS = BATCH * SEQ
N_SAMPLE_ROWS = DEC_BATCH * DEC_SEQ
N_ROWS = N_PROMPT_ROWS + N_SAMPLE_ROWS
SUBLANES = 8
N_COND_ROWS = SUBLANES

V7X_VMEM_BYTES = 64 * 1024 * 1024
VMEM_LIMIT_BYTES = V7X_VMEM_BYTES - 8 * 1024 * 1024
NORM_ROWS = 256


def _params(n_axes):
    return pltpu.CompilerParams(
        dimension_semantics=("arbitrary",) * n_axes,
        vmem_limit_bytes=VMEM_LIMIT_BYTES)


def _dot(a, b):
    return jnp.dot(a, b, preferred_element_type=F32)


def _cond_row(i, tm):
    n_prompt = N_PROMPT_ROWS // tm
    per_seq = DEC_SEQ // tm
    return jnp.where(i < n_prompt, 0, 1 + (i - n_prompt) // per_seq)


def _mod_spec(layer, tm):
    return pl.BlockSpec((None, None, 6, D_MODEL),
                        lambda i, *_: (layer, _cond_row(i, tm), 0, 0))


def _gain_spec(layer):
    return pl.BlockSpec((None, 1, D_MODEL), lambda *_: (layer, 0, 0))


def _norm_mod_to(x_ref, g_ref, scale, shift, h_ref):
    gs = g_ref[...] * (1.0 + scale)
    n_chunks = x_ref.shape[0] // NORM_ROWS

    def body(r, carry):
        rows = pl.ds(pl.multiple_of(r * NORM_ROWS, NORM_ROWS), NORM_ROWS)
        x = x_ref[rows, :]
        ms = jnp.mean(x * x, axis=-1, keepdims=True)
        h_ref[rows, :] = (x * lax.rsqrt(ms + EPS) * gs + shift).astype(h_ref.dtype)
        return carry

    lax.fori_loop(0, n_chunks, body, 0)


def _mod_kernel(c_ref, w_ref, b_ref, o_ref):
    cnd = c_ref[...]
    s = (cnd * jax.nn.sigmoid(cnd)).astype(BF16)
    o_ref[...] = _dot(s, w_ref[...].astype(BF16)) + b_ref[...]


def _modulation(cond, w_mod, b_mod):
    tn = 1536
    n_out = 6 * D_MODEL
    out = pl.pallas_call(
        _mod_kernel,
        out_shape=jax.ShapeDtypeStruct((DEPTH, N_COND_ROWS, n_out), F32),
        grid=(DEPTH, n_out // tn),
        in_specs=[
            pl.BlockSpec((N_COND_ROWS, D_MODEL), lambda l, j: (0, 0)),
            pl.BlockSpec((None, D_MODEL, tn), lambda l, j: (l, 0, j)),
            pl.BlockSpec((None, 1, tn), lambda l, j: (l, 0, j)),
        ],
        out_specs=pl.BlockSpec((None, N_COND_ROWS, tn), lambda l, j: (l, 0, j)),
        compiler_params=_params(2),
        name="modulation",
    )(cond, w_mod, b_mod.reshape(DEPTH, 1, n_out))
    return out.reshape(DEPTH, N_COND_ROWS, 6, D_MODEL)


CONV_TM = 2048
CONV_TN = 256


def _conv_in_kernel(x_ref, mod_ref, g_ref, wb_ref, wc_ref, wx_ref, cw_ref, o_ref, h_ref):
    i = pl.program_id(0)

    @pl.when(pl.program_id(1) == 0)
    def _():
        _norm_mod_to(x_ref, g_ref, mod_ref[1:2, :], mod_ref[0:1, :], h_ref)

    h = h_ref[...]
    b = _dot(h, wb_ref[...].astype(BF16))
    z = _dot(h, wc_ref[...].astype(BF16)) * _dot(h, wx_ref[...].astype(BF16))
    seq_len = jnp.where(i < N_PROMPT_ROWS // CONV_TM, SEQ, DEC_SEQ)
    pos = lax.broadcasted_iota(jnp.int32, z.shape, 0) & (seq_len - 1)
    z_prev = jnp.where(pos == 0, 0.0, pltpu.roll(z, 1, 0))
    z_next = jnp.where(pos == seq_len - 1, 0.0, pltpu.roll(z, CONV_TM - 1, 0))
    conv = z_prev * cw_ref[0:1, :] + z * cw_ref[1:2, :] + z_next * cw_ref[2:3, :]
    o_ref[...] = (b * conv).astype(o_ref.dtype)


def _conv_in(x, mod, g_mix, conv_w_in, conv_w, layer, j):
    tm, tn = CONV_TM, CONV_TN
    nb = D_MODEL // tn
    w_spec = lambda part: pl.BlockSpec((None, D_MODEL, tn),
                                       lambda i, n: (j, 0, part * nb + n))
    return pl.pallas_call(
        _conv_in_kernel,
        out_shape=jax.ShapeDtypeStruct((N_ROWS, D_MODEL), BF16),
        grid=(N_ROWS // tm, nb),
        in_specs=[
            pl.BlockSpec((tm, D_MODEL), lambda i, n: (i, 0)),
            _mod_spec(layer, tm),
            _gain_spec(layer),
            w_spec(0), w_spec(1), w_spec(2),
            pl.BlockSpec((None, CONV_W, tn), lambda i, n: (j, 0, n)),
        ],
        out_specs=pl.BlockSpec((tm, tn), lambda i, n: (i, n)),
        scratch_shapes=[pltpu.VMEM((tm, D_MODEL), BF16)],
        compiler_params=_params(2),
        name="conv_in",
    )(x, mod, g_mix, conv_w_in, conv_w_in, conv_w_in, conv_w)


PROJ_TM = 1024


def _proj_res_kernel(a_ref, w_ref, x_ref, mod_ref, o_ref):
    y = _dot(a_ref[...], w_ref[...].astype(BF16))
    o_ref[...] = x_ref[...] + mod_ref[2:3, :] * y


def _proj_res(a, w, j, x, mod, layer):
    tm = PROJ_TM
    k = a.shape[1]
    return pl.pallas_call(
        _proj_res_kernel,
        out_shape=jax.ShapeDtypeStruct((N_ROWS, D_MODEL), F32),
        grid=(N_ROWS // tm,),
        in_specs=[
            pl.BlockSpec((tm, k), lambda i: (i, 0)),
            pl.BlockSpec((None, k, D_MODEL), lambda i: (j, 0, 0)),
            pl.BlockSpec((tm, D_MODEL), lambda i: (i, 0)),
            _mod_spec(layer, tm),
        ],
        out_specs=pl.BlockSpec((tm, D_MODEL), lambda i: (i, 0)),
        compiler_params=_params(1),
        name="proj_res",
    )(a, w, x, mod)


GMLP_TM = 1024
GMLP_TN = 1024
GMLP_OUT_TM = 512


def _gelu_tanh(x):
    c = 0.7978845608028654
    return 0.5 * x * (1.0 + jnp.tanh(c * (x + 0.044715 * (x * x * x))))


def _gmlp_in_kernel(x_ref, mod_ref, g_ref, w_ref, b_ref, o_ref, h_ref):
    @pl.when(pl.program_id(1) == 0)
    def _():
        _norm_mod_to(x_ref, g_ref, mod_ref[1:2, :], mod_ref[0:1, :], h_ref)

    z = _dot(h_ref[...], w_ref[...].astype(BF16)) + b_ref[...]
    o_ref[...] = _gelu_tanh(z).astype(o_ref.dtype)


def _gmlp_in(x, mod, g_mix, w_in, b_in, layer, j):
    tm, tn = GMLP_TM, GMLP_TN
    n_out = 2 * GMLP_DIM
    return pl.pallas_call(
        _gmlp_in_kernel,
        out_shape=jax.ShapeDtypeStruct((N_ROWS, n_out), BF16),
        grid=(N_ROWS // tm, n_out // tn),
        in_specs=[
            pl.BlockSpec((tm, D_MODEL), lambda i, n: (i, 0)),
            _mod_spec(layer, tm),
            _gain_spec(layer),
            pl.BlockSpec((None, D_MODEL, tn), lambda i, n: (j, 0, n)),
            pl.BlockSpec((None, 1, tn), lambda i, n: (j, 0, n)),
        ],
        out_specs=pl.BlockSpec((tm, tn), lambda i, n: (i, n)),
        scratch_shapes=[pltpu.VMEM((tm, D_MODEL), BF16)],
        compiler_params=_params(2),
        name="gmlp_in",
    )(x, mod, g_mix, w_in, b_in.reshape(-1, 1, n_out))


def _gmlp_out_kernel(u_ref, v_ref, lng_ref, lnb_ref, ws_ref, bs_ref, w_ref, x_ref, mod_ref,
                     o_ref, a_ref):
    lng = lng_ref[...]
    lnb = lnb_ref[...]

    def chunk(c, carry):
        rows = pl.ds(pl.multiple_of(c * CHUNK, CHUNK), CHUNK)
        v = v_ref[rows, :].astype(F32)
        mu = jnp.mean(v, axis=-1, keepdims=True)
        vc = v - mu
        var = jnp.mean(vc * vc, axis=-1, keepdims=True)
        vn = (vc * lax.rsqrt(var + EPS) * lng + lnb).astype(BF16)
        for g in range(GMLP_GROUPS):
            cols = slice(g * GMLP_GROUP_DIM, (g + 1) * GMLP_GROUP_DIM)
            mixed = _dot(ws_ref[g].astype(BF16), vn[:, cols]) + bs_ref[:, g:g + 1]
            a_ref[rows, cols] = (u_ref[rows, cols].astype(F32) * mixed).astype(a_ref.dtype)
        return carry

    lax.fori_loop(0, u_ref.shape[0] // CHUNK, chunk, 0)
    y = _dot(a_ref[...], w_ref[...].astype(BF16))
    o_ref[...] = x_ref[...] + mod_ref[2:3, :] * y


def _gmlp_out(z, ln_g, ln_b, w_s, b_s, w_out, x, mod, layer, j):
    tm = GMLP_OUT_TM
    return pl.pallas_call(
        _gmlp_out_kernel,
        out_shape=jax.ShapeDtypeStruct((N_ROWS, D_MODEL), F32),
        grid=(N_ROWS // tm,),
        in_specs=[
            pl.BlockSpec((tm, GMLP_DIM), lambda i: (i, 0)),
            pl.BlockSpec((tm, GMLP_DIM), lambda i: (i, 1)),
            pl.BlockSpec((None, 1, GMLP_DIM), lambda i: (j, 0, 0)),
            pl.BlockSpec((None, 1, GMLP_DIM), lambda i: (j, 0, 0)),
            pl.BlockSpec((None, GMLP_GROUPS, CHUNK, CHUNK), lambda i: (j, 0, 0, 0)),
            pl.BlockSpec((None, CHUNK, GMLP_GROUPS), lambda i: (j, 0, 0)),
            pl.BlockSpec((None, GMLP_DIM, D_MODEL), lambda i: (j, 0, 0)),
            pl.BlockSpec((tm, D_MODEL), lambda i: (i, 0)),
            _mod_spec(layer, tm),
        ],
        out_specs=pl.BlockSpec((tm, D_MODEL), lambda i: (i, 0)),
        scratch_shapes=[pltpu.VMEM((tm, GMLP_DIM), BF16)],
        compiler_params=_params(1),
        name="gmlp_out",
    )(z, z, ln_g.reshape(-1, 1, GMLP_DIM), ln_b.reshape(-1, 1, GMLP_DIM), w_s,
      jnp.swapaxes(b_s, 1, 2), w_out, x, mod)


QKV_TM = 512
ATTN_TQ = 256


def _rope_tables(tm):
    pos = jnp.arange(DEC_SEQ)
    row = (pos // GRID_W).astype(F32)
    col = (pos % GRID_W).astype(F32)
    n_freq = HEAD_DIM // 4
    inv_freq = jnp.power(jnp.float32(ROPE_THETA), -jnp.arange(n_freq, dtype=F32) / n_freq)
    ang = jnp.concatenate([row[:, None] * inv_freq, col[:, None] * inv_freq], axis=-1)
    cos = jnp.repeat(jnp.cos(ang), 2, axis=-1)
    sign = jnp.tile(jnp.array([-1.0, 1.0], F32), HEAD_DIM // 2)
    sin = jnp.repeat(jnp.sin(ang), 2, axis=-1) * sign
    cos = jnp.concatenate([jnp.ones((tm, HEAD_DIM), F32), cos], axis=0)
    sin = jnp.concatenate([jnp.zeros((tm, HEAD_DIM), F32), sin], axis=0)
    return cos, sin


def _qkv_kernel(x_ref, mod_ref, g_ref, w_ref, qg_ref, kg_ref, cos_ref, sin_ref, o_ref, h_ref):
    _norm_mod_to(x_ref, g_ref, mod_ref[1:2, :], mod_ref[0:1, :], h_ref)
    y = _dot(h_ref[...], w_ref[...].astype(BF16))
    cos = cos_ref[...]
    sin = sin_ref[...]
    even = (lax.broadcasted_iota(jnp.int32, cos.shape, 1) & 1) == 0
    for hd in range(N_HEADS + N_KV_HEADS):
        cols = slice(hd * HEAD_DIM, (hd + 1) * HEAD_DIM)
        yh = y[:, cols]
        gain = qg_ref[...] if hd < N_HEADS else kg_ref[...]
        ms = jnp.mean(yh * yh, axis=-1, keepdims=True)
        yn = yh * lax.rsqrt(ms + EPS) * gain
        partner = jnp.where(even, pltpu.roll(yn, HEAD_DIM - 1, 1), pltpu.roll(yn, 1, 1))
        o_ref[:, cols] = yn * cos + partner * sin
    v_cols = slice((N_HEADS + N_KV_HEADS) * HEAD_DIM, QKV_OUT)
    o_ref[:, v_cols] = y[:, v_cols]


def _qkv(x, mod, g_mix, w_qkv, q_g, k_g, layer, j):
    tm = QKV_TM
    cos, sin = _rope_tables(tm)
    n_prompt = N_PROMPT_ROWS // tm
    per_seq = DEC_SEQ // tm
    tbl = lambda i: (jnp.where(i < n_prompt, 0, 1 + (i - n_prompt) % per_seq), 0)
    return pl.pallas_call(
        _qkv_kernel,
        out_shape=jax.ShapeDtypeStruct((N_ROWS, QKV_OUT), F32),
        grid=(N_ROWS // tm,),
        in_specs=[
            pl.BlockSpec((tm, D_MODEL), lambda i: (i, 0)),
            _mod_spec(layer, tm),
            _gain_spec(layer),
            pl.BlockSpec((None, D_MODEL, QKV_OUT), lambda i: (j, 0, 0)),
            pl.BlockSpec((None, 1, HEAD_DIM), lambda i: (j, 0, 0)),
            pl.BlockSpec((None, 1, HEAD_DIM), lambda i: (j, 0, 0)),
            pl.BlockSpec((tm, HEAD_DIM), tbl),
            pl.BlockSpec((tm, HEAD_DIM), tbl),
        ],
        out_specs=pl.BlockSpec((tm, QKV_OUT), lambda i: (i, 0)),
        scratch_shapes=[pltpu.VMEM((tm, D_MODEL), BF16)],
        compiler_params=_params(1),
        name="qkv",
    )(x, mod, g_mix, w_qkv, q_g.reshape(-1, 1, HEAD_DIM), k_g.reshape(-1, 1, HEAD_DIM), cos, sin)


def _attn_kernel(*refs, has_ctx):
    if has_ctx:
        q_ref, k_ref, v_ref, ck_ref, cv_ref, o_ref = refs
    else:
        q_ref, k_ref, v_ref, o_ref = refs
    nt = (((1,), (1,)), ((), ()))
    k = k_ref[...].astype(BF16)
    v = v_ref[...].astype(BF16)
    if has_ctx:
        ck = ck_ref[...].astype(BF16)
        cv = cv_ref[...].astype(BF16)
    for g in range(Q_PER_KV):
        cols = slice(g * HEAD_DIM, (g + 1) * HEAD_DIM)
        q = (q_ref[:, cols] * (HEAD_DIM ** -0.5)).astype(BF16)
        s = lax.dot_general(q, k, nt, preferred_element_type=F32)
        m = jnp.max(s, axis=-1, keepdims=True)
        if has_ctx:
            sc = lax.dot_general(q, ck, nt, preferred_element_type=F32)
            m = jnp.maximum(m, jnp.max(sc, axis=-1, keepdims=True))
        p = jnp.exp(s - m)
        den = jnp.sum(p, axis=-1, keepdims=True)
        o = _dot(p.astype(BF16), v)
        if has_ctx:
            pc = jnp.exp(sc - m)
            den = den + jnp.sum(pc, axis=-1, keepdims=True)
            o = o + _dot(pc.astype(BF16), cv)
        o_ref[:, cols] = (o / den).astype(o_ref.dtype)


def _attention(qkv, ctx_k, ctx_v):
    group_w = Q_PER_KV * HEAD_DIM
    k_col = N_HEADS
    v_col = N_HEADS + N_KV_HEADS

    def call(n_batch, seq, tq, row0, ctx):
        q_blocks = seq // tq
        o_row = lambda b, qi: b * q_blocks + qi
        kv_row = lambda b: row0 // seq + b
        in_specs = [
            pl.BlockSpec((tq, group_w), lambda b, kh, qi: (row0 // tq + o_row(b, qi), kh)),
            pl.BlockSpec((seq, HEAD_DIM), lambda b, kh, qi: (kv_row(b), k_col + kh)),
            pl.BlockSpec((seq, HEAD_DIM), lambda b, kh, qi: (kv_row(b), v_col + kh)),
        ]
        args = [qkv, qkv, qkv]
        if ctx is not None:
            ctx_spec = pl.BlockSpec((None, PAST_LEN, HEAD_DIM), lambda b, kh, qi: (b, 0, kh))
            in_specs += [ctx_spec, ctx_spec]
            args += list(ctx)
        return pl.pallas_call(
            functools.partial(_attn_kernel, has_ctx=ctx is not None),
            out_shape=jax.ShapeDtypeStruct((n_batch * seq, N_HEADS * HEAD_DIM), BF16),
            grid=(n_batch, N_KV_HEADS, q_blocks),
            in_specs=in_specs,
            out_specs=pl.BlockSpec((tq, group_w), lambda b, kh, qi: (o_row(b, qi), kh)),
            compiler_params=_params(3),
            name="attention",
        )(*args)

    o_prompt = call(BATCH, SEQ, SEQ, 0, None)
    o_sample = call(DEC_BATCH, DEC_SEQ, ATTN_TQ, N_PROMPT_ROWS, (ctx_k, ctx_v))
    return jnp.concatenate([o_prompt, o_sample], axis=0)


FFN_TM = 1024
FFN_TF = 256
TOP_K = 2
ROUTER_TM = 1024
ROUTE_LANES = 128
MOE_TF = 512
MOE_TILE = 1024
MOE_SUB = 256
MOE_MAX_TILES = TOP_K * N_ROWS // MOE_TILE + N_EXPERTS
MOE_ROWS = MOE_MAX_TILES * MOE_TILE
SCATTER_TM = 512
COMBINE_TM = 256
WIN_CHUNK = 128
WIN_CHUNKS = COMBINE_TM // WIN_CHUNK + 1


def _swiglu_hidden(h, wg, wu):
    gate = _dot(h, wg)
    return (gate * jax.nn.sigmoid(gate) * _dot(h, wu)).astype(BF16)


def _ffn_kernel(x_ref, mod_ref, g_ref, wg_ref, wu_ref, wd_ref, o_ref, h_ref, acc_ref):
    f = pl.program_id(1)

    @pl.when(f == 0)
    def _():
        _norm_mod_to(x_ref, g_ref, mod_ref[4:5, :], mod_ref[3:4, :], h_ref)
        acc_ref[...] = jnp.zeros_like(acc_ref)

    a = _swiglu_hidden(h_ref[...], wg_ref[...].astype(BF16), wu_ref[...].astype(BF16))
    acc_ref[...] += _dot(a, wd_ref[...].astype(BF16))

    @pl.when(f == pl.num_programs(1) - 1)
    def _():
        o_ref[...] = x_ref[...] + mod_ref[5:6, :] * acc_ref[...]


def _ffn(x, mod, g_ffn, w_gate, w_up, w_down, layer, li):
    tm, tf = FFN_TM, FFN_TF
    return pl.pallas_call(
        _ffn_kernel,
        out_shape=jax.ShapeDtypeStruct((N_ROWS, D_MODEL), F32),
        grid=(N_ROWS // tm, D_FF // tf),
        in_specs=[
            pl.BlockSpec((tm, D_MODEL), lambda i, f: (i, 0)),
            _mod_spec(layer, tm),
            _gain_spec(layer),
            pl.BlockSpec((None, D_MODEL, tf), lambda i, f: (li, 0, f)),
            pl.BlockSpec((None, D_MODEL, tf), lambda i, f: (li, 0, f)),
            pl.BlockSpec((None, tf, D_MODEL), lambda i, f: (li, f, 0)),
        ],
        out_specs=pl.BlockSpec((tm, D_MODEL), lambda i, f: (i, 0)),
        scratch_shapes=[pltpu.VMEM((tm, D_MODEL), BF16), pltpu.VMEM((tm, D_MODEL), F32)],
        compiler_params=_params(2),
        name="ffn",
    )(x, mod, g_ffn, w_gate, w_up, w_down)


def _split_bf16(x):
    hi = x.astype(BF16)
    return hi, (x - hi.astype(F32)).astype(BF16)


def _router_kernel(x_ref, mod_ref, g_ref, w_ref, route_ref, cnt_ref, base_ref, h_ref, seen_ref):
    @pl.when(pl.program_id(0) == 0)
    def _():
        seen_ref[...] = jnp.zeros_like(seen_ref)

    _norm_mod_to(x_ref, g_ref, mod_ref[4:5, :], mod_ref[3:4, :], h_ref)
    h_hi, h_lo = _split_bf16(h_ref[...])
    w_hi, w_lo = _split_bf16(w_ref[...])
    logits = _dot(h_hi, w_hi) + (_dot(h_hi, w_lo) + _dot(h_lo, w_hi))
    lane = lax.broadcasted_iota(jnp.int32, logits.shape, 1)
    logits = jnp.where(lane < N_EXPERTS, logits, -jnp.inf)
    m1 = jnp.max(logits, axis=-1, keepdims=True)
    i1 = jnp.min(jnp.where(logits == m1, lane, N_EXPERTS), axis=-1, keepdims=True)
    rest = jnp.where(lane == i1, -jnp.inf, logits)
    m2 = jnp.max(rest, axis=-1, keepdims=True)
    i2 = jnp.min(jnp.where(rest == m2, lane, N_EXPERTS), axis=-1, keepdims=True)
    e2 = jnp.exp(m2 - m1)
    w1 = 1.0 / (1.0 + e2)
    chosen = jnp.where((lane == i1) | (lane == i2), 1.0, 0.0)
    tm = chosen.shape[0]
    earlier = (lax.broadcasted_iota(jnp.int32, (tm, tm), 1)
               < lax.broadcasted_iota(jnp.int32, (tm, tm), 0))
    prior = _dot(jnp.where(earlier, 1.0, 0.0).astype(BF16), chosen.astype(BF16)) + seen_ref[...]
    rank1 = jnp.sum(jnp.where(lane == i1, prior, 0.0), axis=-1, keepdims=True)
    rank2 = jnp.sum(jnp.where(lane == i2, prior, 0.0), axis=-1, keepdims=True)
    seen_ref[...] += jnp.sum(chosen, axis=0, keepdims=True)
    route = jnp.zeros_like(logits)
    for k, field in enumerate((w1, e2 * w1, i1.astype(F32), i2.astype(F32), rank1, rank2)):
        route = jnp.where(lane == k, field, route)
    route_ref[...] = route
    cnt_ref[...] = jnp.broadcast_to(seen_ref[...], cnt_ref.shape)
    base_ref[...] = jnp.zeros_like(base_ref)
    for k in range(tm // COMBINE_TM):
        base_ref[k:k + 1, :] = prior[k * COMBINE_TM:k * COMBINE_TM + 1, :]


def _router(x, mod, g_ffn, w_router, layer, li):
    tm = ROUTER_TM
    w_pad = jnp.pad(w_router, ((0, 0), (0, 0), (0, ROUTE_LANES - N_EXPERTS)))
    return pl.pallas_call(
        _router_kernel,
        out_shape=(jax.ShapeDtypeStruct((N_ROWS, ROUTE_LANES), F32),
                   jax.ShapeDtypeStruct((SUBLANES, ROUTE_LANES), F32),
                   jax.ShapeDtypeStruct((N_ROWS // tm, SUBLANES, ROUTE_LANES), F32)),
        grid=(N_ROWS // tm,),
        in_specs=[
            pl.BlockSpec((tm, D_MODEL), lambda i: (i, 0)),
            _mod_spec(layer, tm),
            _gain_spec(layer),
            pl.BlockSpec((None, D_MODEL, ROUTE_LANES), lambda i: (li, 0, 0)),
        ],
        out_specs=(pl.BlockSpec((tm, ROUTE_LANES), lambda i: (i, 0)),
                   pl.BlockSpec((SUBLANES, ROUTE_LANES), lambda i: (0, 0)),
                   pl.BlockSpec((None, SUBLANES, ROUTE_LANES), lambda i: (i, 0, 0))),
        scratch_shapes=[pltpu.VMEM((tm, D_MODEL), F32), pltpu.VMEM((1, ROUTE_LANES), F32)],
        compiler_params=_params(1),
        name="router",
    )(x, mod, g_ffn, w_pad)


def _dispatch_plan(route, cnt, tile_base):
    counts = cnt[0, :N_EXPERTS].astype(jnp.int32)
    e1, e2, r1, r2 = (route[:, k].astype(jnp.int32) for k in (2, 3, 4, 5))
    n_tiles = (counts + MOE_TILE - 1) // MOE_TILE
    tile_end = jnp.cumsum(n_tiles)
    tile_start = tile_end - n_tiles
    row_start = tile_start * MOE_TILE
    pos = jnp.concatenate([row_start[e1] + r1, row_start[e2] + r2])
    n_active = tile_end[-1]
    t = jnp.arange(MOE_MAX_TILES, dtype=jnp.int32)
    tile_block = jnp.minimum(t, n_active - 1)
    tile_expert = jnp.sum(tile_block[:, None] >= tile_end[None, :], axis=1).astype(jnp.int32)
    valid = counts[tile_expert] - (tile_block - tile_start[tile_expert]) * MOE_TILE
    n_sub = (jnp.clip(valid, 0, MOE_TILE) + MOE_SUB - 1) // MOE_SUB
    tile_nsub = jnp.where(t < n_active, n_sub, 0).astype(jnp.int32)
    pad = jnp.concatenate([row_start + counts, (-counts) % MOE_SUB]).astype(jnp.int32)
    base = tile_base[:, :ROUTER_TM // COMBINE_TM, :N_EXPERTS].astype(jnp.int32)
    base = base.reshape(N_ROWS // COMBINE_TM, N_EXPERTS)
    lo = row_start[None, :] + base
    hi = row_start[None, :] + jnp.concatenate([base[1:], counts[None, :]], axis=0)
    win_chunk0 = lo // WIN_CHUNK
    win_chunks = jnp.where(hi > lo, (hi + WIN_CHUNK - 1) // WIN_CHUNK - win_chunk0, 0)
    return dict(pos=pos, pad=pad, tile_expert=tile_expert, tile_nsub=tile_nsub,
                tile_block=tile_block, row_start=row_start.astype(jnp.int32),
                win_chunk0=win_chunk0.reshape(-1).astype(jnp.int32),
                win_chunks=win_chunks.reshape(-1).astype(jnp.int32))


def _for_rows(n_rows, body):
    def group(j, carry):
        r0 = pl.multiple_of(j * SUBLANES, SUBLANES)
        for u in range(SUBLANES):
            body(r0 + u)
        return carry
    lax.fori_loop(0, n_rows // SUBLANES, group, 0)


def _scatter_kernel(pos_ref, pad_ref, tn_ref, x_ref, mod_ref, g_ref, xs_ref, h_ref, zero_ref,
                    sem, fill_sem):
    i = pl.program_id(0)
    ts = x_ref.shape[0]
    _norm_mod_to(x_ref, g_ref, mod_ref[4:5, :], mod_ref[3:4, :], h_ref)

    first = i * ts

    def row_copy(r, k):
        dst = pos_ref[k * N_ROWS + first + r]
        return pltpu.make_async_copy(h_ref.at[pl.ds(r, 1)], xs_ref.at[pl.ds(dst, 1)], sem)

    def start_row(r):
        for k in range(TOP_K):
            row_copy(r, k).start(priority=k)

    def wait_row(r):
        for k in range(TOP_K):
            row_copy(r, k).wait()

    _for_rows(ts, start_row)

    @pl.when(i == 0)
    def _():
        zero_ref[...] = jnp.zeros_like(zero_ref)

        def fills(act):
            for e in range(N_EXPERTS):
                def row_body(r, carry, e=e):
                    act(pltpu.make_async_copy(zero_ref.at[pl.ds(0, 1)],
                                              xs_ref.at[pl.ds(pad_ref[e] + r, 1)], fill_sem))
                    return carry
                lax.fori_loop(0, pad_ref[N_EXPERTS + e], row_body, 0)

            def tile_body(t, carry):
                def sub_body(s, carry):
                    row = pl.multiple_of(t * MOE_TILE + s * MOE_SUB, MOE_SUB)
                    act(pltpu.make_async_copy(zero_ref, xs_ref.at[pl.ds(row, MOE_SUB)], fill_sem))
                    return carry
                return lax.fori_loop(tn_ref[t], MOE_TILE // MOE_SUB, sub_body, carry)
            lax.fori_loop(0, MOE_MAX_TILES, tile_body, 0)

        fills(lambda cp: cp.start())
        fills(lambda cp: cp.wait())

    _for_rows(ts, wait_row)


def _moe_scatter(x, mod, g_ffn, plan, layer):
    ts = SCATTER_TM
    grid_spec = pltpu.PrefetchScalarGridSpec(
        num_scalar_prefetch=3,
        grid=(N_ROWS // ts,),
        in_specs=[
            pl.BlockSpec((ts, D_MODEL), lambda i, *_: (i, 0)),
            _mod_spec(layer, ts),
            _gain_spec(layer),
        ],
        out_specs=pl.BlockSpec(memory_space=pl.ANY),
        scratch_shapes=[pltpu.VMEM((ts, D_MODEL), F32),
                        pltpu.VMEM((MOE_SUB, D_MODEL), F32),
                        pltpu.SemaphoreType.DMA(()), pltpu.SemaphoreType.DMA(())],
    )
    return pl.pallas_call(
        _scatter_kernel,
        out_shape=jax.ShapeDtypeStruct((MOE_ROWS, D_MODEL), F32),
        grid_spec=grid_spec,
        compiler_params=_params(1),
        name="moe_scatter",
    )(plan["pos"], plan["pad"], plan["tile_nsub"], x, mod, g_ffn)


def _moe_kernel(te_ref, tn_ref, tb_ref, x_ref, wg_ref, wu_ref, wd_ref, o_ref,
                xb_ref, acc_ref, wgb_ref, wub_ref, wdb_ref):
    n_sub = tn_ref[pl.program_id(0)]
    first_f = pl.program_id(1) == 0
    full = MOE_TILE // MOE_SUB

    def sub_rows(s):
        return pl.ds(pl.multiple_of(s * MOE_SUB, MOE_SUB), MOE_SUB)

    @pl.when(first_f)
    def _():
        acc_ref[...] = jnp.zeros_like(acc_ref)

    @pl.when(n_sub == full)
    def _():
        @pl.when(first_f)
        def _():
            xb_ref[...] = x_ref[...].astype(BF16)

        a = _swiglu_hidden(xb_ref[...], wg_ref[...].astype(BF16), wu_ref[...].astype(BF16))
        acc_ref[...] += _dot(a, wd_ref[...].astype(BF16))

    @pl.when((n_sub > 0) & (n_sub < full))
    def _():
        @pl.when(first_f)
        def _():
            def cast(s, carry):
                xb_ref[sub_rows(s), :] = x_ref[sub_rows(s), :].astype(BF16)
                return carry

            lax.fori_loop(0, n_sub, cast, 0)

        wgb_ref[...] = wg_ref[...].astype(BF16)
        wub_ref[...] = wu_ref[...].astype(BF16)
        wdb_ref[...] = wd_ref[...].astype(BF16)

        def block(s, carry):
            a = _swiglu_hidden(xb_ref[sub_rows(s), :], wgb_ref[...], wub_ref[...])
            acc_ref[sub_rows(s), :] += _dot(a, wdb_ref[...])
            return carry

        lax.fori_loop(0, n_sub, block, 0)

    @pl.when(pl.program_id(1) == pl.num_programs(1) - 1)
    def _():
        o_ref[...] = acc_ref[...].astype(o_ref.dtype)


def _moe_experts(xs, plan, w_gate, w_up, w_down, li):
    tile, tf = MOE_TILE, MOE_TF
    nf = D_FF_EXPERT // tf
    frozen_f = lambda i, f, tn: jnp.where(tn[i] > 0, f, nf - 1)
    grid_spec = pltpu.PrefetchScalarGridSpec(
        num_scalar_prefetch=3,
        grid=(MOE_MAX_TILES, nf),
        in_specs=[
            pl.BlockSpec((tile, D_MODEL), lambda i, f, te, tn, tb: (tb[i], 0)),
            pl.BlockSpec((None, None, D_MODEL, tf),
                         lambda i, f, te, tn, tb: (li, te[i], 0, frozen_f(i, f, tn))),
            pl.BlockSpec((None, None, D_MODEL, tf),
                         lambda i, f, te, tn, tb: (li, te[i], 0, frozen_f(i, f, tn))),
            pl.BlockSpec((None, None, tf, D_MODEL),
                         lambda i, f, te, tn, tb: (li, te[i], frozen_f(i, f, tn), 0)),
        ],
        out_specs=pl.BlockSpec((tile, D_MODEL), lambda i, f, te, tn, tb: (i, 0)),
        scratch_shapes=[pltpu.VMEM((tile, D_MODEL), BF16), pltpu.VMEM((tile, D_MODEL), F32),
                        pltpu.VMEM((D_MODEL, tf), BF16), pltpu.VMEM((D_MODEL, tf), BF16),
                        pltpu.VMEM((tf, D_MODEL), BF16)],
    )
    return pl.pallas_call(
        _moe_kernel,
        out_shape=jax.ShapeDtypeStruct((MOE_ROWS, D_MODEL), BF16),
        grid_spec=grid_spec,
        compiler_params=_params(2),
        name="moe_experts",
    )(plan["tile_expert"], plan["tile_nsub"], plan["tile_block"], xs, w_gate, w_up, w_down)


def _combine_kernel(start_ref, c0_ref, n_ref, x_ref, mod_ref, route_ref, ys_ref, o_ref,
                    win_ref, sem):
    i = pl.program_id(0)
    slot = lax.rem(i, 2)

    def chunk_slots(tile):
        for e in range(N_EXPERTS):
            c0 = c0_ref[tile * N_EXPERTS + e]
            n = n_ref[tile * N_EXPERTS + e]
            for j in range(WIN_CHUNKS):
                yield e * WIN_CHUNKS + j, c0 + j, j < n

    def window_copies(tile, tile_slot, act):
        for s, chunk, needed in chunk_slots(tile):
            @pl.when(needed)
            def _(s=s, chunk=chunk):
                rows = pl.ds(pl.multiple_of(chunk * WIN_CHUNK, WIN_CHUNK), WIN_CHUNK)
                act(pltpu.make_async_copy(ys_ref.at[rows],
                                          win_ref.at[tile_slot, pl.ds(s * WIN_CHUNK, WIN_CHUNK)],
                                          sem.at[tile_slot]))

    @pl.when(i == 0)
    def _():
        win_ref[...] = jnp.zeros_like(win_ref)
        window_copies(i, slot, lambda cp: cp.start())

    @pl.when(i + 1 < pl.num_programs(0))
    def _():
        window_copies(i + 1, 1 - slot, lambda cp: cp.start())

    window_copies(i, slot, lambda cp: cp.wait())

    w1, w2, e1, e2, pos1, pos2 = (route_ref[:, k:k + 1] for k in range(6))
    for e in range(N_EXPERTS):
        first_row = start_ref[e].astype(F32)
        pos1 = pos1 + jnp.where(e1 == e, first_row, 0.0)
        pos2 = pos2 + jnp.where(e2 == e, first_row, 0.0)
    lane = lax.broadcasted_iota(jnp.int32, (x_ref.shape[0], WIN_CHUNK), 1).astype(F32)
    blocks = []
    for s, chunk, needed in chunk_slots(i):
        row0 = jnp.where(needed, chunk * WIN_CHUNK, -MOE_ROWS).astype(F32)
        rows = lane + row0
        weight = jnp.where(pos1 == rows, w1, 0.0) + jnp.where(pos2 == rows, w2, 0.0)
        blocks.append(weight.astype(BF16))
    y = _dot(jnp.concatenate(blocks, axis=1), win_ref[slot])
    o_ref[...] = x_ref[...] + mod_ref[5:6, :] * y


def _moe_combine(x, mod, route, ys, plan, layer):
    tc = COMBINE_TM
    grid_spec = pltpu.PrefetchScalarGridSpec(
        num_scalar_prefetch=3,
        grid=(N_ROWS // tc,),
        in_specs=[
            pl.BlockSpec((tc, D_MODEL), lambda i, *_: (i, 0)),
            _mod_spec(layer, tc),
            pl.BlockSpec((tc, ROUTE_LANES), lambda i, *_: (i, 0)),
            pl.BlockSpec(memory_space=pl.ANY),
        ],
        out_specs=pl.BlockSpec((tc, D_MODEL), lambda i, *_: (i, 0)),
        scratch_shapes=[pltpu.VMEM((2, N_EXPERTS * WIN_CHUNKS * WIN_CHUNK, D_MODEL), BF16),
                        pltpu.SemaphoreType.DMA((2,))],
    )
    return pl.pallas_call(
        _combine_kernel,
        out_shape=jax.ShapeDtypeStruct((N_ROWS, D_MODEL), F32),
        grid_spec=grid_spec,
        compiler_params=_params(1),
        name="moe_combine",
    )(plan["row_start"], plan["win_chunk0"], plan["win_chunks"], x, mod, route, ys)


def kernel(x_prompt, x_sample, cache_k, cache_v, c, c_ctx, w_mod, b_mod, g_mix, g_ffn,
           conv_w_in, conv_w, conv_w_out,
           gmlp_w_in, gmlp_b_in, gmlp_ln_g, gmlp_ln_b, gmlp_w_s, gmlp_b_s, gmlp_w_out,
           attn_w_qkv, attn_q_g, attn_k_g, attn_w_o,
           ffn_w_gate, ffn_w_up, ffn_w_down,
           moe_w_router, moe_w_gate, moe_w_up, moe_w_down):
    x = jnp.concatenate([x_prompt.reshape(N_PROMPT_ROWS, D_MODEL),
                         x_sample.reshape(N_SAMPLE_ROWS, D_MODEL)], axis=0)
    cond = jnp.zeros((N_COND_ROWS, D_MODEL), F32).at[0].set(c_ctx).at[1:1 + DEC_BATCH].set(c)
    mod = _modulation(cond, w_mod, b_mod)
    g_mix3 = g_mix.reshape(DEPTH, 1, D_MODEL)
    g_ffn3 = g_ffn.reshape(DEPTH, 1, D_MODEL)
    n_attn = cache_k.shape[1]
    ctx_k = cache_k.reshape(DEC_BATCH, n_attn, PAST_LEN, N_KV_HEADS * HEAD_DIM)
    ctx_v = cache_v.reshape(DEC_BATCH, n_attn, PAST_LEN, N_KV_HEADS * HEAD_DIM)

    new_k, new_v = [], []
    for layer in range(DEPTH):
        kind, j = layer % N_MIXERS, layer // N_MIXERS
        if kind == 0:
            a = _conv_in(x, mod, g_mix3, conv_w_in, conv_w, layer, j)
            x = _proj_res(a, conv_w_out, j, x, mod, layer)
        elif kind == 1:
            z = _gmlp_in(x, mod, g_mix3, gmlp_w_in, gmlp_b_in, layer, j)
            x = _gmlp_out(z, gmlp_ln_g, gmlp_ln_b, gmlp_w_s, gmlp_b_s, gmlp_w_out, x, mod,
                          layer, j)
        else:
            qkv = _qkv(x, mod, g_mix3, attn_w_qkv, attn_q_g, attn_k_g, layer, j)
            k_cols = slice(N_HEADS * HEAD_DIM, (N_HEADS + N_KV_HEADS) * HEAD_DIM)
            v_cols = slice((N_HEADS + N_KV_HEADS) * HEAD_DIM, QKV_OUT)
            new_k.append(qkv[:N_PROMPT_ROWS, k_cols].reshape(BATCH, SEQ, N_KV_HEADS, HEAD_DIM))
            new_v.append(qkv[:N_PROMPT_ROWS, v_cols].reshape(BATCH, SEQ, N_KV_HEADS, HEAD_DIM))
            a = _attention(qkv, ctx_k[:, j], ctx_v[:, j])
            x = _proj_res(a, attn_w_o, j, x, mod, layer)
        li = layer // 2
        if layer % 2 == 0:
            x = _ffn(x, mod, g_ffn3, ffn_w_gate, ffn_w_up, ffn_w_down, layer, li)
        else:
            route, cnt, tile_base = _router(x, mod, g_ffn3, moe_w_router, layer, li)
            plan = _dispatch_plan(route, cnt, tile_base)
            xs = _moe_scatter(x, mod, g_ffn3, plan, layer)
            ys = _moe_experts(xs, plan, moe_w_gate, moe_w_up, moe_w_down, li)
            x = _moe_combine(x, mod, route, ys, plan, layer)

    y_prompt = x[:N_PROMPT_ROWS].reshape(BATCH, SEQ, D_MODEL)
    y_sample = x[N_PROMPT_ROWS:].reshape(DEC_BATCH, DEC_SEQ, D_MODEL)
    return (y_prompt, y_sample, jnp.stack(new_k, axis=1), jnp.stack(new_v, axis=1))
```

```python
import functools

import jax
import jax.numpy as jnp
import numpy as np
from jax import lax
from jax.experimental import pallas as pl
from jax.experimental.pallas import tpu as pltpu

F32 = jnp.float32
BF16 = jnp.bfloat16

D_MODEL = 1024
BATCH = 16
SEQ = 256
DEPTH = 4
DEC_BATCH = 2
DEC_SEQ = 2048
PAST_LEN = 256
GRID_W = 64
N_MIXERS = 3
N_HEADS = 8
N_KV_HEADS = 2
HEAD_DIM = 128
Q_PER_KV = N_HEADS // N_KV_HEADS
ROPE_THETA = 10000.0
CONV_W = 3
CHUNK = 128
GMLP_DIM = 2 * D_MODEL
GMLP_GROUPS = 8
GMLP_GROUP_DIM = GMLP_DIM // GMLP_GROUPS
D_FF = 2816
N_EXPERTS = 8
D_FF_EXPERT = 3584
EPS = 1e-6
QKV_OUT = (N_HEADS + 2 * N_KV_HEADS) * HEAD_DIM

N_PROMPT_ROWS = BATCH * SEQ
N_SAMPLE_ROWS = DEC_BATCH * DEC_SEQ
N_ROWS = N_PROMPT_ROWS + N_SAMPLE_ROWS
SUBLANES = 8
N_COND_ROWS = SUBLANES

V7X_VMEM_BYTES = 64 * 1024 * 1024
VMEM_LIMIT_BYTES = V7X_VMEM_BYTES - 8 * 1024 * 1024
NORM_ROWS = 256


def _params(n_axes):
    return pltpu.CompilerParams(
        dimension_semantics=("arbitrary",) * n_axes,
        vmem_limit_bytes=VMEM_LIMIT_BYTES)


def _dot(a, b):
    return jnp.dot(a, b, preferred_element_type=F32)


def _cond_row(i, tm):
    n_prompt = N_PROMPT_ROWS // tm
    per_seq = DEC_SEQ // tm
    return jnp.where(i < n_prompt, 0, 1 + (i - n_prompt) // per_seq)


def _mod_spec(layer, tm):
    return pl.BlockSpec((None, None, 6, D_MODEL),
                        lambda i, *_: (layer, _cond_row(i, tm), 0, 0))


def _gain_spec(layer):
    return pl.BlockSpec((None, 1, D_MODEL), lambda *_: (layer, 0, 0))


def _norm_mod_to(x_ref, g_ref, scale, shift, h_ref):
    gs = g_ref[...] * (1.0 + scale)
    n_chunks = x_ref.shape[0] // NORM_ROWS

    def body(r, carry):
        rows = pl.ds(pl.multiple_of(r * NORM_ROWS, NORM_ROWS), NORM_ROWS)
        x = x_ref[rows, :]
        ms = jnp.mean(x * x, axis=-1, keepdims=True)
        h_ref[rows, :] = (x * lax.rsqrt(ms + EPS) * gs + shift).astype(h_ref.dtype)
        return carry

    lax.fori_loop(0, n_chunks, body, 0)


def _mod_kernel(c_ref, w_ref, b_ref, o_ref):
    cnd = c_ref[...]
    s = (cnd * jax.nn.sigmoid(cnd)).astype(BF16)
    o_ref[...] = _dot(s, w_ref[...].astype(BF16)) + b_ref[...]


def _modulation(cond, w_mod, b_mod):
    tn = 1536
    n_out = 6 * D_MODEL
    out = pl.pallas_call(
        _mod_kernel,
        out_shape=jax.ShapeDtypeStruct((DEPTH, N_COND_ROWS, n_out), F32),
        grid=(DEPTH, n_out // tn),
        in_specs=[
            pl.BlockSpec((N_COND_ROWS, D_MODEL), lambda l, j: (0, 0)),
            pl.BlockSpec((None, D_MODEL, tn), lambda l, j: (l, 0, j)),
            pl.BlockSpec((None, 1, tn), lambda l, j: (l, 0, j)),
        ],
        out_specs=pl.BlockSpec((None, N_COND_ROWS, tn), lambda l, j: (l, 0, j)),
        compiler_params=_params(2),
        name="modulation",
    )(cond, w_mod, b_mod.reshape(DEPTH, 1, n_out))
    return out.reshape(DEPTH, N_COND_ROWS, 6, D_MODEL)


CONV_TM = 2048
CONV_TN = 256


def _conv_in_kernel(x_ref, mod_ref, g_ref, wb_ref, wc_ref, wx_ref, cw_ref, o_ref, h_ref):
    i = pl.program_id(0)

    @pl.when(pl.program_id(1) == 0)
    def _():
        _norm_mod_to(x_ref, g_ref, mod_ref[1:2, :], mod_ref[0:1, :], h_ref)

    h = h_ref[...]
    b = _dot(h, wb_ref[...].astype(BF16))
    z = _dot(h, wc_ref[...].astype(BF16)) * _dot(h, wx_ref[...].astype(BF16))
    seq_len = jnp.where(i < N_PROMPT_ROWS // CONV_TM, SEQ, DEC_SEQ)
    pos = lax.broadcasted_iota(jnp.int32, z.shape, 0) & (seq_len - 1)
    z_prev = jnp.where(pos == 0, 0.0, pltpu.roll(z, 1, 0))
    z_next = jnp.where(pos == seq_len - 1, 0.0, pltpu.roll(z, CONV_TM - 1, 0))
    conv = z_prev * cw_ref[0:1, :] + z * cw_ref[1:2, :] + z_next * cw_ref[2:3, :]
    o_ref[...] = (b * conv).astype(o_ref.dtype)


def _conv_in(x, mod, g_mix, conv_w_in, conv_w, layer, j):
    tm, tn = CONV_TM, CONV_TN
    nb = D_MODEL // tn
    w_spec = lambda part: pl.BlockSpec((None, D_MODEL, tn),
                                       lambda i, n: (j, 0, part * nb + n))
    return pl.pallas_call(
        _conv_in_kernel,
        out_shape=jax.ShapeDtypeStruct((N_ROWS, D_MODEL), BF16),
        grid=(N_ROWS // tm, nb),
        in_specs=[
            pl.BlockSpec((tm, D_MODEL), lambda i, n: (i, 0)),
            _mod_spec(layer, tm),
            _gain_spec(layer),
            w_spec(0), w_spec(1), w_spec(2),
            pl.BlockSpec((None, CONV_W, tn), lambda i, n: (j, 0, n)),
        ],
        out_specs=pl.BlockSpec((tm, tn), lambda i, n: (i, n)),
        scratch_shapes=[pltpu.VMEM((tm, D_MODEL), BF16)],
        compiler_params=_params(2),
        name="conv_in",
    )(x, mod, g_mix, conv_w_in, conv_w_in, conv_w_in, conv_w)


PROJ_TM = 1024


def _proj_res_kernel(a_ref, w_ref, x_ref, mod_ref, o_ref):
    y = _dot(a_ref[...], w_ref[...].astype(BF16))
    o_ref[...] = x_ref[...] + mod_ref[2:3, :] * y


def _proj_res(a, w, j, x, mod, layer):
    tm = PROJ_TM
    k = a.shape[1]
    return pl.pallas_call(
        _proj_res_kernel,
        out_shape=jax.ShapeDtypeStruct((N_ROWS, D_MODEL), F32),
        grid=(N_ROWS // tm,),
        in_specs=[
            pl.BlockSpec((tm, k), lambda i: (i, 0)),
            pl.BlockSpec((None, k, D_MODEL), lambda i: (j, 0, 0)),
            pl.BlockSpec((tm, D_MODEL), lambda i: (i, 0)),
            _mod_spec(layer, tm),
        ],
        out_specs=pl.BlockSpec((tm, D_MODEL), lambda i: (i, 0)),
        compiler_params=_params(1),
        name="proj_res",
    )(a, w, x, mod)


GMLP_TM = 1024
GMLP_TN = 1024
GMLP_OUT_TM = 512


def _gelu_tanh(x):
    c = 0.7978845608028654
    return 0.5 * x * (1.0 + jnp.tanh(c * (x + 0.044715 * (x * x * x))))


def _gmlp_in_kernel(x_ref, mod_ref, g_ref, w_ref, b_ref, o_ref, h_ref):
    @pl.when(pl.program_id(1) == 0)
    def _():
        _norm_mod_to(x_ref, g_ref, mod_ref[1:2, :], mod_ref[0:1, :], h_ref)

    z = _dot(h_ref[...], w_ref[...].astype(BF16)) + b_ref[...]
    o_ref[...] = _gelu_tanh(z).astype(o_ref.dtype)


def _gmlp_in(x, mod, g_mix, w_in, b_in, layer, j):
    tm, tn = GMLP_TM, GMLP_TN
    n_out = 2 * GMLP_DIM
    return pl.pallas_call(
        _gmlp_in_kernel,
        out_shape=jax.ShapeDtypeStruct((N_ROWS, n_out), BF16),
        grid=(N_ROWS // tm, n_out // tn),
        in_specs=[
            pl.BlockSpec((tm, D_MODEL), lambda i, n: (i, 0)),
            _mod_spec(layer, tm),
            _gain_spec(layer),
            pl.BlockSpec((None, D_MODEL, tn), lambda i, n: (j, 0, n)),
            pl.BlockSpec((None, 1, tn), lambda i, n: (j, 0, n)),
        ],
        out_specs=pl.BlockSpec((tm, tn), lambda i, n: (i, n)),
        scratch_shapes=[pltpu.VMEM((tm, D_MODEL), BF16)],
        compiler_params=_params(2),
        name="gmlp_in",
    )(x, mod, g_mix, w_in, b_in.reshape(-1, 1, n_out))


def _gmlp_out_kernel(u_ref, v_ref, lng_ref, lnb_ref, ws_ref, bs_ref, w_ref, x_ref, mod_ref,
                     o_ref, a_ref):
    lng = lng_ref[...]
    lnb = lnb_ref[...]

    def chunk(c, carry):
        rows = pl.ds(pl.multiple_of(c * CHUNK, CHUNK), CHUNK)
        v = v_ref[rows, :].astype(F32)
        mu = jnp.mean(v, axis=-1, keepdims=True)
        vc = v - mu
        var = jnp.mean(vc * vc, axis=-1, keepdims=True)
        vn = (vc * lax.rsqrt(var + EPS) * lng + lnb).astype(BF16)
        for g in range(GMLP_GROUPS):
            cols = slice(g * GMLP_GROUP_DIM, (g + 1) * GMLP_GROUP_DIM)
            mixed = _dot(ws_ref[g].astype(BF16), vn[:, cols]) + bs_ref[:, g:g + 1]
            a_ref[rows, cols] = (u_ref[rows, cols].astype(F32) * mixed).astype(a_ref.dtype)
        return carry

    lax.fori_loop(0, u_ref.shape[0] // CHUNK, chunk, 0)
    y = _dot(a_ref[...], w_ref[...].astype(BF16))
    o_ref[...] = x_ref[...] + mod_ref[2:3, :] * y


def _gmlp_out(z, ln_g, ln_b, w_s, b_s, w_out, x, mod, layer, j):
    tm = GMLP_OUT_TM
    return pl.pallas_call(
        _gmlp_out_kernel,
        out_shape=jax.ShapeDtypeStruct((N_ROWS, D_MODEL), F32),
        grid=(N_ROWS // tm,),
        in_specs=[
            pl.BlockSpec((tm, GMLP_DIM), lambda i: (i, 0)),
            pl.BlockSpec((tm, GMLP_DIM), lambda i: (i, 1)),
            pl.BlockSpec((None, 1, GMLP_DIM), lambda i: (j, 0, 0)),
            pl.BlockSpec((None, 1, GMLP_DIM), lambda i: (j, 0, 0)),
            pl.BlockSpec((None, GMLP_GROUPS, CHUNK, CHUNK), lambda i: (j, 0, 0, 0)),
            pl.BlockSpec((None, CHUNK, GMLP_GROUPS), lambda i: (j, 0, 0)),
            pl.BlockSpec((None, GMLP_DIM, D_MODEL), lambda i: (j, 0, 0)),
            pl.BlockSpec((tm, D_MODEL), lambda i: (i, 0)),
            _mod_spec(layer, tm),
        ],
        out_specs=pl.BlockSpec((tm, D_MODEL), lambda i: (i, 0)),
        scratch_shapes=[pltpu.VMEM((tm, GMLP_DIM), BF16)],
        compiler_params=_params(1),
        name="gmlp_out",
    )(z, z, ln_g.reshape(-1, 1, GMLP_DIM), ln_b.reshape(-1, 1, GMLP_DIM), w_s,
      jnp.swapaxes(b_s, 1, 2), w_out, x, mod)


QKV_TM = 512
ATTN_TQ = 256


def _rope_tables(tm):
    pos = np.arange(DEC_SEQ)
    row = (pos // GRID_W).astype(np.float32)
    col = (pos % GRID_W).astype(np.float32)
    n_freq = HEAD_DIM // 4
    inv_freq = np.power(np.float32(ROPE_THETA), -np.arange(n_freq, dtype=np.float32) / n_freq)
    ang = np.concatenate([row[:, None] * inv_freq, col[:, None] * inv_freq], axis=-1)
    cos = np.repeat(np.cos(ang), 2, axis=-1)
    sign = np.tile(np.array([-1.0, 1.0], np.float32), HEAD_DIM // 2)
    sin = np.repeat(np.sin(ang), 2, axis=-1) * sign
    cos = np.concatenate([np.ones((tm, HEAD_DIM), np.float32), cos], axis=0)
    sin = np.concatenate([np.zeros((tm, HEAD_DIM), np.float32), sin], axis=0)
    return jnp.asarray(cos, F32), jnp.asarray(sin, F32)


def _swap_pairs(x):
    even = (lax.broadcasted_iota(jnp.int32, x.shape, 1) & 1) == 0
    return jnp.where(even, pltpu.roll(x, HEAD_DIM - 1, 1), pltpu.roll(x, 1, 1))


def _qkv_kernel(x_ref, mod_ref, g_ref, w_ref, qg_ref, kg_ref, cos_ref, sin_ref, o_ref,
                h_ref, wsw_ref):
    n_rot_heads = N_HEADS + N_KV_HEADS

    @pl.when(pl.program_id(0) == 0)
    def _():
        for hd in range(n_rot_heads):
            cols = slice(hd * HEAD_DIM, (hd + 1) * HEAD_DIM)
            wsw_ref[:, cols] = _swap_pairs(w_ref[:, cols]).astype(BF16)

    _norm_mod_to(x_ref, g_ref, mod_ref[1:2, :], mod_ref[0:1, :], h_ref)
    y = _dot(h_ref[...], w_ref[...].astype(BF16))
    y_sw = _dot(h_ref[...], wsw_ref[...])
    cos = cos_ref[...]
    sin = sin_ref[...]
    for hd in range(n_rot_heads):
        cols = slice(hd * HEAD_DIM, (hd + 1) * HEAD_DIM)
        yh = y[:, cols]
        gain = qg_ref[...] if hd < N_HEADS else kg_ref[...]
        inv = lax.rsqrt(jnp.mean(yh * yh, axis=-1, keepdims=True) + EPS)
        o_ref[:, cols] = (yh * (gain * cos) + y_sw[:, cols] * (_swap_pairs(gain) * sin)) * inv
    v_cols = slice(n_rot_heads * HEAD_DIM, QKV_OUT)
    o_ref[:, v_cols] = y[:, v_cols]


def _qkv(x, mod, g_mix, w_qkv, q_g, k_g, layer, j):
    tm = QKV_TM
    cos, sin = _rope_tables(tm)
    n_prompt = N_PROMPT_ROWS // tm
    per_seq = DEC_SEQ // tm
    tbl = lambda i: (jnp.where(i < n_prompt, 0, 1 + (i - n_prompt) % per_seq), 0)
    return pl.pallas_call(
        _qkv_kernel,
        out_shape=jax.ShapeDtypeStruct((N_ROWS, QKV_OUT), F32),
        grid=(N_ROWS // tm,),
        in_specs=[
            pl.BlockSpec((tm, D_MODEL), lambda i: (i, 0)),
            _mod_spec(layer, tm),
            _gain_spec(layer),
            pl.BlockSpec((None, D_MODEL, QKV_OUT), lambda i: (j, 0, 0)),
            pl.BlockSpec((None, 1, HEAD_DIM), lambda i: (j, 0, 0)),
            pl.BlockSpec((None, 1, HEAD_DIM), lambda i: (j, 0, 0)),
            pl.BlockSpec((tm, HEAD_DIM), tbl),
            pl.BlockSpec((tm, HEAD_DIM), tbl),
        ],
        out_specs=pl.BlockSpec((tm, QKV_OUT), lambda i: (i, 0)),
        scratch_shapes=[pltpu.VMEM((tm, D_MODEL), BF16),
                        pltpu.VMEM((D_MODEL, (N_HEADS + N_KV_HEADS) * HEAD_DIM), BF16)],
        compiler_params=_params(1),
        name="qkv",
    )(x, mod, g_mix, w_qkv, q_g.reshape(-1, 1, HEAD_DIM), k_g.reshape(-1, 1, HEAD_DIM), cos, sin)


def _attn_kernel(*refs, has_ctx):
    if has_ctx:
        q_ref, k_ref, v_ref, ck_ref, cv_ref, o_ref = refs
    else:
        q_ref, k_ref, v_ref, o_ref = refs
    nt = (((1,), (1,)), ((), ()))
    k = k_ref[...].astype(BF16)
    v = v_ref[...].astype(BF16)
    if has_ctx:
        ck = ck_ref[...].astype(BF16)
        cv = cv_ref[...].astype(BF16)
    for g in range(Q_PER_KV):
        cols = slice(g * HEAD_DIM, (g + 1) * HEAD_DIM)
        q = (q_ref[:, cols] * (HEAD_DIM ** -0.5)).astype(BF16)
        s = lax.dot_general(q, k, nt, preferred_element_type=F32)
        m = jnp.max(s, axis=-1, keepdims=True)
        if has_ctx:
            sc = lax.dot_general(q, ck, nt, preferred_element_type=F32)
            m = jnp.maximum(m, jnp.max(sc, axis=-1, keepdims=True))
        p = jnp.exp(s - m)
        den = jnp.sum(p, axis=-1, keepdims=True)
        o = _dot(p.astype(BF16), v)
        if has_ctx:
            pc = jnp.exp(sc - m)
            den = den + jnp.sum(pc, axis=-1, keepdims=True)
            o = o + _dot(pc.astype(BF16), cv)
        o_ref[:, cols] = (o / den).astype(o_ref.dtype)


def _attention(qkv, ctx_k, ctx_v):
    group_w = Q_PER_KV * HEAD_DIM
    k_col = N_HEADS
    v_col = N_HEADS + N_KV_HEADS

    def call(n_batch, seq, tq, row0, ctx):
        q_blocks = seq // tq
        o_row = lambda b, qi: b * q_blocks + qi
        kv_row = lambda b: row0 // seq + b
        in_specs = [
            pl.BlockSpec((tq, group_w), lambda b, kh, qi: (row0 // tq + o_row(b, qi), kh)),
            pl.BlockSpec((seq, HEAD_DIM), lambda b, kh, qi: (kv_row(b), k_col + kh)),
            pl.BlockSpec((seq, HEAD_DIM), lambda b, kh, qi: (kv_row(b), v_col + kh)),
        ]
        args = [qkv, qkv, qkv]
        if ctx is not None:
            ctx_spec = pl.BlockSpec((None, PAST_LEN, HEAD_DIM), lambda b, kh, qi: (b, 0, kh))
            in_specs += [ctx_spec, ctx_spec]
            args += list(ctx)
        return pl.pallas_call(
            functools.partial(_attn_kernel, has_ctx=ctx is not None),
            out_shape=jax.ShapeDtypeStruct((n_batch * seq, N_HEADS * HEAD_DIM), BF16),
            grid=(n_batch, N_KV_HEADS, q_blocks),
            in_specs=in_specs,
            out_specs=pl.BlockSpec((tq, group_w), lambda b, kh, qi: (o_row(b, qi), kh)),
            compiler_params=_params(3),
            name="attention",
        )(*args)

    o_prompt = call(BATCH, SEQ, SEQ, 0, None)
    o_sample = call(DEC_BATCH, DEC_SEQ, ATTN_TQ, N_PROMPT_ROWS, (ctx_k, ctx_v))
    return jnp.concatenate([o_prompt, o_sample], axis=0)


FFN_TM = 2048
FFN_TF = 256
TOP_K = 2
ROUTER_TM = 1024
ROUTE_LANES = 128
MOE_TF = 512
MOE_TILE = 1024
MOE_SUB = 256
MOE_MAX_TILES = TOP_K * N_ROWS // MOE_TILE + N_EXPERTS
MOE_ROWS = MOE_MAX_TILES * MOE_TILE
SCATTER_TM = 512
COMBINE_TM = 256
WIN_CHUNK = 128
WIN_CHUNKS = COMBINE_TM // WIN_CHUNK + 1


def _swiglu_hidden(h, wg, wu):
    gate = _dot(h, wg)
    return (gate * jax.nn.sigmoid(gate) * _dot(h, wu)).astype(BF16)


def _ffn_kernel(x_ref, mod_ref, g_ref, wg_ref, wu_ref, wd_ref, o_ref, h_ref):
    f = pl.program_id(1)

    @pl.when(f == 0)
    def _():
        _norm_mod_to(x_ref, g_ref, mod_ref[4:5, :], mod_ref[3:4, :], h_ref)
        o_ref[...] = jnp.zeros_like(o_ref)

    a = _swiglu_hidden(h_ref[...], wg_ref[...].astype(BF16), wu_ref[...].astype(BF16))
    o_ref[...] += _dot(a, wd_ref[...].astype(BF16))

    @pl.when(f == pl.num_programs(1) - 1)
    def _():
        o_ref[...] = x_ref[...] + mod_ref[5:6, :] * o_ref[...]


def _ffn(x, mod, g_ffn, w_gate, w_up, w_down, layer, li):
    tm, tf = FFN_TM, FFN_TF
    return pl.pallas_call(
        _ffn_kernel,
        out_shape=jax.ShapeDtypeStruct((N_ROWS, D_MODEL), F32),
        grid=(N_ROWS // tm, D_FF // tf),
        in_specs=[
            pl.BlockSpec((tm, D_MODEL), lambda i, f: (i, 0), pipeline_mode=pl.Buffered(1)),
            _mod_spec(layer, tm),
            _gain_spec(layer),
            pl.BlockSpec((None, D_MODEL, tf), lambda i, f: (li, 0, f)),
            pl.BlockSpec((None, D_MODEL, tf), lambda i, f: (li, 0, f)),
            pl.BlockSpec((None, tf, D_MODEL), lambda i, f: (li, f, 0)),
        ],
        out_specs=pl.BlockSpec((tm, D_MODEL), lambda i, f: (i, 0)),
        scratch_shapes=[pltpu.VMEM((tm, D_MODEL), BF16)],
        compiler_params=_params(2),
        name="ffn",
    )(x, mod, g_ffn, w_gate, w_up, w_down)


def _split_bf16(x):
    hi = x.astype(BF16)
    return hi, (x - hi.astype(F32)).astype(BF16)


def _router_kernel(x_ref, mod_ref, g_ref, w_ref, route_ref, route_t_ref, cnt_ref, base_ref,
                   h_ref, seen_ref):
    @pl.when(pl.program_id(0) == 0)
    def _():
        seen_ref[...] = jnp.zeros_like(seen_ref)

    _norm_mod_to(x_ref, g_ref, mod_ref[4:5, :], mod_ref[3:4, :], h_ref)
    h_hi, h_lo = _split_bf16(h_ref[...])
    w_hi, w_lo = _split_bf16(w_ref[...])
    logits = _dot(h_hi, w_hi) + (_dot(h_hi, w_lo) + _dot(h_lo, w_hi))
    lane = lax.broadcasted_iota(jnp.int32, logits.shape, 1)
    logits = jnp.where(lane < N_EXPERTS, logits, -jnp.inf)
    m1 = jnp.max(logits, axis=-1, keepdims=True)
    i1 = jnp.min(jnp.where(logits == m1, lane, N_EXPERTS), axis=-1, keepdims=True)
    rest = jnp.where(lane == i1, -jnp.inf, logits)
    m2 = jnp.max(rest, axis=-1, keepdims=True)
    i2 = jnp.min(jnp.where(rest == m2, lane, N_EXPERTS), axis=-1, keepdims=True)
    e2 = jnp.exp(m2 - m1)
    w1 = 1.0 / (1.0 + e2)
    chosen = jnp.where((lane == i1) | (lane == i2), 1.0, 0.0)
    tm = chosen.shape[0]
    earlier = (lax.broadcasted_iota(jnp.int32, (tm, tm), 1)
               < lax.broadcasted_iota(jnp.int32, (tm, tm), 0))
    prior = _dot(jnp.where(earlier, 1.0, 0.0).astype(BF16), chosen.astype(BF16)) + seen_ref[...]
    rank1 = jnp.sum(jnp.where(lane == i1, prior, 0.0), axis=-1, keepdims=True)
    rank2 = jnp.sum(jnp.where(lane == i2, prior, 0.0), axis=-1, keepdims=True)
    seen_ref[...] += jnp.sum(chosen, axis=0, keepdims=True)
    route = jnp.zeros_like(logits)
    for k, field in enumerate((w1, e2 * w1, i1.astype(F32), i2.astype(F32), rank1, rank2)):
        route = jnp.where(lane == k, field, route)
    route_ref[...] = route
    route_t_ref[...] = jnp.transpose(route)[:SUBLANES, :]
    cnt_ref[...] = jnp.broadcast_to(seen_ref[...], cnt_ref.shape)
    base_ref[...] = jnp.zeros_like(base_ref)
    for k in range(tm // COMBINE_TM):
        base_ref[k:k + 1, :] = prior[k * COMBINE_TM:k * COMBINE_TM + 1, :]


def _router(x, mod, g_ffn, w_router, layer, li):
    tm = ROUTER_TM
    w_pad = jnp.pad(w_router, ((0, 0), (0, 0), (0, ROUTE_LANES - N_EXPERTS)))
    return pl.pallas_call(
        _router_kernel,
        out_shape=(jax.ShapeDtypeStruct((N_ROWS, ROUTE_LANES), F32),
                   jax.ShapeDtypeStruct((SUBLANES, N_ROWS), F32),
                   jax.ShapeDtypeStruct((SUBLANES, ROUTE_LANES), F32),
                   jax.ShapeDtypeStruct((N_ROWS // tm, SUBLANES, ROUTE_LANES), F32)),
        grid=(N_ROWS // tm,),
        in_specs=[
            pl.BlockSpec((tm, D_MODEL), lambda i: (i, 0)),
            _mod_spec(layer, tm),
            _gain_spec(layer),
            pl.BlockSpec((None, D_MODEL, ROUTE_LANES), lambda i: (li, 0, 0)),
        ],
        out_specs=(pl.BlockSpec((tm, ROUTE_LANES), lambda i: (i, 0)),
                   pl.BlockSpec((SUBLANES, tm), lambda i: (0, i)),
                   pl.BlockSpec((SUBLANES, ROUTE_LANES), lambda i: (0, 0)),
                   pl.BlockSpec((None, SUBLANES, ROUTE_LANES), lambda i: (i, 0, 0))),
        scratch_shapes=[pltpu.VMEM((tm, D_MODEL), F32), pltpu.VMEM((1, ROUTE_LANES), F32)],
        compiler_params=_params(1),
        name="router",
    )(x, mod, g_ffn, w_pad)


def _dispatch_plan(route_t, cnt, tile_base):
    counts = cnt[0, :N_EXPERTS].astype(jnp.int32)
    e1, e2, r1, r2 = (route_t[k].astype(jnp.int32) for k in (2, 3, 4, 5))
    n_tiles = (counts + MOE_TILE - 1) // MOE_TILE
    tile_end = jnp.cumsum(n_tiles)
    tile_start = tile_end - n_tiles
    row_start = tile_start * MOE_TILE
    pos = jnp.concatenate([row_start[e1] + r1, row_start[e2] + r2])
    n_active = tile_end[-1]
    t = jnp.arange(MOE_MAX_TILES, dtype=jnp.int32)
    tile_block = jnp.minimum(t, n_active - 1)
    tile_expert = jnp.sum(tile_block[:, None] >= tile_end[None, :], axis=1).astype(jnp.int32)
    valid = counts[tile_expert] - (tile_block - tile_start[tile_expert]) * MOE_TILE
    n_sub = (jnp.clip(valid, 0, MOE_TILE) + MOE_SUB - 1) // MOE_SUB
    tile_nsub = jnp.where(t < n_active, n_sub, 0).astype(jnp.int32)
    pad = jnp.concatenate([row_start + counts, (-counts) % MOE_SUB]).astype(jnp.int32)
    base = tile_base[:, :ROUTER_TM // COMBINE_TM, :N_EXPERTS].astype(jnp.int32)
    base = base.reshape(N_ROWS // COMBINE_TM, N_EXPERTS)
    lo = row_start[None, :] + base
    hi = row_start[None, :] + jnp.concatenate([base[1:], counts[None, :]], axis=0)
    win_chunk0 = lo // WIN_CHUNK
    win_chunks = jnp.where(hi > lo, (hi + WIN_CHUNK - 1) // WIN_CHUNK - win_chunk0, 0)
    return dict(pos=pos, pad=pad, tile_expert=tile_expert, tile_nsub=tile_nsub,
                tile_block=tile_block, row_start=row_start.astype(jnp.int32),
                win_chunk0=win_chunk0.reshape(-1).astype(jnp.int32),
                win_chunks=win_chunks.reshape(-1).astype(jnp.int32))


def _for_rows(n_rows, body):
    def group(j, carry):
        r0 = pl.multiple_of(j * SUBLANES, SUBLANES)
        for u in range(SUBLANES):
            body(r0 + u)
        return carry
    lax.fori_loop(0, n_rows // SUBLANES, group, 0)


def _scatter_kernel(pos_ref, pad_ref, tn_ref, x_ref, mod_ref, g_ref, xs_ref, h_ref, zero_ref,
                    sem, fill_sem):
    i = pl.program_id(0)
    ts = x_ref.shape[0]
    slot = lax.rem(i, 2)
    _norm_mod_to(x_ref, g_ref, mod_ref[4:5, :], mod_ref[3:4, :], h_ref.at[slot])

    def for_row_copies(step, step_slot, act):
        first = step * ts

        def row(r):
            for k in range(TOP_K):
                dst = pos_ref[k * N_ROWS + first + r]
                act(pltpu.make_async_copy(h_ref.at[step_slot, pl.ds(r, 1)],
                                          xs_ref.at[pl.ds(dst, 1)], sem.at[step_slot]), k)
        _for_rows(ts, row)

    for_row_copies(i, slot, lambda cp, k: cp.start(priority=k))

    @pl.when(i == 0)
    def _():
        zero_ref[...] = jnp.zeros_like(zero_ref)

        def fills(act):
            for e in range(N_EXPERTS):
                def row_body(r, carry, e=e):
                    act(pltpu.make_async_copy(zero_ref.at[pl.ds(0, 1)],
                                              xs_ref.at[pl.ds(pad_ref[e] + r, 1)], fill_sem))
                    return carry
                lax.fori_loop(0, pad_ref[N_EXPERTS + e], row_body, 0)

            def tile_body(t, carry):
                def sub_body(s, carry):
                    row = pl.multiple_of(t * MOE_TILE + s * MOE_SUB, MOE_SUB)
                    act(pltpu.make_async_copy(zero_ref, xs_ref.at[pl.ds(row, MOE_SUB)], fill_sem))
                    return carry
                return lax.fori_loop(tn_ref[t], MOE_TILE // MOE_SUB, sub_body, carry)
            lax.fori_loop(0, MOE_MAX_TILES, tile_body, 0)

        fills(lambda cp: cp.start())
        fills(lambda cp: cp.wait())

    @pl.when(i > 0)
    def _():
        for_row_copies(i - 1, 1 - slot, lambda cp, k: cp.wait())

    @pl.when(i == pl.num_programs(0) - 1)
    def _():
        for_row_copies(i, slot, lambda cp, k: cp.wait())


def _moe_scatter(x, mod, g_ffn, plan, layer):
    ts = SCATTER_TM
    grid_spec = pltpu.PrefetchScalarGridSpec(
        num_scalar_prefetch=3,
        grid=(N_ROWS // ts,),
        in_specs=[
            pl.BlockSpec((ts, D_MODEL), lambda i, *_: (i, 0)),
            _mod_spec(layer, ts),
            _gain_spec(layer),
        ],
        out_specs=pl.BlockSpec(memory_space=pl.ANY),
        scratch_shapes=[pltpu.VMEM((2, ts, D_MODEL), F32),
                        pltpu.VMEM((MOE_SUB, D_MODEL), F32),
                        pltpu.SemaphoreType.DMA((2,)), pltpu.SemaphoreType.DMA(())],
    )
    return pl.pallas_call(
        _scatter_kernel,
        out_shape=jax.ShapeDtypeStruct((MOE_ROWS, D_MODEL), F32),
        grid_spec=grid_spec,
        compiler_params=_params(1),
        name="moe_scatter",
    )(plan["pos"], plan["pad"], plan["tile_nsub"], x, mod, g_ffn)


def _moe_kernel(te_ref, tn_ref, tb_ref, x_ref, wg_ref, wu_ref, wd_ref, o_ref,
                xb_ref, acc_ref):
    n_sub = tn_ref[pl.program_id(0)]
    first_f = pl.program_id(1) == 0

    @pl.when(first_f)
    def _():
        acc_ref[...] = jnp.zeros_like(acc_ref)

    for n in range(1, MOE_TILE // MOE_SUB + 1):
        @pl.when(n_sub == n)
        def _(n=n):
            rows = slice(0, n * MOE_SUB)

            @pl.when(first_f)
            def _():
                xb_ref[rows, :] = x_ref[rows, :].astype(BF16)

            a = _swiglu_hidden(xb_ref[rows, :], wg_ref[...].astype(BF16),
                               wu_ref[...].astype(BF16))
            acc_ref[rows, :] += _dot(a, wd_ref[...].astype(BF16))

    @pl.when(pl.program_id(1) == pl.num_programs(1) - 1)
    def _():
        o_ref[...] = acc_ref[...].astype(o_ref.dtype)


def _moe_experts(xs, plan, w_gate, w_up, w_down, li):
    tile, tf = MOE_TILE, MOE_TF
    nf = D_FF_EXPERT // tf
    frozen_f = lambda i, f, tn: jnp.where(tn[i] > 0, f, nf - 1)
    grid_spec = pltpu.PrefetchScalarGridSpec(
        num_scalar_prefetch=3,
        grid=(MOE_MAX_TILES, nf),
        in_specs=[
            pl.BlockSpec((tile, D_MODEL), lambda i, f, te, tn, tb: (tb[i], 0)),
            pl.BlockSpec((None, None, D_MODEL, tf),
                         lambda i, f, te, tn, tb: (li, te[i], 0, frozen_f(i, f, tn))),
            pl.BlockSpec((None, None, D_MODEL, tf),
                         lambda i, f, te, tn, tb: (li, te[i], 0, frozen_f(i, f, tn))),
            pl.BlockSpec((None, None, tf, D_MODEL),
                         lambda i, f, te, tn, tb: (li, te[i], frozen_f(i, f, tn), 0)),
        ],
        out_specs=pl.BlockSpec((tile, D_MODEL), lambda i, f, te, tn, tb: (i, 0)),
        scratch_shapes=[pltpu.VMEM((tile, D_MODEL), BF16), pltpu.VMEM((tile, D_MODEL), F32)],
    )
    return pl.pallas_call(
        _moe_kernel,
        out_shape=jax.ShapeDtypeStruct((MOE_ROWS, D_MODEL), BF16),
        grid_spec=grid_spec,
        compiler_params=_params(2),
        name="moe_experts",
    )(plan["tile_expert"], plan["tile_nsub"], plan["tile_block"], xs, w_gate, w_up, w_down)


def _combine_kernel(start_ref, c0_ref, n_ref, x_ref, mod_ref, route_ref, ys_ref, o_ref,
                    win_ref, sem):
    i = pl.program_id(0)
    slot = lax.rem(i, 2)

    def chunk_slots(tile):
        for e in range(N_EXPERTS):
            c0 = c0_ref[tile * N_EXPERTS + e]
            n = n_ref[tile * N_EXPERTS + e]
            for j in range(WIN_CHUNKS):
                yield e * WIN_CHUNKS + j, c0 + j, j < n

    def window_copies(tile, tile_slot, act):
        for s, chunk, needed in chunk_slots(tile):
            @pl.when(needed)
            def _(s=s, chunk=chunk):
                rows = pl.ds(pl.multiple_of(chunk * WIN_CHUNK, WIN_CHUNK), WIN_CHUNK)
                act(pltpu.make_async_copy(ys_ref.at[rows],
                                          win_ref.at[tile_slot, pl.ds(s * WIN_CHUNK, WIN_CHUNK)],
                                          sem.at[tile_slot]))

    @pl.when(i == 0)
    def _():
        win_ref[...] = jnp.zeros_like(win_ref)
        window_copies(i, slot, lambda cp: cp.start())

    @pl.when(i + 1 < pl.num_programs(0))
    def _():
        window_copies(i + 1, 1 - slot, lambda cp: cp.start())

    window_copies(i, slot, lambda cp: cp.wait())

    w1, w2, e1, e2, pos1, pos2 = (route_ref[:, k:k + 1] for k in range(6))
    for e in range(N_EXPERTS):
        first_row = start_ref[e].astype(F32)
        pos1 = pos1 + jnp.where(e1 == e, first_row, 0.0)
        pos2 = pos2 + jnp.where(e2 == e, first_row, 0.0)
    lane = lax.broadcasted_iota(jnp.int32, (x_ref.shape[0], WIN_CHUNK), 1).astype(F32)
    blocks = []
    for s, chunk, needed in chunk_slots(i):
        row0 = jnp.where(needed, chunk * WIN_CHUNK, -MOE_ROWS).astype(F32)
        rows = lane + row0
        weight = jnp.where(pos1 == rows, w1, 0.0) + jnp.where(pos2 == rows, w2, 0.0)
        blocks.append(weight.astype(BF16))
    y = _dot(jnp.concatenate(blocks, axis=1), win_ref[slot])
    o_ref[...] = x_ref[...] + mod_ref[5:6, :] * y


def _moe_combine(x, mod, route, ys, plan, layer):
    tc = COMBINE_TM
    grid_spec = pltpu.PrefetchScalarGridSpec(
        num_scalar_prefetch=3,
        grid=(N_ROWS // tc,),
        in_specs=[
            pl.BlockSpec((tc, D_MODEL), lambda i, *_: (i, 0)),
            _mod_spec(layer, tc),
            pl.BlockSpec((tc, ROUTE_LANES), lambda i, *_: (i, 0)),
            pl.BlockSpec(memory_space=pl.ANY),
        ],
        out_specs=pl.BlockSpec((tc, D_MODEL), lambda i, *_: (i, 0)),
        scratch_shapes=[pltpu.VMEM((2, N_EXPERTS * WIN_CHUNKS * WIN_CHUNK, D_MODEL), BF16),
                        pltpu.SemaphoreType.DMA((2,))],
    )
    return pl.pallas_call(
        _combine_kernel,
        out_shape=jax.ShapeDtypeStruct((N_ROWS, D_MODEL), F32),
        grid_spec=grid_spec,
        compiler_params=_params(1),
        name="moe_combine",
    )(plan["row_start"], plan["win_chunk0"], plan["win_chunks"], x, mod, route, ys)


def kernel(x_prompt, x_sample, cache_k, cache_v, c, c_ctx, w_mod, b_mod, g_mix, g_ffn,
           conv_w_in, conv_w, conv_w_out,
           gmlp_w_in, gmlp_b_in, gmlp_ln_g, gmlp_ln_b, gmlp_w_s, gmlp_b_s, gmlp_w_out,
           attn_w_qkv, attn_q_g, attn_k_g, attn_w_o,
           ffn_w_gate, ffn_w_up, ffn_w_down,
           moe_w_router, moe_w_gate, moe_w_up, moe_w_down):
    x = jnp.concatenate([x_prompt.reshape(N_PROMPT_ROWS, D_MODEL),
                         x_sample.reshape(N_SAMPLE_ROWS, D_MODEL)], axis=0)
    cond = jnp.zeros((N_COND_ROWS, D_MODEL), F32).at[0].set(c_ctx).at[1:1 + DEC_BATCH].set(c)
    mod = _modulation(cond, w_mod, b_mod)
    g_mix3 = g_mix.reshape(DEPTH, 1, D_MODEL)
    g_ffn3 = g_ffn.reshape(DEPTH, 1, D_MODEL)
    n_attn = cache_k.shape[1]
    ctx_k = cache_k.reshape(DEC_BATCH, n_attn, PAST_LEN, N_KV_HEADS * HEAD_DIM)
    ctx_v = cache_v.reshape(DEC_BATCH, n_attn, PAST_LEN, N_KV_HEADS * HEAD_DIM)

    new_k, new_v = [], []
    for layer in range(DEPTH):
        kind, j = layer % N_MIXERS, layer // N_MIXERS
        if kind == 0:
            a = _conv_in(x, mod, g_mix3, conv_w_in, conv_w, layer, j)
            x = _proj_res(a, conv_w_out, j, x, mod, layer)
        elif kind == 1:
            z = _gmlp_in(x, mod, g_mix3, gmlp_w_in, gmlp_b_in, layer, j)
            x = _gmlp_out(z, gmlp_ln_g, gmlp_ln_b, gmlp_w_s, gmlp_b_s, gmlp_w_out, x, mod,
                          layer, j)
        else:
            qkv = _qkv(x, mod, g_mix3, attn_w_qkv, attn_q_g, attn_k_g, layer, j)
            k_cols = slice(N_HEADS * HEAD_DIM, (N_HEADS + N_KV_HEADS) * HEAD_DIM)
            v_cols = slice((N_HEADS + N_KV_HEADS) * HEAD_DIM, QKV_OUT)
            new_k.append(qkv[:N_PROMPT_ROWS, k_cols].reshape(BATCH, SEQ, N_KV_HEADS, HEAD_DIM))
            new_v.append(qkv[:N_PROMPT_ROWS, v_cols].reshape(BATCH, SEQ, N_KV_HEADS, HEAD_DIM))
            a = _attention(qkv, ctx_k[:, j], ctx_v[:, j])
            x = _proj_res(a, attn_w_o, j, x, mod, layer)
        li = layer // 2
        if layer % 2 == 0:
            x = _ffn(x, mod, g_ffn3, ffn_w_gate, ffn_w_up, ffn_w_down, layer, li)
        else:
            route, route_t, cnt, tile_base = _router(x, mod, g_ffn3, moe_w_router, layer, li)
            plan = _dispatch_plan(route_t, cnt, tile_base)
            xs = _moe_scatter(x, mod, g_ffn3, plan, layer)
            ys = _moe_experts(xs, plan, moe_w_gate, moe_w_up, moe_w_down, li)
            x = _moe_combine(x, mod, route, ys, plan, layer)

    y_prompt = x[:N_PROMPT_ROWS].reshape(BATCH, SEQ, D_MODEL)
    y_sample = x[N_PROMPT_ROWS:].reshape(DEC_BATCH, DEC_SEQ, D_MODEL)
    return (y_prompt, y_sample, jnp.stack(new_k, axis=1), jnp.stack(new_v, axis=1))
```

```python
import functools

import jax
import jax.numpy as jnp
import numpy as np
from jax import lax
from jax.experimental import pallas as pl
from jax.experimental.pallas import tpu as pltpu

F32 = jnp.float32
BF16 = jnp.bfloat16

D_MODEL = 1024
BATCH = 16
SEQ = 256
DEPTH = 4
DEC_BATCH = 2
DEC_SEQ = 2048
PAST_LEN = 256
GRID_W = 64
N_MIXERS = 3
N_HEADS = 8
N_KV_HEADS = 2
HEAD_DIM = 128
Q_PER_KV = N_HEADS // N_KV_HEADS
ROPE_THETA = 10000.0
CONV_W = 3
CHUNK = 128
GMLP_DIM = 2 * D_MODEL
GMLP_GROUPS = 8
GMLP_GROUP_DIM = GMLP_DIM // GMLP_GROUPS
D_FF = 2816
N_EXPERTS = 8
D_FF_EXPERT = 3584
EPS = 1e-6
QKV_OUT = (N_HEADS + 2 * N_KV_HEADS) * HEAD_DIM

N_PROMPT_ROWS = BATCH * SEQ
N_SAMPLE_ROWS = DEC_BATCH * DEC_SEQ
N_ROWS = N_PROMPT_ROWS + N_SAMPLE_ROWS
SUBLANES = 8
N_COND_ROWS = SUBLANES

V7X_VMEM_BYTES = 64 * 1024 * 1024
VMEM_LIMIT_BYTES = V7X_VMEM_BYTES - 8 * 1024 * 1024
NORM_ROWS = 256


def _params(n_axes):
    return pltpu.CompilerParams(
        dimension_semantics=("arbitrary",) * n_axes,
        vmem_limit_bytes=VMEM_LIMIT_BYTES)


def _dot(a, b):
    return jnp.dot(a, b, preferred_element_type=F32)


def _cond_row(i, tm):
    n_prompt = N_PROMPT_ROWS // tm
    per_seq = DEC_SEQ // tm
    return jnp.where(i < n_prompt, 0, 1 + (i - n_prompt) // per_seq)


def _mod_spec(layer, tm):
    return pl.BlockSpec((None, None, 6, D_MODEL),
                        lambda i, *_: (layer, _cond_row(i, tm), 0, 0))


def _gain_spec(layer):
    return pl.BlockSpec((None, 1, D_MODEL), lambda *_: (layer, 0, 0))


def _row_parts(x):
    return tuple(x) if isinstance(x, (tuple, list)) else (x,)


def _row_specs(x, tm):
    parts = _row_parts(x)
    block = (tm, parts[0].shape[1])
    if len(parts) == 1:
        return [pl.BlockSpec(block, lambda i, *_: (i, 0))]
    n_prompt = N_PROMPT_ROWS // tm
    return [pl.BlockSpec(block, lambda i, *_: (jnp.minimum(i, n_prompt - 1), 0)),
            pl.BlockSpec(block, lambda i, *_: (jnp.maximum(i - n_prompt, 0), 0))]


def _row_loader(refs, tm):
    if len(refs) == 1:
        return lambda rows: refs[0][rows, :]
    is_prompt = pl.program_id(0) < N_PROMPT_ROWS // tm
    return lambda rows: jnp.where(is_prompt, refs[0][rows, :], refs[1][rows, :])


def _norm_mod_rows(load, n_rows, g_ref, scale, shift, h_ref):
    gs = g_ref[...] * (1.0 + scale)

    def body(r, carry):
        rows = pl.ds(pl.multiple_of(r * NORM_ROWS, NORM_ROWS), NORM_ROWS)
        x = load(rows)
        ms = jnp.mean(x * x, axis=-1, keepdims=True)
        h_ref[rows, :] = (x * lax.rsqrt(ms + EPS) * gs + shift).astype(h_ref.dtype)
        return carry

    lax.fori_loop(0, n_rows // NORM_ROWS, body, 0)


def _norm_mod_to(x_ref, g_ref, scale, shift, h_ref):
    _norm_mod_rows(lambda rows: x_ref[rows, :], x_ref.shape[0], g_ref, scale, shift, h_ref)


def _mod_kernel(c_ref, w_ref, b_ref, o_ref):
    cnd = c_ref[...]
    s = (cnd * jax.nn.sigmoid(cnd)).astype(BF16)
    o_ref[...] = _dot(s, w_ref[...].astype(BF16)) + b_ref[...]


def _modulation(cond, w_mod, b_mod):
    tn = 1536
    n_out = 6 * D_MODEL
    out = pl.pallas_call(
        _mod_kernel,
        out_shape=jax.ShapeDtypeStruct((DEPTH, N_COND_ROWS, n_out), F32),
        grid=(DEPTH, n_out // tn),
        in_specs=[
            pl.BlockSpec((N_COND_ROWS, D_MODEL), lambda l, j: (0, 0)),
            pl.BlockSpec((None, D_MODEL, tn), lambda l, j: (l, 0, j)),
            pl.BlockSpec((None, 1, tn), lambda l, j: (l, 0, j)),
        ],
        out_specs=pl.BlockSpec((None, N_COND_ROWS, tn), lambda l, j: (l, 0, j)),
        compiler_params=_params(2),
        name="modulation",
    )(cond, w_mod, b_mod.reshape(DEPTH, 1, n_out))
    return out.reshape(DEPTH, N_COND_ROWS, 6, D_MODEL)


CONV_TM = 2048
CONV_TN = 256


def _conv_in_kernel(*refs, n_x):
    x_refs = refs[:n_x]
    mod_ref, g_ref, wb_ref, wc_ref, wx_ref, cw_ref, o_ref, h_ref = refs[n_x:]
    i = pl.program_id(0)

    @pl.when(pl.program_id(1) == 0)
    def _():
        _norm_mod_rows(_row_loader(x_refs, CONV_TM), CONV_TM, g_ref, mod_ref[1:2, :],
                       mod_ref[0:1, :], h_ref)

    h = h_ref[...]
    b = _dot(h, wb_ref[...].astype(BF16))
    z = _dot(h, wc_ref[...].astype(BF16)) * _dot(h, wx_ref[...].astype(BF16))
    seq_len = jnp.where(i < N_PROMPT_ROWS // CONV_TM, SEQ, DEC_SEQ)
    pos = lax.broadcasted_iota(jnp.int32, z.shape, 0) & (seq_len - 1)
    z_prev = jnp.where(pos == 0, 0.0, pltpu.roll(z, 1, 0))
    z_next = jnp.where(pos == seq_len - 1, 0.0, pltpu.roll(z, CONV_TM - 1, 0))
    conv = z_prev * cw_ref[0:1, :] + z * cw_ref[1:2, :] + z_next * cw_ref[2:3, :]
    o_ref[...] = (b * conv).astype(o_ref.dtype)


def _conv_in(x, mod, g_mix, conv_w_in, conv_w, layer, j):
    tm, tn = CONV_TM, CONV_TN
    nb = D_MODEL // tn
    w_spec = lambda part: pl.BlockSpec((None, D_MODEL, tn),
                                       lambda i, n: (j, 0, part * nb + n))
    x_parts = _row_parts(x)
    return pl.pallas_call(
        functools.partial(_conv_in_kernel, n_x=len(x_parts)),
        out_shape=jax.ShapeDtypeStruct((N_ROWS, D_MODEL), BF16),
        grid=(N_ROWS // tm, nb),
        in_specs=_row_specs(x, tm) + [
            _mod_spec(layer, tm),
            _gain_spec(layer),
            w_spec(0), w_spec(1), w_spec(2),
            pl.BlockSpec((None, CONV_W, tn), lambda i, n: (j, 0, n)),
        ],
        out_specs=pl.BlockSpec((tm, tn), lambda i, n: (i, n)),
        scratch_shapes=[pltpu.VMEM((tm, D_MODEL), BF16)],
        compiler_params=_params(2),
        name="conv_in",
    )(*x_parts, mod, g_mix, conv_w_in, conv_w_in, conv_w_in, conv_w)


PROJ_TM = 1024


def _proj_res_kernel(*refs, n_a, n_x):
    a_refs, w_ref, x_refs = refs[:n_a], refs[n_a], refs[n_a + 1:n_a + 1 + n_x]
    mod_ref, o_ref = refs[n_a + 1 + n_x:]
    whole = slice(None)
    y = _dot(_row_loader(a_refs, PROJ_TM)(whole), w_ref[...].astype(BF16))
    o_ref[...] = _row_loader(x_refs, PROJ_TM)(whole) + mod_ref[2:3, :] * y


def _proj_res(a, w, j, x, mod, layer):
    tm = PROJ_TM
    a_parts, x_parts = _row_parts(a), _row_parts(x)
    k = a_parts[0].shape[1]
    return pl.pallas_call(
        functools.partial(_proj_res_kernel, n_a=len(a_parts), n_x=len(x_parts)),
        out_shape=jax.ShapeDtypeStruct((N_ROWS, D_MODEL), F32),
        grid=(N_ROWS // tm,),
        in_specs=(_row_specs(a, tm) + [pl.BlockSpec((None, k, D_MODEL), lambda i: (j, 0, 0))]
                  + _row_specs(x, tm) + [_mod_spec(layer, tm)]),
        out_specs=pl.BlockSpec((tm, D_MODEL), lambda i: (i, 0)),
        compiler_params=_params(1),
        name="proj_res",
    )(*a_parts, w, *x_parts, mod)


GMLP_TM = 1024
GMLP_TN = 1024
GMLP_OUT_TM = 512


def _gelu_tanh(x):
    c = 0.7978845608028654
    return 0.5 * x * (1.0 + jnp.tanh(c * (x + 0.044715 * (x * x * x))))


def _gmlp_in_kernel(x_ref, mod_ref, g_ref, w_ref, b_ref, o_ref, h_ref):
    @pl.when(pl.program_id(1) == 0)
    def _():
        _norm_mod_to(x_ref, g_ref, mod_ref[1:2, :], mod_ref[0:1, :], h_ref)

    z = _dot(h_ref[...], w_ref[...].astype(BF16)) + b_ref[...]
    o_ref[...] = _gelu_tanh(z).astype(o_ref.dtype)


def _gmlp_in(x, mod, g_mix, w_in, b_in, layer, j):
    tm, tn = GMLP_TM, GMLP_TN
    n_out = 2 * GMLP_DIM
    return pl.pallas_call(
        _gmlp_in_kernel,
        out_shape=jax.ShapeDtypeStruct((N_ROWS, n_out), BF16),
        grid=(N_ROWS // tm, n_out // tn),
        in_specs=[
            pl.BlockSpec((tm, D_MODEL), lambda i, n: (i, 0)),
            _mod_spec(layer, tm),
            _gain_spec(layer),
            pl.BlockSpec((None, D_MODEL, tn), lambda i, n: (j, 0, n)),
            pl.BlockSpec((None, 1, tn), lambda i, n: (j, 0, n)),
        ],
        out_specs=pl.BlockSpec((tm, tn), lambda i, n: (i, n)),
        scratch_shapes=[pltpu.VMEM((tm, D_MODEL), BF16)],
        compiler_params=_params(2),
        name="gmlp_in",
    )(x, mod, g_mix, w_in, b_in.reshape(-1, 1, n_out))


def _gmlp_out_kernel(u_ref, v_ref, lng_ref, lnb_ref, ws_ref, bs_ref, w_ref, x_ref, mod_ref,
                     o_ref, a_ref):
    lng = lng_ref[...]
    lnb = lnb_ref[...]

    def chunk(c, carry):
        rows = pl.ds(pl.multiple_of(c * CHUNK, CHUNK), CHUNK)
        v = v_ref[rows, :].astype(F32)
        mu = jnp.mean(v, axis=-1, keepdims=True)
        vc = v - mu
        var = jnp.mean(vc * vc, axis=-1, keepdims=True)
        vn = (vc * lax.rsqrt(var + EPS) * lng + lnb).astype(BF16)
        for g in range(GMLP_GROUPS):
            cols = slice(g * GMLP_GROUP_DIM, (g + 1) * GMLP_GROUP_DIM)
            mixed = _dot(ws_ref[g].astype(BF16), vn[:, cols]) + bs_ref[:, g:g + 1]
            a_ref[rows, cols] = (u_ref[rows, cols].astype(F32) * mixed).astype(a_ref.dtype)
        return carry

    lax.fori_loop(0, u_ref.shape[0] // CHUNK, chunk, 0)
    y = _dot(a_ref[...], w_ref[...].astype(BF16))
    o_ref[...] = x_ref[...] + mod_ref[2:3, :] * y


def _gmlp_out(z, ln_g, ln_b, w_s, b_s, w_out, x, mod, layer, j):
    tm = GMLP_OUT_TM
    return pl.pallas_call(
        _gmlp_out_kernel,
        out_shape=jax.ShapeDtypeStruct((N_ROWS, D_MODEL), F32),
        grid=(N_ROWS // tm,),
        in_specs=[
            pl.BlockSpec((tm, GMLP_DIM), lambda i: (i, 0)),
            pl.BlockSpec((tm, GMLP_DIM), lambda i: (i, 1)),
            pl.BlockSpec((None, 1, GMLP_DIM), lambda i: (j, 0, 0)),
            pl.BlockSpec((None, 1, GMLP_DIM), lambda i: (j, 0, 0)),
            pl.BlockSpec((None, GMLP_GROUPS, CHUNK, CHUNK), lambda i: (j, 0, 0, 0)),
            pl.BlockSpec((None, CHUNK, GMLP_GROUPS), lambda i: (j, 0, 0)),
            pl.BlockSpec((None, GMLP_DIM, D_MODEL), lambda i: (j, 0, 0)),
            pl.BlockSpec((tm, D_MODEL), lambda i: (i, 0)),
            _mod_spec(layer, tm),
        ],
        out_specs=pl.BlockSpec((tm, D_MODEL), lambda i: (i, 0)),
        scratch_shapes=[pltpu.VMEM((tm, GMLP_DIM), BF16)],
        compiler_params=_params(1),
        name="gmlp_out",
    )(z, z, ln_g.reshape(-1, 1, GMLP_DIM), ln_b.reshape(-1, 1, GMLP_DIM), w_s,
      jnp.swapaxes(b_s, 1, 2), w_out, x, mod)


QKV_TM = 512
ATTN_TQ = 256
ATTN_PROMPT_SEQS = 4


def _rope_tables(tm):
    pos = np.arange(DEC_SEQ)
    row = (pos // GRID_W).astype(np.float32)
    col = (pos % GRID_W).astype(np.float32)
    n_freq = HEAD_DIM // 4
    inv_freq = np.power(np.float32(ROPE_THETA), -np.arange(n_freq, dtype=np.float32) / n_freq)
    ang = np.concatenate([row[:, None] * inv_freq, col[:, None] * inv_freq], axis=-1)
    cos = np.repeat(np.cos(ang), 2, axis=-1)
    sign = np.tile(np.array([-1.0, 1.0], np.float32), HEAD_DIM // 2)
    sin = np.repeat(np.sin(ang), 2, axis=-1) * sign
    cos = np.concatenate([np.ones((tm, HEAD_DIM), np.float32), cos], axis=0)
    sin = np.concatenate([np.zeros((tm, HEAD_DIM), np.float32), sin], axis=0)
    return jnp.asarray(cos, F32), jnp.asarray(sin, F32)


def _swap_pairs(x):
    even = (lax.broadcasted_iota(jnp.int32, x.shape, 1) & 1) == 0
    return jnp.where(even, pltpu.roll(x, HEAD_DIM - 1, 1), pltpu.roll(x, 1, 1))


def _qkv_kernel(x_ref, mod_ref, g_ref, w_ref, qg_ref, kg_ref, cos_ref, sin_ref, o_ref,
                h_ref, wsw_ref):
    n_rot_heads = N_HEADS + N_KV_HEADS

    @pl.when(pl.program_id(0) == 0)
    def _():
        for hd in range(n_rot_heads):
            cols = slice(hd * HEAD_DIM, (hd + 1) * HEAD_DIM)
            wsw_ref[:, cols] = _swap_pairs(w_ref[:, cols]).astype(BF16)

    _norm_mod_to(x_ref, g_ref, mod_ref[1:2, :], mod_ref[0:1, :], h_ref)
    y = _dot(h_ref[...], w_ref[...].astype(BF16))
    y_sw = _dot(h_ref[...], wsw_ref[...])
    cos = cos_ref[...]
    sin = sin_ref[...]
    for hd in range(n_rot_heads):
        cols = slice(hd * HEAD_DIM, (hd + 1) * HEAD_DIM)
        yh = y[:, cols]
        gain = qg_ref[...] if hd < N_HEADS else kg_ref[...]
        inv = lax.rsqrt(jnp.mean(yh * yh, axis=-1, keepdims=True) + EPS)
        o_ref[:, cols] = (yh * (gain * cos) + y_sw[:, cols] * (_swap_pairs(gain) * sin)) * inv
    v_cols = slice(n_rot_heads * HEAD_DIM, QKV_OUT)
    o_ref[:, v_cols] = y[:, v_cols]


def _qkv(x, mod, g_mix, w_qkv, q_g, k_g, layer, j):
    tm = QKV_TM
    cos, sin = _rope_tables(tm)
    n_prompt = N_PROMPT_ROWS // tm
    per_seq = DEC_SEQ // tm
    tbl = lambda i: (jnp.where(i < n_prompt, 0, 1 + (i - n_prompt) % per_seq), 0)
    return pl.pallas_call(
        _qkv_kernel,
        out_shape=jax.ShapeDtypeStruct((N_ROWS, QKV_OUT), F32),
        grid=(N_ROWS // tm,),
        in_specs=[
            pl.BlockSpec((tm, D_MODEL), lambda i: (i, 0)),
            _mod_spec(layer, tm),
            _gain_spec(layer),
            pl.BlockSpec((None, D_MODEL, QKV_OUT), lambda i: (j, 0, 0)),
            pl.BlockSpec((None, 1, HEAD_DIM), lambda i: (j, 0, 0)),
            pl.BlockSpec((None, 1, HEAD_DIM), lambda i: (j, 0, 0)),
            pl.BlockSpec((tm, HEAD_DIM), tbl),
            pl.BlockSpec((tm, HEAD_DIM), tbl),
        ],
        out_specs=pl.BlockSpec((tm, QKV_OUT), lambda i: (i, 0)),
        scratch_shapes=[pltpu.VMEM((tm, D_MODEL), BF16),
                        pltpu.VMEM((D_MODEL, (N_HEADS + N_KV_HEADS) * HEAD_DIM), BF16)],
        compiler_params=_params(1),
        name="qkv",
    )(x, mod, g_mix, w_qkv, q_g.reshape(-1, 1, HEAD_DIM), k_g.reshape(-1, 1, HEAD_DIM), cos, sin)


def _attn_kernel(*refs, has_ctx, n_seq):
    if has_ctx:
        q_ref, k_ref, v_ref, ck_ref, cv_ref, o_ref = refs
    else:
        q_ref, k_ref, v_ref, o_ref = refs
    nt = (((1,), (1,)), ((), ()))
    if has_ctx:
        ck = ck_ref[...].astype(BF16)
        cv = cv_ref[...].astype(BF16)
    q_len = q_ref.shape[0] // n_seq
    k_len = k_ref.shape[0] // n_seq
    for sq in range(n_seq):
        q_rows = slice(sq * q_len, (sq + 1) * q_len)
        k_rows = slice(sq * k_len, (sq + 1) * k_len)
        k = k_ref[k_rows, :].astype(BF16)
        v = v_ref[k_rows, :].astype(BF16)
        for g in range(Q_PER_KV):
            cols = slice(g * HEAD_DIM, (g + 1) * HEAD_DIM)
            q = (q_ref[q_rows, cols] * (HEAD_DIM ** -0.5)).astype(BF16)
            s = lax.dot_general(q, k, nt, preferred_element_type=F32)
            m = jnp.max(s, axis=-1, keepdims=True)
            if has_ctx:
                sc = lax.dot_general(q, ck, nt, preferred_element_type=F32)
                m = jnp.maximum(m, jnp.max(sc, axis=-1, keepdims=True))
            p = jnp.exp(s - m)
            den = jnp.sum(p, axis=-1, keepdims=True)
            o = _dot(p.astype(BF16), v)
            if has_ctx:
                pc = jnp.exp(sc - m)
                den = den + jnp.sum(pc, axis=-1, keepdims=True)
                o = o + _dot(pc.astype(BF16), cv)
            o_ref[q_rows, cols] = (o / den).astype(o_ref.dtype)


def _attention(qkv, ctx_k, ctx_v):
    group_w = Q_PER_KV * HEAD_DIM
    k_col = N_HEADS
    v_col = N_HEADS + N_KV_HEADS

    def call(n_batch, seq, tq, row0, ctx, n_seq):
        q_blocks = seq // tq
        q_rows, kv_rows = n_seq * tq, n_seq * seq
        o_row = lambda b, qi: b * q_blocks + qi
        kv_row = lambda b: row0 // kv_rows + b
        in_specs = [
            pl.BlockSpec((q_rows, group_w), lambda b, kh, qi: (row0 // q_rows + o_row(b, qi), kh)),
            pl.BlockSpec((kv_rows, HEAD_DIM), lambda b, kh, qi: (kv_row(b), k_col + kh)),
            pl.BlockSpec((kv_rows, HEAD_DIM), lambda b, kh, qi: (kv_row(b), v_col + kh)),
        ]
        args = [qkv, qkv, qkv]
        if ctx is not None:
            ctx_spec = pl.BlockSpec((None, PAST_LEN, HEAD_DIM), lambda b, kh, qi: (b, 0, kh))
            in_specs += [ctx_spec, ctx_spec]
            args += list(ctx)
        return pl.pallas_call(
            functools.partial(_attn_kernel, has_ctx=ctx is not None, n_seq=n_seq),
            out_shape=jax.ShapeDtypeStruct((n_batch * seq, N_HEADS * HEAD_DIM), BF16),
            grid=(n_batch // n_seq, N_KV_HEADS, q_blocks),
            in_specs=in_specs,
            out_specs=pl.BlockSpec((q_rows, group_w), lambda b, kh, qi: (o_row(b, qi), kh)),
            compiler_params=_params(3),
            name="attention",
        )(*args)

    o_prompt = call(BATCH, SEQ, SEQ, 0, None, ATTN_PROMPT_SEQS)
    o_sample = call(DEC_BATCH, DEC_SEQ, ATTN_TQ, N_PROMPT_ROWS, (ctx_k, ctx_v), 1)
    return o_prompt, o_sample


FFN_TM = 1024
FFN_TF = 256
TOP_K = 2
ROUTER_TM = 1024
ROUTE_LANES = 128
MOE_TF = 512
MOE_TILE = 1024
MOE_SUB = 256
MOE_MAX_TILES = TOP_K * N_ROWS // MOE_TILE + N_EXPERTS
MOE_ROWS = MOE_MAX_TILES * MOE_TILE
SCATTER_TM = 512
COMBINE_TM = 256
WIN_CHUNK = 128
WIN_CHUNKS = COMBINE_TM // WIN_CHUNK + 1


def _swiglu_hidden(h, wg, wu):
    gate = _dot(h, wg)
    return (gate * jax.nn.sigmoid(gate) * _dot(h, wu)).astype(BF16)


def _ffn_kernel(x_ref, mod_ref, g_ref, wg_ref, wu_ref, wd_ref, o_ref, h_ref):
    f = pl.program_id(1)

    @pl.when(f == 0)
    def _():
        _norm_mod_to(x_ref, g_ref, mod_ref[4:5, :], mod_ref[3:4, :], h_ref)
        o_ref[...] = jnp.zeros_like(o_ref)

    a = _swiglu_hidden(h_ref[...], wg_ref[...].astype(BF16), wu_ref[...].astype(BF16))
    o_ref[...] += _dot(a, wd_ref[...].astype(BF16))

    @pl.when(f == pl.num_programs(1) - 1)
    def _():
        o_ref[...] = x_ref[...] + mod_ref[5:6, :] * o_ref[...]


def _ffn(x, mod, g_ffn, w_gate, w_up, w_down, layer, li):
    tm, tf = FFN_TM, FFN_TF
    return pl.pallas_call(
        _ffn_kernel,
        out_shape=jax.ShapeDtypeStruct((N_ROWS, D_MODEL), F32),
        grid=(N_ROWS // tm, D_FF // tf),
        in_specs=[
            pl.BlockSpec((tm, D_MODEL), lambda i, f: (i, 0)),
            _mod_spec(layer, tm),
            _gain_spec(layer),
            pl.BlockSpec((None, D_MODEL, tf), lambda i, f: (li, 0, f)),
            pl.BlockSpec((None, D_MODEL, tf), lambda i, f: (li, 0, f)),
            pl.BlockSpec((None, tf, D_MODEL), lambda i, f: (li, f, 0)),
        ],
        out_specs=pl.BlockSpec((tm, D_MODEL), lambda i, f: (i, 0)),
        scratch_shapes=[pltpu.VMEM((tm, D_MODEL), BF16)],
        compiler_params=_params(2),
        name="ffn",
    )(x, mod, g_ffn, w_gate, w_up, w_down)


def _split_bf16(x):
    hi = x.astype(BF16)
    return hi, (x - hi.astype(F32)).astype(BF16)


def _router_kernel(x_ref, mod_ref, g_ref, w_ref, route_ref, route_t_ref, cnt_ref, base_ref,
                   h_ref, seen_ref):
    @pl.when(pl.program_id(0) == 0)
    def _():
        seen_ref[...] = jnp.zeros_like(seen_ref)

    _norm_mod_to(x_ref, g_ref, mod_ref[4:5, :], mod_ref[3:4, :], h_ref)
    h_hi, h_lo = _split_bf16(h_ref[...])
    w_hi, w_lo = _split_bf16(w_ref[...])
    logits = _dot(h_hi, w_hi) + (_dot(h_hi, w_lo) + _dot(h_lo, w_hi))
    lane = lax.broadcasted_iota(jnp.int32, logits.shape, 1)
    logits = jnp.where(lane < N_EXPERTS, logits, -jnp.inf)
    m1 = jnp.max(logits, axis=-1, keepdims=True)
    i1 = jnp.min(jnp.where(logits == m1, lane, N_EXPERTS), axis=-1, keepdims=True)
    rest = jnp.where(lane == i1, -jnp.inf, logits)
    m2 = jnp.max(rest, axis=-1, keepdims=True)
    i2 = jnp.min(jnp.where(rest == m2, lane, N_EXPERTS), axis=-1, keepdims=True)
    e2 = jnp.exp(m2 - m1)
    w1 = 1.0 / (1.0 + e2)
    chosen = jnp.where((lane == i1) | (lane == i2), 1.0, 0.0)
    tm = chosen.shape[0]
    blk = COMBINE_TM
    earlier = (lax.broadcasted_iota(jnp.int32, (blk, blk), 1)
               < lax.broadcasted_iota(jnp.int32, (blk, blk), 0))
    earlier = jnp.where(earlier, 1.0, 0.0).astype(BF16)
    base_ref[...] = jnp.zeros_like(base_ref)
    seen = seen_ref[...]
    priors = []
    for k in range(tm // blk):
        rows = slice(k * blk, (k + 1) * blk)
        base_ref[k:k + 1, :] = seen
        priors.append(_dot(earlier, chosen[rows].astype(BF16)) + seen)
        seen = seen + jnp.sum(chosen[rows], axis=0, keepdims=True)
    seen_ref[...] = seen
    prior = jnp.concatenate(priors, axis=0)
    rank1 = jnp.sum(jnp.where(lane == i1, prior, 0.0), axis=-1, keepdims=True)
    rank2 = jnp.sum(jnp.where(lane == i2, prior, 0.0), axis=-1, keepdims=True)
    route = jnp.zeros_like(logits)
    for k, field in enumerate((w1, e2 * w1, i1.astype(F32), i2.astype(F32), rank1, rank2)):
        route = jnp.where(lane == k, field, route)
    route_ref[...] = route
    route_t_ref[...] = jnp.transpose(route)[:SUBLANES, :]
    cnt_ref[...] = jnp.broadcast_to(seen, cnt_ref.shape)


def _router(x, mod, g_ffn, w_router, layer, li):
    tm = ROUTER_TM
    w_pad = jnp.pad(w_router, ((0, 0), (0, 0), (0, ROUTE_LANES - N_EXPERTS)))
    return pl.pallas_call(
        _router_kernel,
        out_shape=(jax.ShapeDtypeStruct((N_ROWS, ROUTE_LANES), F32),
                   jax.ShapeDtypeStruct((SUBLANES, N_ROWS), F32),
                   jax.ShapeDtypeStruct((SUBLANES, ROUTE_LANES), F32),
                   jax.ShapeDtypeStruct((N_ROWS // tm, SUBLANES, ROUTE_LANES), F32)),
        grid=(N_ROWS // tm,),
        in_specs=[
            pl.BlockSpec((tm, D_MODEL), lambda i: (i, 0)),
            _mod_spec(layer, tm),
            _gain_spec(layer),
            pl.BlockSpec((None, D_MODEL, ROUTE_LANES), lambda i: (li, 0, 0)),
        ],
        out_specs=(pl.BlockSpec((tm, ROUTE_LANES), lambda i: (i, 0)),
                   pl.BlockSpec((SUBLANES, tm), lambda i: (0, i)),
                   pl.BlockSpec((SUBLANES, ROUTE_LANES), lambda i: (0, 0)),
                   pl.BlockSpec((None, SUBLANES, ROUTE_LANES), lambda i: (i, 0, 0))),
        scratch_shapes=[pltpu.VMEM((tm, D_MODEL), F32), pltpu.VMEM((1, ROUTE_LANES), F32)],
        compiler_params=_params(1),
        name="router",
    )(x, mod, g_ffn, w_pad)


def _dispatch_plan(route_t, cnt, tile_base):
    counts = cnt[0, :N_EXPERTS].astype(jnp.int32)
    e1, e2, r1, r2 = (route_t[k].astype(jnp.int32) for k in (2, 3, 4, 5))
    n_tiles = (counts + MOE_TILE - 1) // MOE_TILE
    tile_end = jnp.cumsum(n_tiles)
    tile_start = tile_end - n_tiles
    row_start = tile_start * MOE_TILE
    pos = jnp.concatenate([row_start[e1] + r1, row_start[e2] + r2])
    n_active = tile_end[-1]
    t = jnp.arange(MOE_MAX_TILES, dtype=jnp.int32)
    tile_block = jnp.minimum(t, n_active - 1)
    tile_expert = jnp.sum(tile_block[:, None] >= tile_end[None, :], axis=1).astype(jnp.int32)
    valid = counts[tile_expert] - (tile_block - tile_start[tile_expert]) * MOE_TILE
    n_sub = (jnp.clip(valid, 0, MOE_TILE) + MOE_SUB - 1) // MOE_SUB
    tile_nsub = jnp.where(t < n_active, n_sub, 0).astype(jnp.int32)
    pad = jnp.concatenate([row_start + counts, (-counts) % MOE_SUB]).astype(jnp.int32)
    base = tile_base[:, :ROUTER_TM // COMBINE_TM, :N_EXPERTS].astype(jnp.int32)
    base = base.reshape(N_ROWS // COMBINE_TM, N_EXPERTS)
    lo = row_start[None, :] + base
    hi = row_start[None, :] + jnp.concatenate([base[1:], counts[None, :]], axis=0)
    win_chunk0 = lo // WIN_CHUNK
    win_chunks = jnp.where(hi > lo, (hi + WIN_CHUNK - 1) // WIN_CHUNK - win_chunk0, 0)
    return dict(pos=pos, pad=pad, tile_expert=tile_expert, tile_nsub=tile_nsub,
                tile_block=tile_block, row_start=row_start.astype(jnp.int32),
                win_chunk0=win_chunk0.reshape(-1).astype(jnp.int32),
                win_chunks=win_chunks.reshape(-1).astype(jnp.int32))


def _for_rows(n_rows, body):
    def group(j, carry):
        r0 = pl.multiple_of(j * SUBLANES, SUBLANES)
        for u in range(SUBLANES):
            body(r0 + u)
        return carry
    lax.fori_loop(0, n_rows // SUBLANES, group, 0)


def _scatter_kernel(pos_ref, pad_ref, tn_ref, x_ref, mod_ref, g_ref, xs_ref, h_ref, zero_ref,
                    sem, fill_sem):
    i = pl.program_id(0)
    ts = x_ref.shape[0]
    slot = lax.rem(i, 2)
    _norm_mod_to(x_ref, g_ref, mod_ref[4:5, :], mod_ref[3:4, :], h_ref.at[slot])

    def for_row_copies(step, step_slot, act):
        first = step * ts

        def row(r):
            for k in range(TOP_K):
                dst = pos_ref[k * N_ROWS + first + r]
                act(pltpu.make_async_copy(h_ref.at[step_slot, pl.ds(r, 1)],
                                          xs_ref.at[pl.ds(dst, 1)], sem.at[step_slot]), k)
        _for_rows(ts, row)

    for_row_copies(i, slot, lambda cp, k: cp.start(priority=k))

    @pl.when(i == 0)
    def _():
        zero_ref[...] = jnp.zeros_like(zero_ref)

        def fills(act):
            for e in range(N_EXPERTS):
                def row_body(r, carry, e=e):
                    act(pltpu.make_async_copy(zero_ref.at[pl.ds(0, 1)],
                                              xs_ref.at[pl.ds(pad_ref[e] + r, 1)], fill_sem))
                    return carry
                lax.fori_loop(0, pad_ref[N_EXPERTS + e], row_body, 0)

            def tile_body(t, carry):
                def sub_body(s, carry):
                    row = pl.multiple_of(t * MOE_TILE + s * MOE_SUB, MOE_SUB)
                    act(pltpu.make_async_copy(zero_ref, xs_ref.at[pl.ds(row, MOE_SUB)], fill_sem))
                    return carry
                return lax.fori_loop(tn_ref[t], MOE_TILE // MOE_SUB, sub_body, carry)
            lax.fori_loop(0, MOE_MAX_TILES, tile_body, 0)

        fills(lambda cp: cp.start())
        fills(lambda cp: cp.wait())

    @pl.when(i > 0)
    def _():
        for_row_copies(i - 1, 1 - slot, lambda cp, k: cp.wait())

    @pl.when(i == pl.num_programs(0) - 1)
    def _():
        for_row_copies(i, slot, lambda cp, k: cp.wait())


def _moe_scatter(x, mod, g_ffn, plan, layer):
    ts = SCATTER_TM
    grid_spec = pltpu.PrefetchScalarGridSpec(
        num_scalar_prefetch=3,
        grid=(N_ROWS // ts,),
        in_specs=[
            pl.BlockSpec((ts, D_MODEL), lambda i, *_: (i, 0)),
            _mod_spec(layer, ts),
            _gain_spec(layer),
        ],
        out_specs=pl.BlockSpec(memory_space=pl.ANY),
        scratch_shapes=[pltpu.VMEM((2, ts, D_MODEL), F32),
                        pltpu.VMEM((MOE_SUB, D_MODEL), F32),
                        pltpu.SemaphoreType.DMA((2,)), pltpu.SemaphoreType.DMA(())],
    )
    return pl.pallas_call(
        _scatter_kernel,
        out_shape=jax.ShapeDtypeStruct((MOE_ROWS, D_MODEL), F32),
        grid_spec=grid_spec,
        compiler_params=_params(1),
        name="moe_scatter",
    )(plan["pos"], plan["pad"], plan["tile_nsub"], x, mod, g_ffn)


def _moe_kernel(te_ref, tn_ref, tb_ref, x_ref, wg_ref, wu_ref, wd_ref, o_ref,
                xb_ref, acc_ref):
    n_sub = tn_ref[pl.program_id(0)]
    first_f = pl.program_id(1) == 0

    @pl.when(first_f)
    def _():
        acc_ref[...] = jnp.zeros_like(acc_ref)

    for n in range(1, MOE_TILE // MOE_SUB + 1):
        @pl.when(n_sub == n)
        def _(n=n):
            rows = slice(0, n * MOE_SUB)

            @pl.when(first_f)
            def _():
                xb_ref[rows, :] = x_ref[rows, :].astype(BF16)

            a = _swiglu_hidden(xb_ref[rows, :], wg_ref[...].astype(BF16),
                               wu_ref[...].astype(BF16))
            acc_ref[rows, :] += _dot(a, wd_ref[...].astype(BF16))

    @pl.when(pl.program_id(1) == pl.num_programs(1) - 1)
    def _():
        o_ref[...] = acc_ref[...].astype(o_ref.dtype)


def _moe_experts(xs, plan, w_gate, w_up, w_down, li):
    tile, tf = MOE_TILE, MOE_TF
    nf = D_FF_EXPERT // tf
    frozen_f = lambda i, f, tn: jnp.where(tn[i] > 0, f, nf - 1)
    grid_spec = pltpu.PrefetchScalarGridSpec(
        num_scalar_prefetch=3,
        grid=(MOE_MAX_TILES, nf),
        in_specs=[
            pl.BlockSpec((tile, D_MODEL), lambda i, f, te, tn, tb: (tb[i], 0)),
            pl.BlockSpec((None, None, D_MODEL, tf),
                         lambda i, f, te, tn, tb: (li, te[i], 0, frozen_f(i, f, tn))),
            pl.BlockSpec((None, None, D_MODEL, tf),
                         lambda i, f, te, tn, tb: (li, te[i], 0, frozen_f(i, f, tn))),
            pl.BlockSpec((None, None, tf, D_MODEL),
                         lambda i, f, te, tn, tb: (li, te[i], frozen_f(i, f, tn), 0)),
        ],
        out_specs=pl.BlockSpec((tile, D_MODEL), lambda i, f, te, tn, tb: (i, 0)),
        scratch_shapes=[pltpu.VMEM((tile, D_MODEL), BF16), pltpu.VMEM((tile, D_MODEL), F32)],
    )
    return pl.pallas_call(
        _moe_kernel,
        out_shape=jax.ShapeDtypeStruct((MOE_ROWS, D_MODEL), BF16),
        grid_spec=grid_spec,
        compiler_params=_params(2),
        name="moe_experts",
    )(plan["tile_expert"], plan["tile_nsub"], plan["tile_block"], xs, w_gate, w_up, w_down)


def _combine_kernel(start_ref, c0_ref, n_ref, x_ref, mod_ref, route_ref, ys_ref, *refs):
    *o_refs, win_ref, sem = refs
    i = pl.program_id(0)
    slot = lax.rem(i, 2)

    def chunk_slots(tile):
        for e in range(N_EXPERTS):
            c0 = c0_ref[tile * N_EXPERTS + e]
            n = n_ref[tile * N_EXPERTS + e]
            for j in range(WIN_CHUNKS):
                yield e * WIN_CHUNKS + j, c0 + j, j < n

    def window_copies(tile, tile_slot, act):
        for s, chunk, needed in chunk_slots(tile):
            @pl.when(needed)
            def _(s=s, chunk=chunk):
                rows = pl.ds(pl.multiple_of(chunk * WIN_CHUNK, WIN_CHUNK), WIN_CHUNK)
                act(pltpu.make_async_copy(ys_ref.at[rows],
                                          win_ref.at[tile_slot, pl.ds(s * WIN_CHUNK, WIN_CHUNK)],
                                          sem.at[tile_slot]))

    @pl.when(i == 0)
    def _():
        win_ref[...] = jnp.zeros_like(win_ref)
        window_copies(i, slot, lambda cp: cp.start())

    @pl.when(i + 1 < pl.num_programs(0))
    def _():
        window_copies(i + 1, 1 - slot, lambda cp: cp.start())

    window_copies(i, slot, lambda cp: cp.wait())

    w1, w2, e1, e2, pos1, pos2 = (route_ref[:, k:k + 1] for k in range(6))
    for e in range(N_EXPERTS):
        first_row = start_ref[e].astype(F32)
        pos1 = pos1 + jnp.where(e1 == e, first_row, 0.0)
        pos2 = pos2 + jnp.where(e2 == e, first_row, 0.0)
    lane = lax.broadcasted_iota(jnp.int32, (x_ref.shape[0], WIN_CHUNK), 1).astype(F32)
    blocks = []
    for s, chunk, needed in chunk_slots(i):
        row0 = jnp.where(needed, chunk * WIN_CHUNK, -MOE_ROWS).astype(F32)
        rows = lane + row0
        weight = jnp.where(pos1 == rows, w1, 0.0) + jnp.where(pos2 == rows, w2, 0.0)
        blocks.append(weight.astype(BF16))
    y = _dot(jnp.concatenate(blocks, axis=1), win_ref[slot])
    out = x_ref[...] + mod_ref[5:6, :] * y
    if len(o_refs) == 1:
        o_refs[0][...] = out
    else:
        is_prompt = i < N_PROMPT_ROWS // x_ref.shape[0]

        @pl.when(is_prompt)
        def _():
            o_refs[0][...] = out

        @pl.when(jnp.logical_not(is_prompt))
        def _():
            o_refs[1][...] = out


def _moe_combine(x, mod, route, ys, plan, layer, split_out):
    tc = COMBINE_TM
    if split_out:
        half = jax.ShapeDtypeStruct((N_PROMPT_ROWS, D_MODEL), F32)
        out_shape = (half, jax.ShapeDtypeStruct((N_SAMPLE_ROWS, D_MODEL), F32))
        out_specs = tuple(_row_specs((half, half), tc))
    else:
        out_shape = jax.ShapeDtypeStruct((N_ROWS, D_MODEL), F32)
        out_specs = pl.BlockSpec((tc, D_MODEL), lambda i, *_: (i, 0))
    grid_spec = pltpu.PrefetchScalarGridSpec(
        num_scalar_prefetch=3,
        grid=(N_ROWS // tc,),
        in_specs=[
            pl.BlockSpec((tc, D_MODEL), lambda i, *_: (i, 0)),
            _mod_spec(layer, tc),
            pl.BlockSpec((tc, ROUTE_LANES), lambda i, *_: (i, 0)),
            pl.BlockSpec(memory_space=pl.ANY),
        ],
        out_specs=out_specs,
        scratch_shapes=[pltpu.VMEM((2, N_EXPERTS * WIN_CHUNKS * WIN_CHUNK, D_MODEL), BF16),
                        pltpu.SemaphoreType.DMA((2,))],
    )
    return pl.pallas_call(
        _combine_kernel,
        out_shape=out_shape,
        grid_spec=grid_spec,
        compiler_params=_params(1),
        name="moe_combine",
    )(plan["row_start"], plan["win_chunk0"], plan["win_chunks"], x, mod, route, ys)


def kernel(x_prompt, x_sample, cache_k, cache_v, c, c_ctx, w_mod, b_mod, g_mix, g_ffn,
           conv_w_in, conv_w, conv_w_out,
           gmlp_w_in, gmlp_b_in, gmlp_ln_g, gmlp_ln_b, gmlp_w_s, gmlp_b_s, gmlp_w_out,
           attn_w_qkv, attn_q_g, attn_k_g, attn_w_o,
           ffn_w_gate, ffn_w_up, ffn_w_down,
           moe_w_router, moe_w_gate, moe_w_up, moe_w_down):
    x = (x_prompt.reshape(N_PROMPT_ROWS, D_MODEL), x_sample.reshape(N_SAMPLE_ROWS, D_MODEL))
    cond = jnp.zeros((N_COND_ROWS, D_MODEL), F32).at[0].set(c_ctx).at[1:1 + DEC_BATCH].set(c)
    mod = _modulation(cond, w_mod, b_mod)
    g_mix3 = g_mix.reshape(DEPTH, 1, D_MODEL)
    g_ffn3 = g_ffn.reshape(DEPTH, 1, D_MODEL)
    n_attn = cache_k.shape[1]
    ctx_k = cache_k.reshape(DEC_BATCH, n_attn, PAST_LEN, N_KV_HEADS * HEAD_DIM)
    ctx_v = cache_v.reshape(DEC_BATCH, n_attn, PAST_LEN, N_KV_HEADS * HEAD_DIM)

    new_k, new_v = [], []
    for layer in range(DEPTH):
        kind, j = layer % N_MIXERS, layer // N_MIXERS
        if kind == 0:
            a = _conv_in(x, mod, g_mix3, conv_w_in, conv_w, layer, j)
            x = _proj_res(a, conv_w_out, j, x, mod, layer)
        elif kind == 1:
            z = _gmlp_in(x, mod, g_mix3, gmlp_w_in, gmlp_b_in, layer, j)
            x = _gmlp_out(z, gmlp_ln_g, gmlp_ln_b, gmlp_w_s, gmlp_b_s, gmlp_w_out, x, mod,
                          layer, j)
        else:
            qkv = _qkv(x, mod, g_mix3, attn_w_qkv, attn_q_g, attn_k_g, layer, j)
            k_cols = slice(N_HEADS * HEAD_DIM, (N_HEADS + N_KV_HEADS) * HEAD_DIM)
            v_cols = slice((N_HEADS + N_KV_HEADS) * HEAD_DIM, QKV_OUT)
            new_k.append(qkv[:N_PROMPT_ROWS, k_cols].reshape(BATCH, SEQ, N_KV_HEADS, HEAD_DIM))
            new_v.append(qkv[:N_PROMPT_ROWS, v_cols].reshape(BATCH, SEQ, N_KV_HEADS, HEAD_DIM))
            a = _attention(qkv, ctx_k[:, j], ctx_v[:, j])
            x = _proj_res(a, attn_w_o, j, x, mod, layer)
        li = layer // 2
        if layer % 2 == 0:
            x = _ffn(x, mod, g_ffn3, ffn_w_gate, ffn_w_up, ffn_w_down, layer, li)
        else:
            route, route_t, cnt, tile_base = _router(x, mod, g_ffn3, moe_w_router, layer, li)
            plan = _dispatch_plan(route_t, cnt, tile_base)
            xs = _moe_scatter(x, mod, g_ffn3, plan, layer)
            ys = _moe_experts(xs, plan, moe_w_gate, moe_w_up, moe_w_down, li)
            x = _moe_combine(x, mod, route, ys, plan, layer, split_out=layer == DEPTH - 1)

    if DEPTH % 2 == 0:
        y_prompt, y_sample = x
    else:
        y_prompt, y_sample = x[:N_PROMPT_ROWS], x[N_PROMPT_ROWS:]
    return (y_prompt.reshape(BATCH, SEQ, D_MODEL), y_sample.reshape(DEC_BATCH, DEC_SEQ, D_MODEL),
            jnp.stack(new_k, axis=1), jnp.stack(new_v, axis=1))
```

```python
import functools

import jax
import jax.numpy as jnp
import numpy as np
from jax import lax
from jax.experimental import pallas as pl
from jax.experimental.pallas import tpu as pltpu

F32 = jnp.float32
BF16 = jnp.bfloat16

D_MODEL = 1024
BATCH = 16
SEQ = 256
DEPTH = 4
DEC_BATCH = 2
DEC_SEQ = 2048
PAST_LEN = 256
GRID_W = 64
N_MIXERS = 3
N_HEADS = 8
N_KV_HEADS = 2
HEAD_DIM = 128
Q_PER_KV = N_HEADS // N_KV_HEADS
ROPE_THETA = 10000.0
CONV_W = 3
CHUNK = 128
GMLP_DIM = 2 * D_MODEL
GMLP_GROUPS = 8
GMLP_GROUP_DIM = GMLP_DIM // GMLP_GROUPS
D_FF = 2816
N_EXPERTS = 8
D_FF_EXPERT = 3584
EPS = 1e-6
QKV_OUT = (N_HEADS + 2 * N_KV_HEADS) * HEAD_DIM

N_PROMPT_ROWS = BATCH * SEQ
N_SAMPLE_ROWS = DEC_BATCH * DEC_SEQ
N_ROWS = N_PROMPT_ROWS + N_SAMPLE_ROWS
SUBLANES = 8
N_COND_ROWS = SUBLANES

V7X_VMEM_BYTES = 64 * 1024 * 1024
VMEM_LIMIT_BYTES = V7X_VMEM_BYTES - 8 * 1024 * 1024
NORM_ROWS = 256


def _params(n_axes):
    return pltpu.CompilerParams(
        dimension_semantics=("arbitrary",) * n_axes,
        vmem_limit_bytes=VMEM_LIMIT_BYTES)


def _dot(a, b):
    return jnp.dot(a, b, preferred_element_type=F32)


def _cond_row(i, tm):
    n_prompt = N_PROMPT_ROWS // tm
    per_seq = DEC_SEQ // tm
    return jnp.where(i < n_prompt, 0, 1 + (i - n_prompt) // per_seq)


def _mod_spec(layer, tm):
    return pl.BlockSpec((None, None, 6, D_MODEL),
                        lambda i, *_: (layer, _cond_row(i, tm), 0, 0))


def _gain_spec(layer):
    return pl.BlockSpec((None, 1, D_MODEL), lambda *_: (layer, 0, 0))


def _row_parts(x):
    return tuple(x) if isinstance(x, (tuple, list)) else (x,)


def _row_specs(x, tm):
    parts = _row_parts(x)
    block = (tm, parts[0].shape[1])
    if len(parts) == 1:
        return [pl.BlockSpec(block, lambda i, *_: (i, 0))]
    n_prompt = N_PROMPT_ROWS // tm
    return [pl.BlockSpec(block, lambda i, *_: (jnp.minimum(i, n_prompt - 1), 0)),
            pl.BlockSpec(block, lambda i, *_: (jnp.maximum(i - n_prompt, 0), 0))]


def _row_loader(refs, tm):
    if len(refs) == 1:
        return lambda rows: refs[0][rows, :]
    is_prompt = pl.program_id(0) < N_PROMPT_ROWS // tm
    return lambda rows: jnp.where(is_prompt, refs[0][rows, :], refs[1][rows, :])


def _norm_mod_rows(load, n_rows, g_ref, scale, shift, h_ref):
    gs = g_ref[...] * (1.0 + scale)

    def body(r, carry):
        rows = pl.ds(pl.multiple_of(r * NORM_ROWS, NORM_ROWS), NORM_ROWS)
        x = load(rows)
        ms = jnp.mean(x * x, axis=-1, keepdims=True)
        h_ref[rows, :] = (x * lax.rsqrt(ms + EPS) * gs + shift).astype(h_ref.dtype)
        return carry

    lax.fori_loop(0, n_rows // NORM_ROWS, body, 0)


def _norm_mod_to(x_ref, g_ref, scale, shift, h_ref):
    _norm_mod_rows(lambda rows: x_ref[rows, :], x_ref.shape[0], g_ref, scale, shift, h_ref)


def _mod_kernel(c_ref, w_ref, b_ref, o_ref):
    cnd = c_ref[...]
    s = (cnd * jax.nn.sigmoid(cnd)).astype(BF16)
    o_ref[...] = _dot(s, w_ref[...].astype(BF16)) + b_ref[...]


def _modulation(cond, w_mod, b_mod):
    tn = 1536
    n_out = 6 * D_MODEL
    out = pl.pallas_call(
        _mod_kernel,
        out_shape=jax.ShapeDtypeStruct((DEPTH, N_COND_ROWS, n_out), F32),
        grid=(DEPTH, n_out // tn),
        in_specs=[
            pl.BlockSpec((N_COND_ROWS, D_MODEL), lambda l, j: (0, 0)),
            pl.BlockSpec((None, D_MODEL, tn), lambda l, j: (l, 0, j)),
            pl.BlockSpec((None, 1, tn), lambda l, j: (l, 0, j)),
        ],
        out_specs=pl.BlockSpec((None, N_COND_ROWS, tn), lambda l, j: (l, 0, j)),
        compiler_params=_params(2),
        name="modulation",
    )(cond, w_mod, b_mod.reshape(DEPTH, 1, n_out))
    return out.reshape(DEPTH, N_COND_ROWS, 6, D_MODEL)


CONV_TM = 2048
CONV_TN = 256


def _conv_in_kernel(*refs, n_x):
    x_refs = refs[:n_x]
    mod_ref, g_ref, wb_ref, wc_ref, wx_ref, cw_ref, o_ref, h_ref = refs[n_x:]
    i = pl.program_id(0)

    @pl.when(pl.program_id(1) == 0)
    def _():
        _norm_mod_rows(_row_loader(x_refs, CONV_TM), CONV_TM, g_ref, mod_ref[1:2, :],
                       mod_ref[0:1, :], h_ref)

    h = h_ref[...]
    b = _dot(h, wb_ref[...].astype(BF16))
    z = _dot(h, wc_ref[...].astype(BF16)) * _dot(h, wx_ref[...].astype(BF16))
    seq_len = jnp.where(i < N_PROMPT_ROWS // CONV_TM, SEQ, DEC_SEQ)
    pos = lax.broadcasted_iota(jnp.int32, z.shape, 0) & (seq_len - 1)
    z_prev = jnp.where(pos == 0, 0.0, pltpu.roll(z, 1, 0))
    z_next = jnp.where(pos == seq_len - 1, 0.0, pltpu.roll(z, CONV_TM - 1, 0))
    conv = z_prev * cw_ref[0:1, :] + z * cw_ref[1:2, :] + z_next * cw_ref[2:3, :]
    o_ref[...] = (b * conv).astype(o_ref.dtype)


def _conv_in(x, mod, g_mix, conv_w_in, conv_w, layer, j):
    tm, tn = CONV_TM, CONV_TN
    nb = D_MODEL // tn
    w_spec = lambda part: pl.BlockSpec((None, D_MODEL, tn),
                                       lambda i, n: (j, 0, part * nb + n))
    x_parts = _row_parts(x)
    return pl.pallas_call(
        functools.partial(_conv_in_kernel, n_x=len(x_parts)),
        out_shape=jax.ShapeDtypeStruct((N_ROWS, D_MODEL), BF16),
        grid=(N_ROWS // tm, nb),
        in_specs=_row_specs(x, tm) + [
            _mod_spec(layer, tm),
            _gain_spec(layer),
            w_spec(0), w_spec(1), w_spec(2),
            pl.BlockSpec((None, CONV_W, tn), lambda i, n: (j, 0, n)),
        ],
        out_specs=pl.BlockSpec((tm, tn), lambda i, n: (i, n)),
        scratch_shapes=[pltpu.VMEM((tm, D_MODEL), BF16)],
        compiler_params=_params(2),
        name="conv_in",
    )(*x_parts, mod, g_mix, conv_w_in, conv_w_in, conv_w_in, conv_w)


PROJ_TM = 1024


def _proj_res_kernel(*refs, n_a, n_x):
    a_refs, w_ref, x_refs = refs[:n_a], refs[n_a], refs[n_a + 1:n_a + 1 + n_x]
    mod_ref, o_ref = refs[n_a + 1 + n_x:]
    whole = slice(None)
    y = _dot(_row_loader(a_refs, PROJ_TM)(whole), w_ref[...].astype(BF16))
    o_ref[...] = _row_loader(x_refs, PROJ_TM)(whole) + mod_ref[2:3, :] * y


def _proj_res(a, w, j, x, mod, layer):
    tm = PROJ_TM
    a_parts, x_parts = _row_parts(a), _row_parts(x)
    k = a_parts[0].shape[1]
    return pl.pallas_call(
        functools.partial(_proj_res_kernel, n_a=len(a_parts), n_x=len(x_parts)),
        out_shape=jax.ShapeDtypeStruct((N_ROWS, D_MODEL), F32),
        grid=(N_ROWS // tm,),
        in_specs=(_row_specs(a, tm) + [pl.BlockSpec((None, k, D_MODEL), lambda i: (j, 0, 0))]
                  + _row_specs(x, tm) + [_mod_spec(layer, tm)]),
        out_specs=pl.BlockSpec((tm, D_MODEL), lambda i: (i, 0)),
        compiler_params=_params(1),
        name="proj_res",
    )(*a_parts, w, *x_parts, mod)


GMLP_TM = 1024
GMLP_TN = 1024
GMLP_OUT_TM = 512


def _gelu_tanh(x):
    c = 0.7978845608028654
    return 0.5 * x * (1.0 + jnp.tanh(c * (x + 0.044715 * (x * x * x))))


def _gmlp_in_kernel(x_ref, mod_ref, g_ref, w_ref, b_ref, o_ref, h_ref):
    @pl.when(pl.program_id(1) == 0)
    def _():
        _norm_mod_to(x_ref, g_ref, mod_ref[1:2, :], mod_ref[0:1, :], h_ref)

    z = _dot(h_ref[...], w_ref[...].astype(BF16)) + b_ref[...]
    o_ref[...] = _gelu_tanh(z).astype(o_ref.dtype)


def _gmlp_in(x, mod, g_mix, w_in, b_in, layer, j):
    tm, tn = GMLP_TM, GMLP_TN
    n_out = 2 * GMLP_DIM
    return pl.pallas_call(
        _gmlp_in_kernel,
        out_shape=jax.ShapeDtypeStruct((N_ROWS, n_out), BF16),
        grid=(N_ROWS // tm, n_out // tn),
        in_specs=[
            pl.BlockSpec((tm, D_MODEL), lambda i, n: (i, 0)),
            _mod_spec(layer, tm),
            _gain_spec(layer),
            pl.BlockSpec((None, D_MODEL, tn), lambda i, n: (j, 0, n)),
            pl.BlockSpec((None, 1, tn), lambda i, n: (j, 0, n)),
        ],
        out_specs=pl.BlockSpec((tm, tn), lambda i, n: (i, n)),
        scratch_shapes=[pltpu.VMEM((tm, D_MODEL), BF16)],
        compiler_params=_params(2),
        name="gmlp_in",
    )(x, mod, g_mix, w_in, b_in.reshape(-1, 1, n_out))


def _gmlp_out_kernel(u_ref, v_ref, lng_ref, lnb_ref, ws_ref, bs_ref, w_ref, x_ref, mod_ref,
                     o_ref, a_ref):
    lng = lng_ref[...]
    lnb = lnb_ref[...]

    def chunk(c, carry):
        rows = pl.ds(pl.multiple_of(c * CHUNK, CHUNK), CHUNK)
        v = v_ref[rows, :].astype(F32)
        mu = jnp.mean(v, axis=-1, keepdims=True)
        vc = v - mu
        var = jnp.mean(vc * vc, axis=-1, keepdims=True)
        vn = (vc * lax.rsqrt(var + EPS) * lng + lnb).astype(BF16)
        for g in range(GMLP_GROUPS):
            cols = slice(g * GMLP_GROUP_DIM, (g + 1) * GMLP_GROUP_DIM)
            mixed = _dot(ws_ref[g].astype(BF16), vn[:, cols]) + bs_ref[:, g:g + 1]
            a_ref[rows, cols] = (u_ref[rows, cols].astype(F32) * mixed).astype(a_ref.dtype)
        return carry

    lax.fori_loop(0, u_ref.shape[0] // CHUNK, chunk, 0)
    y = _dot(a_ref[...], w_ref[...].astype(BF16))
    o_ref[...] = x_ref[...] + mod_ref[2:3, :] * y


def _gmlp_out(z, ln_g, ln_b, w_s, b_s, w_out, x, mod, layer, j):
    tm = GMLP_OUT_TM
    return pl.pallas_call(
        _gmlp_out_kernel,
        out_shape=jax.ShapeDtypeStruct((N_ROWS, D_MODEL), F32),
        grid=(N_ROWS // tm,),
        in_specs=[
            pl.BlockSpec((tm, GMLP_DIM), lambda i: (i, 0)),
            pl.BlockSpec((tm, GMLP_DIM), lambda i: (i, 1)),
            pl.BlockSpec((None, 1, GMLP_DIM), lambda i: (j, 0, 0)),
            pl.BlockSpec((None, 1, GMLP_DIM), lambda i: (j, 0, 0)),
            pl.BlockSpec((None, GMLP_GROUPS, CHUNK, CHUNK), lambda i: (j, 0, 0, 0)),
            pl.BlockSpec((None, CHUNK, GMLP_GROUPS), lambda i: (j, 0, 0)),
            pl.BlockSpec((None, GMLP_DIM, D_MODEL), lambda i: (j, 0, 0)),
            pl.BlockSpec((tm, D_MODEL), lambda i: (i, 0)),
            _mod_spec(layer, tm),
        ],
        out_specs=pl.BlockSpec((tm, D_MODEL), lambda i: (i, 0)),
        scratch_shapes=[pltpu.VMEM((tm, GMLP_DIM), BF16)],
        compiler_params=_params(1),
        name="gmlp_out",
    )(z, z, ln_g.reshape(-1, 1, GMLP_DIM), ln_b.reshape(-1, 1, GMLP_DIM), w_s,
      jnp.swapaxes(b_s, 1, 2), w_out, x, mod)


QKV_TM = 512
ATTN_TQ = 256
ATTN_PROMPT_SEQS = 4


def _rope_tables(tm):
    pos = np.arange(DEC_SEQ)
    row = (pos // GRID_W).astype(np.float32)
    col = (pos % GRID_W).astype(np.float32)
    n_freq = HEAD_DIM // 4
    inv_freq = np.power(np.float32(ROPE_THETA), -np.arange(n_freq, dtype=np.float32) / n_freq)
    ang = np.concatenate([row[:, None] * inv_freq, col[:, None] * inv_freq], axis=-1)
    cos = np.repeat(np.cos(ang), 2, axis=-1)
    sign = np.tile(np.array([-1.0, 1.0], np.float32), HEAD_DIM // 2)
    sin = np.repeat(np.sin(ang), 2, axis=-1) * sign
    cos = np.concatenate([np.ones((tm, HEAD_DIM), np.float32), cos], axis=0)
    sin = np.concatenate([np.zeros((tm, HEAD_DIM), np.float32), sin], axis=0)
    return jnp.asarray(cos, F32), jnp.asarray(sin, F32)


def _swap_pairs(x):
    even = (lax.broadcasted_iota(jnp.int32, x.shape, 1) & 1) == 0
    return jnp.where(even, pltpu.roll(x, HEAD_DIM - 1, 1), pltpu.roll(x, 1, 1))


def _qkv_kernel(x_ref, mod_ref, g_ref, w_ref, qg_ref, kg_ref, cos_ref, sin_ref, o_ref,
                h_ref, wsw_ref):
    n_rot_heads = N_HEADS + N_KV_HEADS

    @pl.when(pl.program_id(0) == 0)
    def _():
        for hd in range(n_rot_heads):
            cols = slice(hd * HEAD_DIM, (hd + 1) * HEAD_DIM)
            wsw_ref[:, cols] = _swap_pairs(w_ref[:, cols]).astype(BF16)

    _norm_mod_to(x_ref, g_ref, mod_ref[1:2, :], mod_ref[0:1, :], h_ref)
    y = _dot(h_ref[...], w_ref[...].astype(BF16))
    y_sw = _dot(h_ref[...], wsw_ref[...])
    cos = cos_ref[...]
    sin = sin_ref[...]
    for hd in range(n_rot_heads):
        cols = slice(hd * HEAD_DIM, (hd + 1) * HEAD_DIM)
        yh = y[:, cols]
        gain = qg_ref[...] if hd < N_HEADS else kg_ref[...]
        inv = lax.rsqrt(jnp.mean(yh * yh, axis=-1, keepdims=True) + EPS)
        o_ref[:, cols] = (yh * (gain * cos) + y_sw[:, cols] * (_swap_pairs(gain) * sin)) * inv
    v_cols = slice(n_rot_heads * HEAD_DIM, QKV_OUT)
    o_ref[:, v_cols] = y[:, v_cols]


def _qkv(x, mod, g_mix, w_qkv, q_g, k_g, layer, j):
    tm = QKV_TM
    cos, sin = _rope_tables(tm)
    n_prompt = N_PROMPT_ROWS // tm
    per_seq = DEC_SEQ // tm
    tbl = lambda i: (jnp.where(i < n_prompt, 0, 1 + (i - n_prompt) % per_seq), 0)
    return pl.pallas_call(
        _qkv_kernel,
        out_shape=jax.ShapeDtypeStruct((N_ROWS, QKV_OUT), F32),
        grid=(N_ROWS // tm,),
        in_specs=[
            pl.BlockSpec((tm, D_MODEL), lambda i: (i, 0)),
            _mod_spec(layer, tm),
            _gain_spec(layer),
            pl.BlockSpec((None, D_MODEL, QKV_OUT), lambda i: (j, 0, 0)),
            pl.BlockSpec((None, 1, HEAD_DIM), lambda i: (j, 0, 0)),
            pl.BlockSpec((None, 1, HEAD_DIM), lambda i: (j, 0, 0)),
            pl.BlockSpec((tm, HEAD_DIM), tbl),
            pl.BlockSpec((tm, HEAD_DIM), tbl),
        ],
        out_specs=pl.BlockSpec((tm, QKV_OUT), lambda i: (i, 0)),
        scratch_shapes=[pltpu.VMEM((tm, D_MODEL), BF16),
                        pltpu.VMEM((D_MODEL, (N_HEADS + N_KV_HEADS) * HEAD_DIM), BF16)],
        compiler_params=_params(1),
        name="qkv",
    )(x, mod, g_mix, w_qkv, q_g.reshape(-1, 1, HEAD_DIM), k_g.reshape(-1, 1, HEAD_DIM), cos, sin)


def _attn_kernel(*refs, has_ctx, n_seq):
    if has_ctx:
        q_ref, k_ref, v_ref, ck_ref, cv_ref, o_ref = refs
    else:
        q_ref, k_ref, v_ref, o_ref = refs
    nt = (((1,), (1,)), ((), ()))
    if has_ctx:
        ck = ck_ref[...].astype(BF16)
        cv = cv_ref[...].astype(BF16)
    q_len = q_ref.shape[0] // n_seq
    k_len = k_ref.shape[0] // n_seq
    for sq in range(n_seq):
        q_rows = slice(sq * q_len, (sq + 1) * q_len)
        k_rows = slice(sq * k_len, (sq + 1) * k_len)
        k = k_ref[k_rows, :].astype(BF16)
        v = v_ref[k_rows, :].astype(BF16)
        for g in range(Q_PER_KV):
            cols = slice(g * HEAD_DIM, (g + 1) * HEAD_DIM)
            q = (q_ref[q_rows, cols] * (HEAD_DIM ** -0.5)).astype(BF16)
            s = lax.dot_general(q, k, nt, preferred_element_type=F32)
            m = jnp.max(s, axis=-1, keepdims=True)
            if has_ctx:
                sc = lax.dot_general(q, ck, nt, preferred_element_type=F32)
                m = jnp.maximum(m, jnp.max(sc, axis=-1, keepdims=True))
            p = jnp.exp(s - m)
            den = jnp.sum(p, axis=-1, keepdims=True)
            o = _dot(p.astype(BF16), v)
            if has_ctx:
                pc = jnp.exp(sc - m)
                den = den + jnp.sum(pc, axis=-1, keepdims=True)
                o = o + _dot(pc.astype(BF16), cv)
            o_ref[q_rows, cols] = (o / den).astype(o_ref.dtype)


def _attention(qkv, ctx_k, ctx_v):
    group_w = Q_PER_KV * HEAD_DIM
    k_col = N_HEADS
    v_col = N_HEADS + N_KV_HEADS

    def call(n_batch, seq, tq, row0, ctx, n_seq):
        q_blocks = seq // tq
        q_rows, kv_rows = n_seq * tq, n_seq * seq
        o_row = lambda b, qi: b * q_blocks + qi
        kv_row = lambda b: row0 // kv_rows + b
        in_specs = [
            pl.BlockSpec((q_rows, group_w), lambda b, kh, qi: (row0 // q_rows + o_row(b, qi), kh)),
            pl.BlockSpec((kv_rows, HEAD_DIM), lambda b, kh, qi: (kv_row(b), k_col + kh)),
            pl.BlockSpec((kv_rows, HEAD_DIM), lambda b, kh, qi: (kv_row(b), v_col + kh)),
        ]
        args = [qkv, qkv, qkv]
        if ctx is not None:
            ctx_spec = pl.BlockSpec((None, PAST_LEN, HEAD_DIM), lambda b, kh, qi: (b, 0, kh))
            in_specs += [ctx_spec, ctx_spec]
            args += list(ctx)
        return pl.pallas_call(
            functools.partial(_attn_kernel, has_ctx=ctx is not None, n_seq=n_seq),
            out_shape=jax.ShapeDtypeStruct((n_batch * seq, N_HEADS * HEAD_DIM), BF16),
            grid=(n_batch // n_seq, N_KV_HEADS, q_blocks),
            in_specs=in_specs,
            out_specs=pl.BlockSpec((q_rows, group_w), lambda b, kh, qi: (o_row(b, qi), kh)),
            compiler_params=_params(3),
            name="attention",
        )(*args)

    o_prompt = call(BATCH, SEQ, SEQ, 0, None, ATTN_PROMPT_SEQS)
    o_sample = call(DEC_BATCH, DEC_SEQ, ATTN_TQ, N_PROMPT_ROWS, (ctx_k, ctx_v), 1)
    return o_prompt, o_sample


FFN_TM = 1024
FFN_TF = 256
TOP_K = 2
ROUTER_TM = 1024
ROUTE_LANES = 128
MOE_TF = 512
MOE_TILE = 1024
MOE_SUB = 256
MOE_MAX_TILES = TOP_K * N_ROWS // MOE_TILE + N_EXPERTS
MOE_ROWS = MOE_MAX_TILES * MOE_TILE
SCATTER_TM = 512
COMBINE_TM = 256
WIN_CHUNK = 128
WIN_CHUNKS = COMBINE_TM // WIN_CHUNK + 1
WIN_SLOTS = N_EXPERTS * WIN_CHUNKS
WIN_SLOT_STEPS = (12, 16, WIN_SLOTS)


def _swiglu_hidden(h, wg, wu):
    gate = _dot(h, wg)
    return (gate * jax.nn.sigmoid(gate) * _dot(h, wu)).astype(BF16)


def _ffn_kernel(x_ref, mod_ref, g_ref, wg_ref, wu_ref, wd_ref, o_ref, h_ref):
    f = pl.program_id(1)

    @pl.when(f == 0)
    def _():
        _norm_mod_to(x_ref, g_ref, mod_ref[4:5, :], mod_ref[3:4, :], h_ref)
        o_ref[...] = jnp.zeros_like(o_ref)

    a = _swiglu_hidden(h_ref[...], wg_ref[...].astype(BF16), wu_ref[...].astype(BF16))
    o_ref[...] += _dot(a, wd_ref[...].astype(BF16))

    @pl.when(f == pl.num_programs(1) - 1)
    def _():
        o_ref[...] = x_ref[...] + mod_ref[5:6, :] * o_ref[...]


def _ffn(x, mod, g_ffn, w_gate, w_up, w_down, layer, li):
    tm, tf = FFN_TM, FFN_TF
    return pl.pallas_call(
        _ffn_kernel,
        out_shape=jax.ShapeDtypeStruct((N_ROWS, D_MODEL), F32),
        grid=(N_ROWS // tm, D_FF // tf),
        in_specs=[
            pl.BlockSpec((tm, D_MODEL), lambda i, f: (i, 0)),
            _mod_spec(layer, tm),
            _gain_spec(layer),
            pl.BlockSpec((None, D_MODEL, tf), lambda i, f: (li, 0, f)),
            pl.BlockSpec((None, D_MODEL, tf), lambda i, f: (li, 0, f)),
            pl.BlockSpec((None, tf, D_MODEL), lambda i, f: (li, f, 0)),
        ],
        out_specs=pl.BlockSpec((tm, D_MODEL), lambda i, f: (i, 0)),
        scratch_shapes=[pltpu.VMEM((tm, D_MODEL), BF16)],
        compiler_params=_params(2),
        name="ffn",
    )(x, mod, g_ffn, w_gate, w_up, w_down)


def _split_bf16(x):
    hi = x.astype(BF16)
    return hi, (x - hi.astype(F32)).astype(BF16)


def _router_kernel(x_ref, mod_ref, g_ref, w_ref, route_ref, route_t_ref, cnt_ref, base_ref,
                   h_ref, seen_ref):
    @pl.when(pl.program_id(0) == 0)
    def _():
        seen_ref[...] = jnp.zeros_like(seen_ref)

    _norm_mod_to(x_ref, g_ref, mod_ref[4:5, :], mod_ref[3:4, :], h_ref)
    h_hi, h_lo = _split_bf16(h_ref[...])
    w_hi, w_lo = _split_bf16(w_ref[...])
    logits = _dot(h_hi, w_hi) + (_dot(h_hi, w_lo) + _dot(h_lo, w_hi))
    lane = lax.broadcasted_iota(jnp.int32, logits.shape, 1)
    logits = jnp.where(lane < N_EXPERTS, logits, -jnp.inf)
    m1 = jnp.max(logits, axis=-1, keepdims=True)
    i1 = jnp.min(jnp.where(logits == m1, lane, N_EXPERTS), axis=-1, keepdims=True)
    rest = jnp.where(lane == i1, -jnp.inf, logits)
    m2 = jnp.max(rest, axis=-1, keepdims=True)
    i2 = jnp.min(jnp.where(rest == m2, lane, N_EXPERTS), axis=-1, keepdims=True)
    e2 = jnp.exp(m2 - m1)
    w1 = 1.0 / (1.0 + e2)
    chosen = jnp.where((lane == i1) | (lane == i2), 1.0, 0.0)
    tm = chosen.shape[0]
    blk = COMBINE_TM
    earlier = (lax.broadcasted_iota(jnp.int32, (blk, blk), 1)
               < lax.broadcasted_iota(jnp.int32, (blk, blk), 0))
    earlier = jnp.where(earlier, 1.0, 0.0).astype(BF16)
    base_ref[...] = jnp.zeros_like(base_ref)
    seen = seen_ref[...]
    priors = []
    for k in range(tm // blk):
        rows = slice(k * blk, (k + 1) * blk)
        base_ref[k:k + 1, :] = seen
        priors.append(_dot(earlier, chosen[rows].astype(BF16)) + seen)
        seen = seen + jnp.sum(chosen[rows], axis=0, keepdims=True)
    seen_ref[...] = seen
    prior = jnp.concatenate(priors, axis=0)
    rank1 = jnp.sum(jnp.where(lane == i1, prior, 0.0), axis=-1, keepdims=True)
    rank2 = jnp.sum(jnp.where(lane == i2, prior, 0.0), axis=-1, keepdims=True)
    route = jnp.zeros_like(logits)
    for k, field in enumerate((w1, e2 * w1, i1.astype(F32), i2.astype(F32), rank1, rank2)):
        route = jnp.where(lane == k, field, route)
    route_ref[...] = route
    route_t_ref[...] = jnp.transpose(route)[:SUBLANES, :]
    cnt_ref[...] = jnp.broadcast_to(seen, cnt_ref.shape)


def _router(x, mod, g_ffn, w_router, layer, li):
    tm = ROUTER_TM
    w_pad = jnp.pad(w_router, ((0, 0), (0, 0), (0, ROUTE_LANES - N_EXPERTS)))
    return pl.pallas_call(
        _router_kernel,
        out_shape=(jax.ShapeDtypeStruct((N_ROWS, ROUTE_LANES), F32),
                   jax.ShapeDtypeStruct((SUBLANES, N_ROWS), F32),
                   jax.ShapeDtypeStruct((SUBLANES, ROUTE_LANES), F32),
                   jax.ShapeDtypeStruct((N_ROWS // tm, SUBLANES, ROUTE_LANES), F32)),
        grid=(N_ROWS // tm,),
        in_specs=[
            pl.BlockSpec((tm, D_MODEL), lambda i: (i, 0)),
            _mod_spec(layer, tm),
            _gain_spec(layer),
            pl.BlockSpec((None, D_MODEL, ROUTE_LANES), lambda i: (li, 0, 0)),
        ],
        out_specs=(pl.BlockSpec((tm, ROUTE_LANES), lambda i: (i, 0)),
                   pl.BlockSpec((SUBLANES, tm), lambda i: (0, i)),
                   pl.BlockSpec((SUBLANES, ROUTE_LANES), lambda i: (0, 0)),
                   pl.BlockSpec((None, SUBLANES, ROUTE_LANES), lambda i: (i, 0, 0))),
        scratch_shapes=[pltpu.VMEM((tm, D_MODEL), F32), pltpu.VMEM((1, ROUTE_LANES), F32)],
        compiler_params=_params(1),
        name="router",
    )(x, mod, g_ffn, w_pad)


def _dispatch_plan(route_t, cnt, tile_base):
    counts = cnt[0, :N_EXPERTS].astype(jnp.int32)
    e1, e2, r1, r2 = (route_t[k].astype(jnp.int32) for k in (2, 3, 4, 5))
    n_tiles = (counts + MOE_TILE - 1) // MOE_TILE
    tile_end = jnp.cumsum(n_tiles)
    tile_start = tile_end - n_tiles
    row_start = tile_start * MOE_TILE
    pos = jnp.concatenate([row_start[e1] + r1, row_start[e2] + r2])
    n_active = tile_end[-1]
    t = jnp.arange(MOE_MAX_TILES, dtype=jnp.int32)
    tile_block = jnp.minimum(t, n_active - 1)
    tile_expert = jnp.sum(tile_block[:, None] >= tile_end[None, :], axis=1).astype(jnp.int32)
    valid = counts[tile_expert] - (tile_block - tile_start[tile_expert]) * MOE_TILE
    n_sub = (jnp.clip(valid, 0, MOE_TILE) + MOE_SUB - 1) // MOE_SUB
    tile_nsub = jnp.where(t < n_active, n_sub, 0).astype(jnp.int32)
    pad = jnp.concatenate([row_start + counts, (-counts) % MOE_SUB]).astype(jnp.int32)
    base = tile_base[:, :ROUTER_TM // COMBINE_TM, :N_EXPERTS].astype(jnp.int32)
    base = base.reshape(N_ROWS // COMBINE_TM, N_EXPERTS)
    lo = row_start[None, :] + base
    hi = row_start[None, :] + jnp.concatenate([base[1:], counts[None, :]], axis=0)
    chunk0 = lo // WIN_CHUNK
    n_chunks = jnp.where(hi > lo, (hi + WIN_CHUNK - 1) // WIN_CHUNK - chunk0, 0)
    j = jnp.arange(WIN_CHUNKS, dtype=jnp.int32)
    needed = (j[None, None, :] < n_chunks[:, :, None]).reshape(-1, WIN_SLOTS)
    chunk = (chunk0[:, :, None] + j[None, None, :]).reshape(-1, WIN_SLOTS)
    order = jnp.argsort(jnp.logical_not(needed), axis=1, stable=True)
    win_count = jnp.sum(needed, axis=1).astype(jnp.int32)
    win_list = jnp.where(jnp.arange(WIN_SLOTS)[None, :] < win_count[:, None],
                         jnp.take_along_axis(chunk, order, axis=1), -1)
    return dict(pos=pos, pad=pad, tile_expert=tile_expert, tile_nsub=tile_nsub,
                tile_block=tile_block, row_start=row_start.astype(jnp.int32),
                win_list=win_list.reshape(-1).astype(jnp.int32), win_count=win_count)


def _for_rows(n_rows, body):
    def group(j, carry):
        r0 = pl.multiple_of(j * SUBLANES, SUBLANES)
        for u in range(SUBLANES):
            body(r0 + u)
        return carry
    lax.fori_loop(0, n_rows // SUBLANES, group, 0)


def _scatter_kernel(pos_ref, pad_ref, tn_ref, x_ref, mod_ref, g_ref, xs_ref, h_ref, zero_ref,
                    sem, fill_sem):
    i = pl.program_id(0)
    ts = x_ref.shape[0]
    slot = lax.rem(i, 2)
    _norm_mod_to(x_ref, g_ref, mod_ref[4:5, :], mod_ref[3:4, :], h_ref.at[slot])

    def for_row_copies(step, step_slot, act):
        first = step * ts

        def row(r):
            for k in range(TOP_K):
                dst = pos_ref[k * N_ROWS + first + r]
                act(pltpu.make_async_copy(h_ref.at[step_slot, pl.ds(r, 1)],
                                          xs_ref.at[pl.ds(dst, 1)], sem.at[step_slot]), k)
        _for_rows(ts, row)

    for_row_copies(i, slot, lambda cp, k: cp.start(priority=k))

    @pl.when(i == 0)
    def _():
        zero_ref[...] = jnp.zeros_like(zero_ref)

        def fills(act):
            for e in range(N_EXPERTS):
                def row_body(r, carry, e=e):
                    act(pltpu.make_async_copy(zero_ref.at[pl.ds(0, 1)],
                                              xs_ref.at[pl.ds(pad_ref[e] + r, 1)], fill_sem))
                    return carry
                lax.fori_loop(0, pad_ref[N_EXPERTS + e], row_body, 0)

            def tile_body(t, carry):
                def sub_body(s, carry):
                    row = pl.multiple_of(t * MOE_TILE + s * MOE_SUB, MOE_SUB)
                    act(pltpu.make_async_copy(zero_ref, xs_ref.at[pl.ds(row, MOE_SUB)], fill_sem))
                    return carry
                return lax.fori_loop(tn_ref[t], MOE_TILE // MOE_SUB, sub_body, carry)
            lax.fori_loop(0, MOE_MAX_TILES, tile_body, 0)

        fills(lambda cp: cp.start())
        fills(lambda cp: cp.wait())

    @pl.when(i > 0)
    def _():
        for_row_copies(i - 1, 1 - slot, lambda cp, k: cp.wait())

    @pl.when(i == pl.num_programs(0) - 1)
    def _():
        for_row_copies(i, slot, lambda cp, k: cp.wait())


def _moe_scatter(x, mod, g_ffn, plan, layer):
    ts = SCATTER_TM
    grid_spec = pltpu.PrefetchScalarGridSpec(
        num_scalar_prefetch=3,
        grid=(N_ROWS // ts,),
        in_specs=[
            pl.BlockSpec((ts, D_MODEL), lambda i, *_: (i, 0)),
            _mod_spec(layer, ts),
            _gain_spec(layer),
        ],
        out_specs=pl.BlockSpec(memory_space=pl.ANY),
        scratch_shapes=[pltpu.VMEM((2, ts, D_MODEL), F32),
                        pltpu.VMEM((MOE_SUB, D_MODEL), F32),
                        pltpu.SemaphoreType.DMA((2,)), pltpu.SemaphoreType.DMA(())],
    )
    return pl.pallas_call(
        _scatter_kernel,
        out_shape=jax.ShapeDtypeStruct((MOE_ROWS, D_MODEL), F32),
        grid_spec=grid_spec,
        compiler_params=_params(1),
        name="moe_scatter",
    )(plan["pos"], plan["pad"], plan["tile_nsub"], x, mod, g_ffn)


def _moe_kernel(te_ref, tn_ref, tb_ref, x_ref, wg_ref, wu_ref, wd_ref, o_ref,
                xb_ref, acc_ref):
    n_sub = tn_ref[pl.program_id(0)]
    first_f = pl.program_id(1) == 0

    @pl.when(first_f)
    def _():
        acc_ref[...] = jnp.zeros_like(acc_ref)

    for n in range(1, MOE_TILE // MOE_SUB + 1):
        @pl.when(n_sub == n)
        def _(n=n):
            rows = slice(0, n * MOE_SUB)

            @pl.when(first_f)
            def _():
                xb_ref[rows, :] = x_ref[rows, :].astype(BF16)

            a = _swiglu_hidden(xb_ref[rows, :], wg_ref[...].astype(BF16),
                               wu_ref[...].astype(BF16))
            acc_ref[rows, :] += _dot(a, wd_ref[...].astype(BF16))

    @pl.when(pl.program_id(1) == pl.num_programs(1) - 1)
    def _():
        o_ref[...] = acc_ref[...].astype(o_ref.dtype)


def _moe_experts(xs, plan, w_gate, w_up, w_down, li):
    tile, tf = MOE_TILE, MOE_TF
    nf = D_FF_EXPERT // tf
    frozen_f = lambda i, f, tn: jnp.where(tn[i] > 0, f, nf - 1)
    grid_spec = pltpu.PrefetchScalarGridSpec(
        num_scalar_prefetch=3,
        grid=(MOE_MAX_TILES, nf),
        in_specs=[
            pl.BlockSpec((tile, D_MODEL), lambda i, f, te, tn, tb: (tb[i], 0)),
            pl.BlockSpec((None, None, D_MODEL, tf),
                         lambda i, f, te, tn, tb: (li, te[i], 0, frozen_f(i, f, tn))),
            pl.BlockSpec((None, None, D_MODEL, tf),
                         lambda i, f, te, tn, tb: (li, te[i], 0, frozen_f(i, f, tn))),
            pl.BlockSpec((None, None, tf, D_MODEL),
                         lambda i, f, te, tn, tb: (li, te[i], frozen_f(i, f, tn), 0)),
        ],
        out_specs=pl.BlockSpec((tile, D_MODEL), lambda i, f, te, tn, tb: (i, 0)),
        scratch_shapes=[pltpu.VMEM((tile, D_MODEL), BF16), pltpu.VMEM((tile, D_MODEL), F32)],
    )
    return pl.pallas_call(
        _moe_kernel,
        out_shape=jax.ShapeDtypeStruct((MOE_ROWS, D_MODEL), BF16),
        grid_spec=grid_spec,
        compiler_params=_params(2),
        name="moe_experts",
    )(plan["tile_expert"], plan["tile_nsub"], plan["tile_block"], xs, w_gate, w_up, w_down)


def _combine_kernel(start_ref, list_ref, n_ref, x_ref, mod_ref, route_ref, ys_ref, *refs):
    *o_refs, win_ref, sem = refs
    i = pl.program_id(0)
    slot = lax.rem(i, 2)

    def window_copies(tile, tile_slot, act):
        def body(s, carry):
            chunk = list_ref[tile * WIN_SLOTS + s]
            src = ys_ref.at[pl.ds(pl.multiple_of(chunk * WIN_CHUNK, WIN_CHUNK), WIN_CHUNK)]
            dst = win_ref.at[tile_slot,
                             pl.ds(pl.multiple_of(s * WIN_CHUNK, WIN_CHUNK), WIN_CHUNK)]
            act(pltpu.make_async_copy(src, dst, sem.at[tile_slot]))
            return carry
        lax.fori_loop(0, n_ref[tile], body, 0)

    @pl.when(i == 0)
    def _():
        win_ref[...] = jnp.zeros_like(win_ref)
        window_copies(i, slot, lambda cp: cp.start())

    @pl.when(i + 1 < pl.num_programs(0))
    def _():
        window_copies(i + 1, 1 - slot, lambda cp: cp.start())

    window_copies(i, slot, lambda cp: cp.wait())

    w1, w2, e1, e2, pos1, pos2 = (route_ref[:, k:k + 1] for k in range(6))
    for e in range(N_EXPERTS):
        first_row = start_ref[e].astype(F32)
        pos1 = pos1 + jnp.where(e1 == e, first_row, 0.0)
        pos2 = pos2 + jnp.where(e2 == e, first_row, 0.0)
    lane = lax.broadcasted_iota(jnp.int32, (x_ref.shape[0], WIN_CHUNK), 1).astype(F32)

    def combine(n_slots):
        blocks = []
        for s in range(n_slots):
            rows = lane + (list_ref[i * WIN_SLOTS + s] * WIN_CHUNK).astype(F32)
            weight = jnp.where(pos1 == rows, w1, 0.0) + jnp.where(pos2 == rows, w2, 0.0)
            blocks.append(weight.astype(BF16))
        y = _dot(jnp.concatenate(blocks, axis=1), win_ref[slot, :n_slots * WIN_CHUNK, :])
        out = x_ref[...] + mod_ref[5:6, :] * y
        if len(o_refs) == 1:
            o_refs[0][...] = out
        else:
            is_prompt = i < N_PROMPT_ROWS // x_ref.shape[0]

            @pl.when(is_prompt)
            def _():
                o_refs[0][...] = out

            @pl.when(jnp.logical_not(is_prompt))
            def _():
                o_refs[1][...] = out

    n_chunks = n_ref[i]
    smaller = -1
    for n_slots in WIN_SLOT_STEPS:
        @pl.when((n_chunks > smaller) & (n_chunks <= n_slots))
        def _(n_slots=n_slots):
            combine(n_slots)
        smaller = n_slots


def _moe_combine(x, mod, route, ys, plan, layer, split_out):
    tc = COMBINE_TM
    if split_out:
        half = jax.ShapeDtypeStruct((N_PROMPT_ROWS, D_MODEL), F32)
        out_shape = (half, jax.ShapeDtypeStruct((N_SAMPLE_ROWS, D_MODEL), F32))
        out_specs = tuple(_row_specs((half, half), tc))
    else:
        out_shape = jax.ShapeDtypeStruct((N_ROWS, D_MODEL), F32)
        out_specs = pl.BlockSpec((tc, D_MODEL), lambda i, *_: (i, 0))
    grid_spec = pltpu.PrefetchScalarGridSpec(
        num_scalar_prefetch=3,
        grid=(N_ROWS // tc,),
        in_specs=[
            pl.BlockSpec((tc, D_MODEL), lambda i, *_: (i, 0)),
            _mod_spec(layer, tc),
            pl.BlockSpec((tc, ROUTE_LANES), lambda i, *_: (i, 0)),
            pl.BlockSpec(memory_space=pl.ANY),
        ],
        out_specs=out_specs,
        scratch_shapes=[pltpu.VMEM((2, WIN_SLOTS * WIN_CHUNK, D_MODEL), BF16),
                        pltpu.SemaphoreType.DMA((2,))],
    )
    return pl.pallas_call(
        _combine_kernel,
        out_shape=out_shape,
        grid_spec=grid_spec,
        compiler_params=_params(1),
        name="moe_combine",
    )(plan["row_start"], plan["win_list"], plan["win_count"], x, mod, route, ys)


def kernel(x_prompt, x_sample, cache_k, cache_v, c, c_ctx, w_mod, b_mod, g_mix, g_ffn,
           conv_w_in, conv_w, conv_w_out,
           gmlp_w_in, gmlp_b_in, gmlp_ln_g, gmlp_ln_b, gmlp_w_s, gmlp_b_s, gmlp_w_out,
           attn_w_qkv, attn_q_g, attn_k_g, attn_w_o,
           ffn_w_gate, ffn_w_up, ffn_w_down,
           moe_w_router, moe_w_gate, moe_w_up, moe_w_down):
    x = (x_prompt.reshape(N_PROMPT_ROWS, D_MODEL), x_sample.reshape(N_SAMPLE_ROWS, D_MODEL))
    cond = jnp.zeros((N_COND_ROWS, D_MODEL), F32).at[0].set(c_ctx).at[1:1 + DEC_BATCH].set(c)
    mod = _modulation(cond, w_mod, b_mod)
    g_mix3 = g_mix.reshape(DEPTH, 1, D_MODEL)
    g_ffn3 = g_ffn.reshape(DEPTH, 1, D_MODEL)
    n_attn = cache_k.shape[1]
    ctx_k = cache_k.reshape(DEC_BATCH, n_attn, PAST_LEN, N_KV_HEADS * HEAD_DIM)
    ctx_v = cache_v.reshape(DEC_BATCH, n_attn, PAST_LEN, N_KV_HEADS * HEAD_DIM)

    new_k, new_v = [], []
    for layer in range(DEPTH):
        kind, j = layer % N_MIXERS, layer // N_MIXERS
        if kind == 0:
            a = _conv_in(x, mod, g_mix3, conv_w_in, conv_w, layer, j)
            x = _proj_res(a, conv_w_out, j, x, mod, layer)
        elif kind == 1:
            z = _gmlp_in(x, mod, g_mix3, gmlp_w_in, gmlp_b_in, layer, j)
            x = _gmlp_out(z, gmlp_ln_g, gmlp_ln_b, gmlp_w_s, gmlp_b_s, gmlp_w_out, x, mod,
                          layer, j)
        else:
            qkv = _qkv(x, mod, g_mix3, attn_w_qkv, attn_q_g, attn_k_g, layer, j)
            k_cols = slice(N_HEADS * HEAD_DIM, (N_HEADS + N_KV_HEADS) * HEAD_DIM)
            v_cols = slice((N_HEADS + N_KV_HEADS) * HEAD_DIM, QKV_OUT)
            new_k.append(qkv[:N_PROMPT_ROWS, k_cols].reshape(BATCH, SEQ, N_KV_HEADS, HEAD_DIM))
            new_v.append(qkv[:N_PROMPT_ROWS, v_cols].reshape(BATCH, SEQ, N_KV_HEADS, HEAD_DIM))
            a = _attention(qkv, ctx_k[:, j], ctx_v[:, j])
            x = _proj_res(a, attn_w_o, j, x, mod, layer)
        li = layer // 2
        if layer % 2 == 0:
            x = _ffn(x, mod, g_ffn3, ffn_w_gate, ffn_w_up, ffn_w_down, layer, li)
        else:
            route, route_t, cnt, tile_base = _router(x, mod, g_ffn3, moe_w_router, layer, li)
            plan = _dispatch_plan(route_t, cnt, tile_base)
            xs = _moe_scatter(x, mod, g_ffn3, plan, layer)
            ys = _moe_experts(xs, plan, moe_w_gate, moe_w_up, moe_w_down, li)
            x = _moe_combine(x, mod, route, ys, plan, layer, split_out=layer == DEPTH - 1)

    if DEPTH % 2 == 0:
        y_prompt, y_sample = x
    else:
        y_prompt, y_sample = x[:N_PROMPT_ROWS], x[N_PROMPT_ROWS:]
    return (y_prompt.reshape(BATCH, SEQ, D_MODEL), y_sample.reshape(DEC_BATCH, DEC_SEQ, D_MODEL),
            jnp.stack(new_k, axis=1), jnp.stack(new_v, axis=1))
```

```python
import functools

import jax
import jax.numpy as jnp
import numpy as np
from jax import lax
from jax.experimental import pallas as pl
from jax.experimental.pallas import tpu as pltpu

F32 = jnp.float32
BF16 = jnp.bfloat16

D_MODEL = 1024
BATCH = 16
SEQ = 256
DEPTH = 4
DEC_BATCH = 2
DEC_SEQ = 2048
PAST_LEN = 256
GRID_W = 64
N_MIXERS = 3
N_HEADS = 8
N_KV_HEADS = 2
HEAD_DIM = 128
Q_PER_KV = N_HEADS // N_KV_HEADS
ROPE_THETA = 10000.0
CONV_W = 3
CHUNK = 128
GMLP_DIM = 2 * D_MODEL
GMLP_GROUPS = 8
GMLP_GROUP_DIM = GMLP_DIM // GMLP_GROUPS
D_FF = 2816
N_EXPERTS = 8
D_FF_EXPERT = 3584
EPS = 1e-6
QKV_OUT = (N_HEADS + 2 * N_KV_HEADS) * HEAD_DIM

N_PROMPT_ROWS = BATCH * SEQ
N_SAMPLE_ROWS = DEC_BATCH * DEC_SEQ
N_ROWS = N_PROMPT_ROWS + N_SAMPLE_ROWS
SUBLANES = 8
N_COND_ROWS = SUBLANES

V7X_VMEM_BYTES = 64 * 1024 * 1024
VMEM_LIMIT_BYTES = V7X_VMEM_BYTES - 8 * 1024 * 1024
NORM_ROWS = 256


def _params(n_axes):
    return pltpu.CompilerParams(
        dimension_semantics=("arbitrary",) * n_axes,
        vmem_limit_bytes=VMEM_LIMIT_BYTES)


def _dot(a, b):
    return jnp.dot(a, b, preferred_element_type=F32)


def _cond_row(i, tm):
    n_prompt = N_PROMPT_ROWS // tm
    per_seq = DEC_SEQ // tm
    return jnp.where(i < n_prompt, 0, 1 + (i - n_prompt) // per_seq)


def _mod_spec(layer, tm):
    return pl.BlockSpec((None, None, 6, D_MODEL),
                        lambda i, *_: (layer, _cond_row(i, tm), 0, 0))


def _gain_spec(layer):
    return pl.BlockSpec((None, 1, D_MODEL), lambda *_: (layer, 0, 0))


def _row_parts(x):
    return tuple(x) if isinstance(x, (tuple, list)) else (x,)


def _row_specs(x, tm):
    parts = _row_parts(x)
    block = (tm, parts[0].shape[1])
    if len(parts) == 1:
        return [pl.BlockSpec(block, lambda i, *_: (i, 0))]
    n_prompt = N_PROMPT_ROWS // tm
    return [pl.BlockSpec(block, lambda i, *_: (jnp.minimum(i, n_prompt - 1), 0)),
            pl.BlockSpec(block, lambda i, *_: (jnp.maximum(i - n_prompt, 0), 0))]


def _row_loader(refs, tm):
    if len(refs) == 1:
        return lambda rows: refs[0][rows, :]
    is_prompt = pl.program_id(0) < N_PROMPT_ROWS // tm
    return lambda rows: jnp.where(is_prompt, refs[0][rows, :], refs[1][rows, :])


def _norm_mod_rows(load, n_rows, g_ref, scale, shift, h_ref):
    gs = g_ref[...] * (1.0 + scale)

    def body(r, carry):
        rows = pl.ds(pl.multiple_of(r * NORM_ROWS, NORM_ROWS), NORM_ROWS)
        x = load(rows)
        ms = jnp.mean(x * x, axis=-1, keepdims=True)
        h_ref[rows, :] = (x * lax.rsqrt(ms + EPS) * gs + shift).astype(h_ref.dtype)
        return carry

    lax.fori_loop(0, n_rows // NORM_ROWS, body, 0)


def _norm_mod_to(x_ref, g_ref, scale, shift, h_ref):
    _norm_mod_rows(lambda rows: x_ref[rows, :], x_ref.shape[0], g_ref, scale, shift, h_ref)


def _mod_kernel(c_ref, w_ref, b_ref, o_ref):
    cnd = c_ref[...]
    s = (cnd * jax.nn.sigmoid(cnd)).astype(BF16)
    o_ref[...] = _dot(s, w_ref[...].astype(BF16)) + b_ref[...]


def _modulation(cond, w_mod, b_mod):
    tn = 1536
    n_out = 6 * D_MODEL
    out = pl.pallas_call(
        _mod_kernel,
        out_shape=jax.ShapeDtypeStruct((DEPTH, N_COND_ROWS, n_out), F32),
        grid=(DEPTH, n_out // tn),
        in_specs=[
            pl.BlockSpec((N_COND_ROWS, D_MODEL), lambda l, j: (0, 0)),
            pl.BlockSpec((None, D_MODEL, tn), lambda l, j: (l, 0, j)),
            pl.BlockSpec((None, 1, tn), lambda l, j: (l, 0, j)),
        ],
        out_specs=pl.BlockSpec((None, N_COND_ROWS, tn), lambda l, j: (l, 0, j)),
        compiler_params=_params(2),
        name="modulation",
    )(cond, w_mod, b_mod.reshape(DEPTH, 1, n_out))
    return out.reshape(DEPTH, N_COND_ROWS, 6, D_MODEL)


CONV_TM = 2048
CONV_TN = 256


def _conv_in_kernel(*refs, n_x):
    x_refs = refs[:n_x]
    mod_ref, g_ref, wb_ref, wc_ref, wx_ref, cw_ref, o_ref, h_ref = refs[n_x:]
    i = pl.program_id(0)

    @pl.when(pl.program_id(1) == 0)
    def _():
        _norm_mod_rows(_row_loader(x_refs, CONV_TM), CONV_TM, g_ref, mod_ref[1:2, :],
                       mod_ref[0:1, :], h_ref)

    h = h_ref[...]
    b = _dot(h, wb_ref[...].astype(BF16))
    z = _dot(h, wc_ref[...].astype(BF16)) * _dot(h, wx_ref[...].astype(BF16))
    seq_len = jnp.where(i < N_PROMPT_ROWS // CONV_TM, SEQ, DEC_SEQ)
    pos = lax.broadcasted_iota(jnp.int32, z.shape, 0) & (seq_len - 1)
    z_prev = jnp.where(pos == 0, 0.0, pltpu.roll(z, 1, 0))
    z_next = jnp.where(pos == seq_len - 1, 0.0, pltpu.roll(z, CONV_TM - 1, 0))
    conv = z_prev * cw_ref[0:1, :] + z * cw_ref[1:2, :] + z_next * cw_ref[2:3, :]
    o_ref[...] = (b * conv).astype(o_ref.dtype)


def _conv_in(x, mod, g_mix, conv_w_in, conv_w, layer, j):
    tm, tn = CONV_TM, CONV_TN
    nb = D_MODEL // tn
    w_spec = lambda part: pl.BlockSpec((None, D_MODEL, tn),
                                       lambda i, n: (j, 0, part * nb + n))
    x_parts = _row_parts(x)
    return pl.pallas_call(
        functools.partial(_conv_in_kernel, n_x=len(x_parts)),
        out_shape=jax.ShapeDtypeStruct((N_ROWS, D_MODEL), BF16),
        grid=(N_ROWS // tm, nb),
        in_specs=_row_specs(x, tm) + [
            _mod_spec(layer, tm),
            _gain_spec(layer),
            w_spec(0), w_spec(1), w_spec(2),
            pl.BlockSpec((None, CONV_W, tn), lambda i, n: (j, 0, n)),
        ],
        out_specs=pl.BlockSpec((tm, tn), lambda i, n: (i, n)),
        scratch_shapes=[pltpu.VMEM((tm, D_MODEL), BF16)],
        compiler_params=_params(2),
        name="conv_in",
    )(*x_parts, mod, g_mix, conv_w_in, conv_w_in, conv_w_in, conv_w)


PROJ_TM = 1024


def _proj_res_kernel(*refs, n_a, n_x):
    a_refs, w_ref, x_refs = refs[:n_a], refs[n_a], refs[n_a + 1:n_a + 1 + n_x]
    mod_ref, o_ref = refs[n_a + 1 + n_x:]
    whole = slice(None)
    y = _dot(_row_loader(a_refs, PROJ_TM)(whole), w_ref[...].astype(BF16))
    o_ref[...] = _row_loader(x_refs, PROJ_TM)(whole) + mod_ref[2:3, :] * y


def _proj_res(a, w, j, x, mod, layer):
    tm = PROJ_TM
    a_parts, x_parts = _row_parts(a), _row_parts(x)
    k = a_parts[0].shape[1]
    return pl.pallas_call(
        functools.partial(_proj_res_kernel, n_a=len(a_parts), n_x=len(x_parts)),
        out_shape=jax.ShapeDtypeStruct((N_ROWS, D_MODEL), F32),
        grid=(N_ROWS // tm,),
        in_specs=(_row_specs(a, tm) + [pl.BlockSpec((None, k, D_MODEL), lambda i: (j, 0, 0))]
                  + _row_specs(x, tm) + [_mod_spec(layer, tm)]),
        out_specs=pl.BlockSpec((tm, D_MODEL), lambda i: (i, 0)),
        compiler_params=_params(1),
        name="proj_res",
    )(*a_parts, w, *x_parts, mod)


GMLP_TM = 1024
GMLP_TN = 1024
GMLP_OUT_TM = 512


def _gelu_tanh(x):
    c = 0.7978845608028654
    return 0.5 * x * (1.0 + jnp.tanh(c * (x + 0.044715 * (x * x * x))))


def _gmlp_in_kernel(x_ref, mod_ref, g_ref, w_ref, b_ref, o_ref, h_ref):
    @pl.when(pl.program_id(1) == 0)
    def _():
        _norm_mod_to(x_ref, g_ref, mod_ref[1:2, :], mod_ref[0:1, :], h_ref)

    z = _dot(h_ref[...], w_ref[...].astype(BF16)) + b_ref[...]
    o_ref[...] = _gelu_tanh(z).astype(o_ref.dtype)


def _gmlp_in(x, mod, g_mix, w_in, b_in, layer, j):
    tm, tn = GMLP_TM, GMLP_TN
    n_out = 2 * GMLP_DIM
    return pl.pallas_call(
        _gmlp_in_kernel,
        out_shape=jax.ShapeDtypeStruct((N_ROWS, n_out), BF16),
        grid=(N_ROWS // tm, n_out // tn),
        in_specs=[
            pl.BlockSpec((tm, D_MODEL), lambda i, n: (i, 0)),
            _mod_spec(layer, tm),
            _gain_spec(layer),
            pl.BlockSpec((None, D_MODEL, tn), lambda i, n: (j, 0, n)),
            pl.BlockSpec((None, 1, tn), lambda i, n: (j, 0, n)),
        ],
        out_specs=pl.BlockSpec((tm, tn), lambda i, n: (i, n)),
        scratch_shapes=[pltpu.VMEM((tm, D_MODEL), BF16)],
        compiler_params=_params(2),
        name="gmlp_in",
    )(x, mod, g_mix, w_in, b_in.reshape(-1, 1, n_out))


def _gmlp_out_kernel(u_ref, v_ref, lng_ref, lnb_ref, ws_ref, bs_ref, w_ref, x_ref, mod_ref,
                     o_ref, a_ref):
    lng = lng_ref[...]
    lnb = lnb_ref[...]

    def chunk(c, carry):
        rows = pl.ds(pl.multiple_of(c * CHUNK, CHUNK), CHUNK)
        v = v_ref[rows, :].astype(F32)
        mu = jnp.mean(v, axis=-1, keepdims=True)
        vc = v - mu
        var = jnp.mean(vc * vc, axis=-1, keepdims=True)
        vn = (vc * lax.rsqrt(var + EPS) * lng + lnb).astype(BF16)
        for g in range(GMLP_GROUPS):
            cols = slice(g * GMLP_GROUP_DIM, (g + 1) * GMLP_GROUP_DIM)
            mixed = _dot(ws_ref[g].astype(BF16), vn[:, cols]) + bs_ref[:, g:g + 1]
            a_ref[rows, cols] = (u_ref[rows, cols].astype(F32) * mixed).astype(a_ref.dtype)
        return carry

    lax.fori_loop(0, u_ref.shape[0] // CHUNK, chunk, 0)
    y = _dot(a_ref[...], w_ref[...].astype(BF16))
    o_ref[...] = x_ref[...] + mod_ref[2:3, :] * y


def _gmlp_out(z, ln_g, ln_b, w_s, b_s, w_out, x, mod, layer, j):
    tm = GMLP_OUT_TM
    return pl.pallas_call(
        _gmlp_out_kernel,
        out_shape=jax.ShapeDtypeStruct((N_ROWS, D_MODEL), F32),
        grid=(N_ROWS // tm,),
        in_specs=[
            pl.BlockSpec((tm, GMLP_DIM), lambda i: (i, 0)),
            pl.BlockSpec((tm, GMLP_DIM), lambda i: (i, 1)),
            pl.BlockSpec((None, 1, GMLP_DIM), lambda i: (j, 0, 0)),
            pl.BlockSpec((None, 1, GMLP_DIM), lambda i: (j, 0, 0)),
            pl.BlockSpec((None, GMLP_GROUPS, CHUNK, CHUNK), lambda i: (j, 0, 0, 0)),
            pl.BlockSpec((None, CHUNK, GMLP_GROUPS), lambda i: (j, 0, 0)),
            pl.BlockSpec((None, GMLP_DIM, D_MODEL), lambda i: (j, 0, 0)),
            pl.BlockSpec((tm, D_MODEL), lambda i: (i, 0)),
            _mod_spec(layer, tm),
        ],
        out_specs=pl.BlockSpec((tm, D_MODEL), lambda i: (i, 0)),
        scratch_shapes=[pltpu.VMEM((tm, GMLP_DIM), BF16)],
        compiler_params=_params(1),
        name="gmlp_out",
    )(z, z, ln_g.reshape(-1, 1, GMLP_DIM), ln_b.reshape(-1, 1, GMLP_DIM), w_s,
      jnp.swapaxes(b_s, 1, 2), w_out, x, mod)


QKV_TM = 512
ATTN_TQ = 512
ATTN_PROMPT_SEQS = 4


def _rope_tables(tm):
    pos = np.arange(DEC_SEQ)
    row = (pos // GRID_W).astype(np.float32)
    col = (pos % GRID_W).astype(np.float32)
    n_freq = HEAD_DIM // 4
    inv_freq = np.power(np.float32(ROPE_THETA), -np.arange(n_freq, dtype=np.float32) / n_freq)
    ang = np.concatenate([row[:, None] * inv_freq, col[:, None] * inv_freq], axis=-1)
    cos = np.repeat(np.cos(ang), 2, axis=-1)
    sign = np.tile(np.array([-1.0, 1.0], np.float32), HEAD_DIM // 2)
    sin = np.repeat(np.sin(ang), 2, axis=-1) * sign
    cos = np.concatenate([np.ones((tm, HEAD_DIM), np.float32), cos], axis=0)
    sin = np.concatenate([np.zeros((tm, HEAD_DIM), np.float32), sin], axis=0)
    return jnp.asarray(cos, F32), jnp.asarray(sin, F32)


def _swap_pairs(x):
    even = (lax.broadcasted_iota(jnp.int32, x.shape, 1) & 1) == 0
    return jnp.where(even, pltpu.roll(x, HEAD_DIM - 1, 1), pltpu.roll(x, 1, 1))


def _qkv_kernel(x_ref, mod_ref, g_ref, w_ref, qg_ref, kg_ref, cos_ref, sin_ref, o_ref,
                h_ref, wsw_ref):
    n_rot_heads = N_HEADS + N_KV_HEADS

    @pl.when(pl.program_id(0) == 0)
    def _():
        for hd in range(n_rot_heads):
            cols = slice(hd * HEAD_DIM, (hd + 1) * HEAD_DIM)
            wsw_ref[:, cols] = _swap_pairs(w_ref[:, cols]).astype(BF16)

    _norm_mod_to(x_ref, g_ref, mod_ref[1:2, :], mod_ref[0:1, :], h_ref)
    y = _dot(h_ref[...], w_ref[...].astype(BF16))
    y_sw = _dot(h_ref[...], wsw_ref[...])
    cos = cos_ref[...]
    sin = sin_ref[...]
    for hd in range(n_rot_heads):
        cols = slice(hd * HEAD_DIM, (hd + 1) * HEAD_DIM)
        yh = y[:, cols]
        gain = qg_ref[...] if hd < N_HEADS else kg_ref[...]
        inv = lax.rsqrt(jnp.mean(yh * yh, axis=-1, keepdims=True) + EPS)
        o_ref[:, cols] = (yh * (gain * cos) + y_sw[:, cols] * (_swap_pairs(gain) * sin)) * inv
    v_cols = slice(n_rot_heads * HEAD_DIM, QKV_OUT)
    o_ref[:, v_cols] = y[:, v_cols]


def _qkv(x, mod, g_mix, w_qkv, q_g, k_g, layer, j):
    tm = QKV_TM
    cos, sin = _rope_tables(tm)
    n_prompt = N_PROMPT_ROWS // tm
    per_seq = DEC_SEQ // tm
    tbl = lambda i: (jnp.where(i < n_prompt, 0, 1 + (i - n_prompt) % per_seq), 0)
    return pl.pallas_call(
        _qkv_kernel,
        out_shape=jax.ShapeDtypeStruct((N_ROWS, QKV_OUT), F32),
        grid=(N_ROWS // tm,),
        in_specs=[
            pl.BlockSpec((tm, D_MODEL), lambda i: (i, 0)),
            _mod_spec(layer, tm),
            _gain_spec(layer),
            pl.BlockSpec((None, D_MODEL, QKV_OUT), lambda i: (j, 0, 0)),
            pl.BlockSpec((None, 1, HEAD_DIM), lambda i: (j, 0, 0)),
            pl.BlockSpec((None, 1, HEAD_DIM), lambda i: (j, 0, 0)),
            pl.BlockSpec((tm, HEAD_DIM), tbl),
            pl.BlockSpec((tm, HEAD_DIM), tbl),
        ],
        out_specs=pl.BlockSpec((tm, QKV_OUT), lambda i: (i, 0)),
        scratch_shapes=[pltpu.VMEM((tm, D_MODEL), BF16),
                        pltpu.VMEM((D_MODEL, (N_HEADS + N_KV_HEADS) * HEAD_DIM), BF16)],
        compiler_params=_params(1),
        name="qkv",
    )(x, mod, g_mix, w_qkv, q_g.reshape(-1, 1, HEAD_DIM), k_g.reshape(-1, 1, HEAD_DIM), cos, sin)


def _attn_kernel(*refs, has_ctx, n_seq):
    if has_ctx:
        q_ref, k_ref, v_ref, ck_ref, cv_ref, o_ref = refs
    else:
        q_ref, k_ref, v_ref, o_ref = refs
    nt = (((1,), (1,)), ((), ()))
    if has_ctx:
        ck = ck_ref[...].astype(BF16)
        cv = cv_ref[...].astype(BF16)
    q_len = q_ref.shape[0] // n_seq
    k_len = k_ref.shape[0] // n_seq
    for sq in range(n_seq):
        q_rows = slice(sq * q_len, (sq + 1) * q_len)
        k_rows = slice(sq * k_len, (sq + 1) * k_len)
        k = k_ref[k_rows, :].astype(BF16)
        v = v_ref[k_rows, :].astype(BF16)
        for g in range(Q_PER_KV):
            cols = slice(g * HEAD_DIM, (g + 1) * HEAD_DIM)
            q = (q_ref[q_rows, cols] * (HEAD_DIM ** -0.5)).astype(BF16)
            s = lax.dot_general(q, k, nt, preferred_element_type=F32)
            m = jnp.max(s, axis=-1, keepdims=True)
            if has_ctx:
                sc = lax.dot_general(q, ck, nt, preferred_element_type=F32)
                m = jnp.maximum(m, jnp.max(sc, axis=-1, keepdims=True))
            p = jnp.exp(s - m)
            den = jnp.sum(p, axis=-1, keepdims=True)
            o = _dot(p.astype(BF16), v)
            if has_ctx:
                pc = jnp.exp(sc - m)
                den = den + jnp.sum(pc, axis=-1, keepdims=True)
                o = o + _dot(pc.astype(BF16), cv)
            o_ref[q_rows, cols] = (o / den).astype(o_ref.dtype)


def _attention(qkv, ctx_k, ctx_v):
    group_w = Q_PER_KV * HEAD_DIM
    k_col = N_HEADS
    v_col = N_HEADS + N_KV_HEADS

    def call(n_batch, seq, tq, row0, ctx, n_seq):
        q_blocks = seq // tq
        q_rows, kv_rows = n_seq * tq, n_seq * seq
        o_row = lambda b, qi: b * q_blocks + qi
        kv_row = lambda b: row0 // kv_rows + b
        in_specs = [
            pl.BlockSpec((q_rows, group_w), lambda b, kh, qi: (row0 // q_rows + o_row(b, qi), kh)),
            pl.BlockSpec((kv_rows, HEAD_DIM), lambda b, kh, qi: (kv_row(b), k_col + kh)),
            pl.BlockSpec((kv_rows, HEAD_DIM), lambda b, kh, qi: (kv_row(b), v_col + kh)),
        ]
        args = [qkv, qkv, qkv]
        if ctx is not None:
            ctx_spec = pl.BlockSpec((None, PAST_LEN, HEAD_DIM), lambda b, kh, qi: (b, 0, kh))
            in_specs += [ctx_spec, ctx_spec]
            args += list(ctx)
        return pl.pallas_call(
            functools.partial(_attn_kernel, has_ctx=ctx is not None, n_seq=n_seq),
            out_shape=jax.ShapeDtypeStruct((n_batch * seq, N_HEADS * HEAD_DIM), BF16),
            grid=(n_batch // n_seq, N_KV_HEADS, q_blocks),
            in_specs=in_specs,
            out_specs=pl.BlockSpec((q_rows, group_w), lambda b, kh, qi: (o_row(b, qi), kh)),
            compiler_params=_params(3),
            name="attention",
        )(*args)

    o_prompt = call(BATCH, SEQ, SEQ, 0, None, ATTN_PROMPT_SEQS)
    o_sample = call(DEC_BATCH, DEC_SEQ, ATTN_TQ, N_PROMPT_ROWS, (ctx_k, ctx_v), 1)
    return o_prompt, o_sample


FFN_TM = 1024
FFN_TF = 256
TOP_K = 2
ROUTER_TM = 1024
ROUTE_LANES = 128
MOE_TF = 512
MOE_TILE = 1024
MOE_SUB = 128
MOE_MAX_TILES = TOP_K * N_ROWS // MOE_TILE + N_EXPERTS
MOE_ROWS = MOE_MAX_TILES * MOE_TILE
SCATTER_TM = 512
COMBINE_TM = 256
WIN_CHUNK = 128
WIN_CHUNKS = COMBINE_TM // WIN_CHUNK + 1
WIN_SLOTS = TOP_K * COMBINE_TM // WIN_CHUNK + 2 * N_EXPERTS
WIN_SLOT_STEPS = (12, 16, WIN_SLOTS)


def _swiglu_hidden(h, wg, wu):
    gate = _dot(h, wg)
    return (gate * jax.nn.sigmoid(gate) * _dot(h, wu)).astype(BF16)


def _ffn_kernel(x_ref, mod_ref, g_ref, wg_ref, wu_ref, wd_ref, o_ref, h_ref):
    f = pl.program_id(1)

    @pl.when(f == 0)
    def _():
        _norm_mod_to(x_ref, g_ref, mod_ref[4:5, :], mod_ref[3:4, :], h_ref)
        o_ref[...] = jnp.zeros_like(o_ref)

    a = _swiglu_hidden(h_ref[...], wg_ref[...].astype(BF16), wu_ref[...].astype(BF16))
    o_ref[...] += _dot(a, wd_ref[...].astype(BF16))

    @pl.when(f == pl.num_programs(1) - 1)
    def _():
        o_ref[...] = x_ref[...] + mod_ref[5:6, :] * o_ref[...]


def _ffn(x, mod, g_ffn, w_gate, w_up, w_down, layer, li):
    tm, tf = FFN_TM, FFN_TF
    return pl.pallas_call(
        _ffn_kernel,
        out_shape=jax.ShapeDtypeStruct((N_ROWS, D_MODEL), F32),
        grid=(N_ROWS // tm, D_FF // tf),
        in_specs=[
            pl.BlockSpec((tm, D_MODEL), lambda i, f: (i, 0)),
            _mod_spec(layer, tm),
            _gain_spec(layer),
            pl.BlockSpec((None, D_MODEL, tf), lambda i, f: (li, 0, f)),
            pl.BlockSpec((None, D_MODEL, tf), lambda i, f: (li, 0, f)),
            pl.BlockSpec((None, tf, D_MODEL), lambda i, f: (li, f, 0)),
        ],
        out_specs=pl.BlockSpec((tm, D_MODEL), lambda i, f: (i, 0)),
        scratch_shapes=[pltpu.VMEM((tm, D_MODEL), BF16)],
        compiler_params=_params(2),
        name="ffn",
    )(x, mod, g_ffn, w_gate, w_up, w_down)


def _split_bf16(x):
    hi = x.astype(BF16)
    return hi, (x - hi.astype(F32)).astype(BF16)


def _router_kernel(x_ref, mod_ref, g_ref, w_ref, route_ref, route_t_ref, cnt_ref, base_ref,
                   h_ref, seen_ref):
    @pl.when(pl.program_id(0) == 0)
    def _():
        seen_ref[...] = jnp.zeros_like(seen_ref)

    _norm_mod_to(x_ref, g_ref, mod_ref[4:5, :], mod_ref[3:4, :], h_ref)
    h_hi, h_lo = _split_bf16(h_ref[...])
    w_hi, w_lo = _split_bf16(w_ref[...])
    logits = _dot(h_hi, w_hi) + (_dot(h_hi, w_lo) + _dot(h_lo, w_hi))
    lane = lax.broadcasted_iota(jnp.int32, logits.shape, 1)
    logits = jnp.where(lane < N_EXPERTS, logits, -jnp.inf)
    m1 = jnp.max(logits, axis=-1, keepdims=True)
    i1 = jnp.min(jnp.where(logits == m1, lane, N_EXPERTS), axis=-1, keepdims=True)
    rest = jnp.where(lane == i1, -jnp.inf, logits)
    m2 = jnp.max(rest, axis=-1, keepdims=True)
    i2 = jnp.min(jnp.where(rest == m2, lane, N_EXPERTS), axis=-1, keepdims=True)
    e2 = jnp.exp(m2 - m1)
    w1 = 1.0 / (1.0 + e2)
    chosen = jnp.where((lane == i1) | (lane == i2), 1.0, 0.0)
    tm = chosen.shape[0]
    blk = COMBINE_TM
    earlier = (lax.broadcasted_iota(jnp.int32, (blk, blk), 1)
               < lax.broadcasted_iota(jnp.int32, (blk, blk), 0))
    earlier = jnp.where(earlier, 1.0, 0.0).astype(BF16)
    base_ref[...] = jnp.zeros_like(base_ref)
    seen = seen_ref[...]
    priors = []
    for k in range(tm // blk):
        rows = slice(k * blk, (k + 1) * blk)
        base_ref[k:k + 1, :] = seen
        priors.append(_dot(earlier, chosen[rows].astype(BF16)) + seen)
        seen = seen + jnp.sum(chosen[rows], axis=0, keepdims=True)
    seen_ref[...] = seen
    prior = jnp.concatenate(priors, axis=0)
    rank1 = jnp.sum(jnp.where(lane == i1, prior, 0.0), axis=-1, keepdims=True)
    rank2 = jnp.sum(jnp.where(lane == i2, prior, 0.0), axis=-1, keepdims=True)
    route = jnp.zeros_like(logits)
    for k, field in enumerate((w1, e2 * w1, i1.astype(F32), i2.astype(F32), rank1, rank2)):
        route = jnp.where(lane == k, field, route)
    route_ref[...] = route
    route_t_ref[...] = jnp.transpose(route)[:SUBLANES, :]
    cnt_ref[...] = jnp.broadcast_to(seen, cnt_ref.shape)


def _router(x, mod, g_ffn, w_router, layer, li):
    tm = ROUTER_TM
    w_pad = jnp.pad(w_router, ((0, 0), (0, 0), (0, ROUTE_LANES - N_EXPERTS)))
    return pl.pallas_call(
        _router_kernel,
        out_shape=(jax.ShapeDtypeStruct((N_ROWS, ROUTE_LANES), F32),
                   jax.ShapeDtypeStruct((SUBLANES, N_ROWS), F32),
                   jax.ShapeDtypeStruct((SUBLANES, ROUTE_LANES), F32),
                   jax.ShapeDtypeStruct((N_ROWS // tm, SUBLANES, ROUTE_LANES), F32)),
        grid=(N_ROWS // tm,),
        in_specs=[
            pl.BlockSpec((tm, D_MODEL), lambda i: (i, 0)),
            _mod_spec(layer, tm),
            _gain_spec(layer),
            pl.BlockSpec((None, D_MODEL, ROUTE_LANES), lambda i: (li, 0, 0)),
        ],
        out_specs=(pl.BlockSpec((tm, ROUTE_LANES), lambda i: (i, 0)),
                   pl.BlockSpec((SUBLANES, tm), lambda i: (0, i)),
                   pl.BlockSpec((SUBLANES, ROUTE_LANES), lambda i: (0, 0)),
                   pl.BlockSpec((None, SUBLANES, ROUTE_LANES), lambda i: (i, 0, 0))),
        scratch_shapes=[pltpu.VMEM((tm, D_MODEL), F32), pltpu.VMEM((1, ROUTE_LANES), F32)],
        compiler_params=_params(1),
        name="router",
    )(x, mod, g_ffn, w_pad)


def _dispatch_plan(route_t, cnt, tile_base):
    counts = cnt[0, :N_EXPERTS].astype(jnp.int32)
    e1, e2, r1, r2 = (route_t[k].astype(jnp.int32) for k in (2, 3, 4, 5))
    n_tiles = (counts + MOE_TILE - 1) // MOE_TILE
    tile_end = jnp.cumsum(n_tiles)
    tile_start = tile_end - n_tiles
    row_start = tile_start * MOE_TILE
    pos = jnp.concatenate([row_start[e1] + r1, row_start[e2] + r2])
    n_active = tile_end[-1]
    t = jnp.arange(MOE_MAX_TILES, dtype=jnp.int32)
    tile_block = jnp.minimum(t, n_active - 1)
    tile_expert = jnp.sum(tile_block[:, None] >= tile_end[None, :], axis=1).astype(jnp.int32)
    valid = counts[tile_expert] - (tile_block - tile_start[tile_expert]) * MOE_TILE
    n_sub = (jnp.clip(valid, 0, MOE_TILE) + MOE_SUB - 1) // MOE_SUB
    tile_nsub = jnp.where(t < n_active, n_sub, 0).astype(jnp.int32)
    pad = jnp.concatenate([row_start + counts, (-counts) % MOE_SUB]).astype(jnp.int32)
    base = tile_base[:, :ROUTER_TM // COMBINE_TM, :N_EXPERTS].astype(jnp.int32)
    base = base.reshape(N_ROWS // COMBINE_TM, N_EXPERTS)
    lo = row_start[None, :] + base
    hi = row_start[None, :] + jnp.concatenate([base[1:], counts[None, :]], axis=0)
    chunk0 = lo // WIN_CHUNK
    n_chunks = jnp.where(hi > lo, (hi + WIN_CHUNK - 1) // WIN_CHUNK - chunk0, 0)
    j = jnp.arange(WIN_CHUNKS, dtype=jnp.int32)
    needed = (j[None, None, :] < n_chunks[:, :, None]).reshape(-1, N_EXPERTS * WIN_CHUNKS)
    chunk = (chunk0[:, :, None] + j[None, None, :]).reshape(-1, N_EXPERTS * WIN_CHUNKS)
    order = jnp.argsort(jnp.logical_not(needed), axis=1, stable=True)[:, :WIN_SLOTS]
    win_count = jnp.sum(needed, axis=1).astype(jnp.int32)
    win_list = jnp.where(jnp.arange(WIN_SLOTS)[None, :] < win_count[:, None],
                         jnp.take_along_axis(chunk, order, axis=1), -1)
    return dict(pos=pos, pad=pad, tile_expert=tile_expert, tile_nsub=tile_nsub,
                tile_block=tile_block, row_start=row_start.astype(jnp.int32),
                win_list=win_list.reshape(-1).astype(jnp.int32), win_count=win_count)


def _for_rows(n_rows, body):
    def group(j, carry):
        r0 = pl.multiple_of(j * SUBLANES, SUBLANES)
        for u in range(SUBLANES):
            body(r0 + u)
        return carry
    lax.fori_loop(0, n_rows // SUBLANES, group, 0)


def _scatter_kernel(pos_ref, pad_ref, tn_ref, x_ref, mod_ref, g_ref, xs_ref, h_ref, zero_ref,
                    sem, fill_sem):
    i = pl.program_id(0)
    ts = x_ref.shape[0]
    slot = lax.rem(i, 2)
    _norm_mod_to(x_ref, g_ref, mod_ref[4:5, :], mod_ref[3:4, :], h_ref.at[slot])

    def for_row_copies(step, step_slot, act):
        first = step * ts

        def row(r):
            for k in range(TOP_K):
                dst = pos_ref[k * N_ROWS + first + r]
                act(pltpu.make_async_copy(h_ref.at[step_slot, pl.ds(r, 1)],
                                          xs_ref.at[pl.ds(dst, 1)], sem.at[step_slot]), k)
        _for_rows(ts, row)

    for_row_copies(i, slot, lambda cp, k: cp.start(priority=k))

    @pl.when(i == 0)
    def _():
        zero_ref[...] = jnp.zeros_like(zero_ref)

        def fills(act):
            for e in range(N_EXPERTS):
                def row_body(r, carry, e=e):
                    act(pltpu.make_async_copy(zero_ref.at[pl.ds(0, 1)],
                                              xs_ref.at[pl.ds(pad_ref[e] + r, 1)], fill_sem))
                    return carry
                lax.fori_loop(0, pad_ref[N_EXPERTS + e], row_body, 0)

            def tile_body(t, carry):
                def sub_body(s, carry):
                    row = pl.multiple_of(t * MOE_TILE + s * MOE_SUB, MOE_SUB)
                    act(pltpu.make_async_copy(zero_ref, xs_ref.at[pl.ds(row, MOE_SUB)], fill_sem))
                    return carry
                return lax.fori_loop(tn_ref[t], MOE_TILE // MOE_SUB, sub_body, carry)
            lax.fori_loop(0, MOE_MAX_TILES, tile_body, 0)

        fills(lambda cp: cp.start())
        fills(lambda cp: cp.wait())

    @pl.when(i > 0)
    def _():
        for_row_copies(i - 1, 1 - slot, lambda cp, k: cp.wait())

    @pl.when(i == pl.num_programs(0) - 1)
    def _():
        for_row_copies(i, slot, lambda cp, k: cp.wait())


def _moe_scatter(x, mod, g_ffn, plan, layer):
    ts = SCATTER_TM
    grid_spec = pltpu.PrefetchScalarGridSpec(
        num_scalar_prefetch=3,
        grid=(N_ROWS // ts,),
        in_specs=[
            pl.BlockSpec((ts, D_MODEL), lambda i, *_: (i, 0)),
            _mod_spec(layer, ts),
            _gain_spec(layer),
        ],
        out_specs=pl.BlockSpec(memory_space=pl.ANY),
        scratch_shapes=[pltpu.VMEM((2, ts, D_MODEL), F32),
                        pltpu.VMEM((MOE_SUB, D_MODEL), F32),
                        pltpu.SemaphoreType.DMA((2,)), pltpu.SemaphoreType.DMA(())],
    )
    return pl.pallas_call(
        _scatter_kernel,
        out_shape=jax.ShapeDtypeStruct((MOE_ROWS, D_MODEL), F32),
        grid_spec=grid_spec,
        compiler_params=_params(1),
        name="moe_scatter",
    )(plan["pos"], plan["pad"], plan["tile_nsub"], x, mod, g_ffn)


def _moe_kernel(te_ref, tn_ref, tb_ref, x_ref, wg_ref, wu_ref, wd_ref, o_ref,
                xb_ref, acc_ref):
    n_sub = tn_ref[pl.program_id(0)]
    first_f = pl.program_id(1) == 0

    @pl.when(first_f)
    def _():
        acc_ref[...] = jnp.zeros_like(acc_ref)

    for n in range(1, MOE_TILE // MOE_SUB + 1):
        @pl.when(n_sub == n)
        def _(n=n):
            rows = slice(0, n * MOE_SUB)

            @pl.when(first_f)
            def _():
                xb_ref[rows, :] = x_ref[rows, :].astype(BF16)

            a = _swiglu_hidden(xb_ref[rows, :], wg_ref[...].astype(BF16),
                               wu_ref[...].astype(BF16))
            acc_ref[rows, :] += _dot(a, wd_ref[...].astype(BF16))

    @pl.when(pl.program_id(1) == pl.num_programs(1) - 1)
    def _():
        o_ref[...] = acc_ref[...].astype(o_ref.dtype)


def _moe_experts(xs, plan, w_gate, w_up, w_down, li):
    tile, tf = MOE_TILE, MOE_TF
    nf = D_FF_EXPERT // tf
    frozen_f = lambda i, f, tn: jnp.where(tn[i] > 0, f, nf - 1)
    grid_spec = pltpu.PrefetchScalarGridSpec(
        num_scalar_prefetch=3,
        grid=(MOE_MAX_TILES, nf),
        in_specs=[
            pl.BlockSpec((tile, D_MODEL), lambda i, f, te, tn, tb: (tb[i], 0)),
            pl.BlockSpec((None, None, D_MODEL, tf),
                         lambda i, f, te, tn, tb: (li, te[i], 0, frozen_f(i, f, tn))),
            pl.BlockSpec((None, None, D_MODEL, tf),
                         lambda i, f, te, tn, tb: (li, te[i], 0, frozen_f(i, f, tn))),
            pl.BlockSpec((None, None, tf, D_MODEL),
                         lambda i, f, te, tn, tb: (li, te[i], frozen_f(i, f, tn), 0)),
        ],
        out_specs=pl.BlockSpec((tile, D_MODEL), lambda i, f, te, tn, tb: (i, 0)),
        scratch_shapes=[pltpu.VMEM((tile, D_MODEL), BF16), pltpu.VMEM((tile, D_MODEL), F32)],
    )
    return pl.pallas_call(
        _moe_kernel,
        out_shape=jax.ShapeDtypeStruct((MOE_ROWS, D_MODEL), BF16),
        grid_spec=grid_spec,
        compiler_params=_params(2),
        name="moe_experts",
    )(plan["tile_expert"], plan["tile_nsub"], plan["tile_block"], xs, w_gate, w_up, w_down)


def _combine_kernel(start_ref, list_ref, n_ref, x_ref, mod_ref, route_ref, ys_ref, *refs):
    *o_refs, win_ref, sem = refs
    i = pl.program_id(0)
    slot = lax.rem(i, 2)

    def window_copies(tile, tile_slot, act):
        def body(s, carry):
            chunk = list_ref[tile * WIN_SLOTS + s]
            src = ys_ref.at[pl.ds(pl.multiple_of(chunk * WIN_CHUNK, WIN_CHUNK), WIN_CHUNK)]
            dst = win_ref.at[tile_slot,
                             pl.ds(pl.multiple_of(s * WIN_CHUNK, WIN_CHUNK), WIN_CHUNK)]
            act(pltpu.make_async_copy(src, dst, sem.at[tile_slot]))
            return carry
        lax.fori_loop(0, n_ref[tile], body, 0)

    @pl.when(i == 0)
    def _():
        win_ref[...] = jnp.zeros_like(win_ref)
        window_copies(i, slot, lambda cp: cp.start())

    @pl.when(i + 1 < pl.num_programs(0))
    def _():
        window_copies(i + 1, 1 - slot, lambda cp: cp.start())

    window_copies(i, slot, lambda cp: cp.wait())

    w1, w2, e1, e2, pos1, pos2 = (route_ref[:, k:k + 1] for k in range(6))
    for e in range(N_EXPERTS):
        first_row = start_ref[e].astype(F32)
        pos1 = pos1 + jnp.where(e1 == e, first_row, 0.0)
        pos2 = pos2 + jnp.where(e2 == e, first_row, 0.0)
    lane = lax.broadcasted_iota(jnp.int32, (x_ref.shape[0], WIN_CHUNK), 1).astype(F32)

    def combine(n_slots):
        blocks = []
        for s in range(n_slots):
            rows = lane + (list_ref[i * WIN_SLOTS + s] * WIN_CHUNK).astype(F32)
            weight = jnp.where(pos1 == rows, w1, 0.0) + jnp.where(pos2 == rows, w2, 0.0)
            blocks.append(weight.astype(BF16))
        y = _dot(jnp.concatenate(blocks, axis=1), win_ref[slot, :n_slots * WIN_CHUNK, :])
        out = x_ref[...] + mod_ref[5:6, :] * y
        if len(o_refs) == 1:
            o_refs[0][...] = out
        else:
            is_prompt = i < N_PROMPT_ROWS // x_ref.shape[0]

            @pl.when(is_prompt)
            def _():
                o_refs[0][...] = out

            @pl.when(jnp.logical_not(is_prompt))
            def _():
                o_refs[1][...] = out

    n_chunks = n_ref[i]
    smaller = -1
    for n_slots in WIN_SLOT_STEPS:
        @pl.when((n_chunks > smaller) & (n_chunks <= n_slots))
        def _(n_slots=n_slots):
            combine(n_slots)
        smaller = n_slots


def _moe_combine(x, mod, route, ys, plan, layer, split_out):
    tc = COMBINE_TM
    if split_out:
        half = jax.ShapeDtypeStruct((N_PROMPT_ROWS, D_MODEL), F32)
        out_shape = (half, jax.ShapeDtypeStruct((N_SAMPLE_ROWS, D_MODEL), F32))
        out_specs = tuple(_row_specs((half, half), tc))
    else:
        out_shape = jax.ShapeDtypeStruct((N_ROWS, D_MODEL), F32)
        out_specs = pl.BlockSpec((tc, D_MODEL), lambda i, *_: (i, 0))
    grid_spec = pltpu.PrefetchScalarGridSpec(
        num_scalar_prefetch=3,
        grid=(N_ROWS // tc,),
        in_specs=[
            pl.BlockSpec((tc, D_MODEL), lambda i, *_: (i, 0)),
            _mod_spec(layer, tc),
            pl.BlockSpec((tc, ROUTE_LANES), lambda i, *_: (i, 0)),
            pl.BlockSpec(memory_space=pl.ANY),
        ],
        out_specs=out_specs,
        scratch_shapes=[pltpu.VMEM((2, WIN_SLOTS * WIN_CHUNK, D_MODEL), BF16),
                        pltpu.SemaphoreType.DMA((2,))],
    )
    return pl.pallas_call(
        _combine_kernel,
        out_shape=out_shape,
        grid_spec=grid_spec,
        compiler_params=_params(1),
        name="moe_combine",
    )(plan["row_start"], plan["win_list"], plan["win_count"], x, mod, route, ys)


def kernel(x_prompt, x_sample, cache_k, cache_v, c, c_ctx, w_mod, b_mod, g_mix, g_ffn,
           conv_w_in, conv_w, conv_w_out,
           gmlp_w_in, gmlp_b_in, gmlp_ln_g, gmlp_ln_b, gmlp_w_s, gmlp_b_s, gmlp_w_out,
           attn_w_qkv, attn_q_g, attn_k_g, attn_w_o,
           ffn_w_gate, ffn_w_up, ffn_w_down,
           moe_w_router, moe_w_gate, moe_w_up, moe_w_down):
    x = (x_prompt.reshape(N_PROMPT_ROWS, D_MODEL), x_sample.reshape(N_SAMPLE_ROWS, D_MODEL))
    cond = jnp.zeros((N_COND_ROWS, D_MODEL), F32).at[0].set(c_ctx).at[1:1 + DEC_BATCH].set(c)
    mod = _modulation(cond, w_mod, b_mod)
    g_mix3 = g_mix.reshape(DEPTH, 1, D_MODEL)
    g_ffn3 = g_ffn.reshape(DEPTH, 1, D_MODEL)
    n_attn = cache_k.shape[1]
    ctx_k = cache_k.reshape(DEC_BATCH, n_attn, PAST_LEN, N_KV_HEADS * HEAD_DIM)
    ctx_v = cache_v.reshape(DEC_BATCH, n_attn, PAST_LEN, N_KV_HEADS * HEAD_DIM)

    new_k, new_v = [], []
    for layer in range(DEPTH):
        kind, j = layer % N_MIXERS, layer // N_MIXERS
        if kind == 0:
            a = _conv_in(x, mod, g_mix3, conv_w_in, conv_w, layer, j)
            x = _proj_res(a, conv_w_out, j, x, mod, layer)
        elif kind == 1:
            z = _gmlp_in(x, mod, g_mix3, gmlp_w_in, gmlp_b_in, layer, j)
            x = _gmlp_out(z, gmlp_ln_g, gmlp_ln_b, gmlp_w_s, gmlp_b_s, gmlp_w_out, x, mod,
                          layer, j)
        else:
            qkv = _qkv(x, mod, g_mix3, attn_w_qkv, attn_q_g, attn_k_g, layer, j)
            k_cols = slice(N_HEADS * HEAD_DIM, (N_HEADS + N_KV_HEADS) * HEAD_DIM)
            v_cols = slice((N_HEADS + N_KV_HEADS) * HEAD_DIM, QKV_OUT)
            new_k.append(qkv[:N_PROMPT_ROWS, k_cols].reshape(BATCH, SEQ, N_KV_HEADS, HEAD_DIM))
            new_v.append(qkv[:N_PROMPT_ROWS, v_cols].reshape(BATCH, SEQ, N_KV_HEADS, HEAD_DIM))
            a = _attention(qkv, ctx_k[:, j], ctx_v[:, j])
            x = _proj_res(a, attn_w_o, j, x, mod, layer)
        li = layer // 2
        if layer % 2 == 0:
            x = _ffn(x, mod, g_ffn3, ffn_w_gate, ffn_w_up, ffn_w_down, layer, li)
        else:
            route, route_t, cnt, tile_base = _router(x, mod, g_ffn3, moe_w_router, layer, li)
            plan = _dispatch_plan(route_t, cnt, tile_base)
            xs = _moe_scatter(x, mod, g_ffn3, plan, layer)
            ys = _moe_experts(xs, plan, moe_w_gate, moe_w_up, moe_w_down, li)
            x = _moe_combine(x, mod, route, ys, plan, layer, split_out=layer == DEPTH - 1)

    if DEPTH % 2 == 0:
        y_prompt, y_sample = x
    else:
        y_prompt, y_sample = x[:N_PROMPT_ROWS], x[N_PROMPT_ROWS:]
    return (y_prompt.reshape(BATCH, SEQ, D_MODEL), y_sample.reshape(DEC_BATCH, DEC_SEQ, D_MODEL),
            jnp.stack(new_k, axis=1), jnp.stack(new_v, axis=1))
```

```python
import functools

import jax
import jax.numpy as jnp
import numpy as np
from jax import lax
from jax.experimental import pallas as pl
from jax.experimental.pallas import tpu as pltpu

F32 = jnp.float32
BF16 = jnp.bfloat16

D_MODEL = 1024
BATCH = 16
SEQ = 256
DEPTH = 4
DEC_BATCH = 2
DEC_SEQ = 2048
PAST_LEN = 256
GRID_W = 64
N_MIXERS = 3
N_HEADS = 8
N_KV_HEADS = 2
HEAD_DIM = 128
Q_PER_KV = N_HEADS // N_KV_HEADS
ROPE_THETA = 10000.0
CONV_W = 3
CHUNK = 128
GMLP_DIM = 2 * D_MODEL
GMLP_GROUPS = 8
GMLP_GROUP_DIM = GMLP_DIM // GMLP_GROUPS
D_FF = 2816
N_EXPERTS = 8
D_FF_EXPERT = 3584
EPS = 1e-6
QKV_OUT = (N_HEADS + 2 * N_KV_HEADS) * HEAD_DIM

N_PROMPT_ROWS = BATCH * SEQ
N_SAMPLE_ROWS = DEC_BATCH * DEC_SEQ
N_ROWS = N_PROMPT_ROWS + N_SAMPLE_ROWS
SUBLANES = 8
LANES = 128
N_COND_ROWS = SUBLANES

V7X_VMEM_BYTES = 64 * 1024 * 1024
VMEM_LIMIT_BYTES = V7X_VMEM_BYTES - 8 * 1024 * 1024
NORM_ROWS = 256


def _params(n_axes):
    return pltpu.CompilerParams(
        dimension_semantics=("arbitrary",) * n_axes,
        vmem_limit_bytes=VMEM_LIMIT_BYTES)


def _dot(a, b):
    return jnp.dot(a, b, preferred_element_type=F32)


def _cond_row(i, tm):
    n_prompt = N_PROMPT_ROWS // tm
    per_seq = DEC_SEQ // tm
    return jnp.where(i < n_prompt, 0, 1 + (i - n_prompt) // per_seq)


def _mod_spec(layer, tm):
    return pl.BlockSpec((None, None, 6, D_MODEL),
                        lambda i, *_: (layer, _cond_row(i, tm), 0, 0))


def _gain_spec(layer):
    return pl.BlockSpec((None, 1, D_MODEL), lambda *_: (layer, 0, 0))


def _row_parts(x):
    return tuple(x) if isinstance(x, (tuple, list)) else (x,)


def _row_specs(x, tm):
    parts = _row_parts(x)
    block = (tm, parts[0].shape[1])
    if len(parts) == 1:
        return [pl.BlockSpec(block, lambda i, *_: (i, 0))]
    n_prompt = N_PROMPT_ROWS // tm
    return [pl.BlockSpec(block, lambda i, *_: (jnp.minimum(i, n_prompt - 1), 0)),
            pl.BlockSpec(block, lambda i, *_: (jnp.maximum(i - n_prompt, 0), 0))]


def _row_loader(refs, tm):
    if len(refs) == 1:
        return lambda rows: refs[0][rows, :]
    is_prompt = pl.program_id(0) < N_PROMPT_ROWS // tm
    return lambda rows: jnp.where(is_prompt, refs[0][rows, :], refs[1][rows, :])


def _norm_mod_rows(load, n_rows, g_ref, scale, shift, h_ref):
    gs = g_ref[...] * (1.0 + scale)

    def body(r, carry):
        rows = pl.ds(pl.multiple_of(r * NORM_ROWS, NORM_ROWS), NORM_ROWS)
        x = load(rows)
        ms = jnp.mean(x * x, axis=-1, keepdims=True)
        h_ref[rows, :] = (x * lax.rsqrt(ms + EPS) * gs + shift).astype(h_ref.dtype)
        return carry

    lax.fori_loop(0, n_rows // NORM_ROWS, body, 0)


def _norm_mod_to(x_ref, g_ref, scale, shift, h_ref):
    _norm_mod_rows(lambda rows: x_ref[rows, :], x_ref.shape[0], g_ref, scale, shift, h_ref)


def _mod_kernel(c_ref, w_ref, b_ref, o_ref):
    cnd = c_ref[...]
    s = (cnd * jax.nn.sigmoid(cnd)).astype(BF16)
    o_ref[...] = _dot(s, w_ref[...].astype(BF16)) + b_ref[...]


def _modulation(cond, w_mod, b_mod):
    tn = 1536
    n_out = 6 * D_MODEL
    out = pl.pallas_call(
        _mod_kernel,
        out_shape=jax.ShapeDtypeStruct((DEPTH, N_COND_ROWS, n_out), F32),
        grid=(DEPTH, n_out // tn),
        in_specs=[
            pl.BlockSpec((N_COND_ROWS, D_MODEL), lambda l, j: (0, 0)),
            pl.BlockSpec((None, D_MODEL, tn), lambda l, j: (l, 0, j)),
            pl.BlockSpec((None, 1, tn), lambda l, j: (l, 0, j)),
        ],
        out_specs=pl.BlockSpec((None, N_COND_ROWS, tn), lambda l, j: (l, 0, j)),
        compiler_params=_params(2),
        name="modulation",
    )(cond, w_mod, b_mod.reshape(DEPTH, 1, n_out))
    return out.reshape(DEPTH, N_COND_ROWS, 6, D_MODEL)


CONV_TM = 2048
CONV_TN = 256


def _conv_in_kernel(*refs, n_x):
    x_refs = refs[:n_x]
    mod_ref, g_ref, wb_ref, wc_ref, wx_ref, cw_ref, o_ref, h_ref = refs[n_x:]
    i = pl.program_id(0)

    @pl.when(pl.program_id(1) == 0)
    def _():
        _norm_mod_rows(_row_loader(x_refs, CONV_TM), CONV_TM, g_ref, mod_ref[1:2, :],
                       mod_ref[0:1, :], h_ref)

    h = h_ref[...]
    b = _dot(h, wb_ref[...].astype(BF16))
    z = _dot(h, wc_ref[...].astype(BF16)) * _dot(h, wx_ref[...].astype(BF16))
    seq_len = jnp.where(i < N_PROMPT_ROWS // CONV_TM, SEQ, DEC_SEQ)
    pos = lax.broadcasted_iota(jnp.int32, z.shape, 0) & (seq_len - 1)
    z_prev = jnp.where(pos == 0, 0.0, pltpu.roll(z, 1, 0))
    z_next = jnp.where(pos == seq_len - 1, 0.0, pltpu.roll(z, CONV_TM - 1, 0))
    conv = z_prev * cw_ref[0:1, :] + z * cw_ref[1:2, :] + z_next * cw_ref[2:3, :]
    o_ref[...] = (b * conv).astype(o_ref.dtype)


def _conv_in(x, mod, g_mix, conv_w_in, conv_w, layer, j):
    tm, tn = CONV_TM, CONV_TN
    nb = D_MODEL // tn
    w_spec = lambda part: pl.BlockSpec((None, D_MODEL, tn),
                                       lambda i, n: (j, 0, part * nb + n))
    x_parts = _row_parts(x)
    return pl.pallas_call(
        functools.partial(_conv_in_kernel, n_x=len(x_parts)),
        out_shape=jax.ShapeDtypeStruct((N_ROWS, D_MODEL), BF16),
        grid=(N_ROWS // tm, nb),
        in_specs=_row_specs(x, tm) + [
            _mod_spec(layer, tm),
            _gain_spec(layer),
            w_spec(0), w_spec(1), w_spec(2),
            pl.BlockSpec((None, CONV_W, tn), lambda i, n: (j, 0, n)),
        ],
        out_specs=pl.BlockSpec((tm, tn), lambda i, n: (i, n)),
        scratch_shapes=[pltpu.VMEM((tm, D_MODEL), BF16)],
        compiler_params=_params(2),
        name="conv_in",
    )(*x_parts, mod, g_mix, conv_w_in, conv_w_in, conv_w_in, conv_w)


PROJ_TM = 1024


def _proj_res_kernel(*refs, n_a, n_x):
    a_refs, w_ref, x_refs = refs[:n_a], refs[n_a], refs[n_a + 1:n_a + 1 + n_x]
    mod_ref, o_ref = refs[n_a + 1 + n_x:]
    whole = slice(None)
    y = _dot(_row_loader(a_refs, PROJ_TM)(whole), w_ref[...].astype(BF16))
    o_ref[...] = _row_loader(x_refs, PROJ_TM)(whole) + mod_ref[2:3, :] * y


def _proj_res(a, w, j, x, mod, layer):
    tm = PROJ_TM
    a_parts, x_parts = _row_parts(a), _row_parts(x)
    k = a_parts[0].shape[1]
    return pl.pallas_call(
        functools.partial(_proj_res_kernel, n_a=len(a_parts), n_x=len(x_parts)),
        out_shape=jax.ShapeDtypeStruct((N_ROWS, D_MODEL), F32),
        grid=(N_ROWS // tm,),
        in_specs=(_row_specs(a, tm) + [pl.BlockSpec((None, k, D_MODEL), lambda i: (j, 0, 0))]
                  + _row_specs(x, tm) + [_mod_spec(layer, tm)]),
        out_specs=pl.BlockSpec((tm, D_MODEL), lambda i: (i, 0)),
        compiler_params=_params(1),
        name="proj_res",
    )(*a_parts, w, *x_parts, mod)


GMLP_TM = 1024
GMLP_TN = 1024
GMLP_OUT_TM = 512


def _gelu_tanh(x):
    c = 0.7978845608028654
    return 0.5 * x * (1.0 + jnp.tanh(c * (x + 0.044715 * (x * x * x))))


def _gmlp_in_kernel(x_ref, mod_ref, g_ref, w_ref, b_ref, o_ref, h_ref):
    @pl.when(pl.program_id(1) == 0)
    def _():
        _norm_mod_to(x_ref, g_ref, mod_ref[1:2, :], mod_ref[0:1, :], h_ref)

    z = _dot(h_ref[...], w_ref[...].astype(BF16)) + b_ref[...]
    o_ref[...] = _gelu_tanh(z).astype(o_ref.dtype)


def _gmlp_in(x, mod, g_mix, w_in, b_in, layer, j):
    tm, tn = GMLP_TM, GMLP_TN
    n_out = 2 * GMLP_DIM
    return pl.pallas_call(
        _gmlp_in_kernel,
        out_shape=jax.ShapeDtypeStruct((N_ROWS, n_out), BF16),
        grid=(N_ROWS // tm, n_out // tn),
        in_specs=[
            pl.BlockSpec((tm, D_MODEL), lambda i, n: (i, 0)),
            _mod_spec(layer, tm),
            _gain_spec(layer),
            pl.BlockSpec((None, D_MODEL, tn), lambda i, n: (j, 0, n)),
            pl.BlockSpec((None, 1, tn), lambda i, n: (j, 0, n)),
        ],
        out_specs=pl.BlockSpec((tm, tn), lambda i, n: (i, n)),
        scratch_shapes=[pltpu.VMEM((tm, D_MODEL), BF16)],
        compiler_params=_params(2),
        name="gmlp_in",
    )(x, mod, g_mix, w_in, b_in.reshape(-1, 1, n_out))


def _gmlp_out_kernel(u_ref, v_ref, lng_ref, lnb_ref, ws_ref, bs_ref, w_ref, x_ref, mod_ref,
                     o_ref, a_ref):
    lng = lng_ref[...]
    lnb = lnb_ref[...]

    def chunk(c, carry):
        rows = pl.ds(pl.multiple_of(c * CHUNK, CHUNK), CHUNK)
        v = v_ref[rows, :].astype(F32)
        mu = jnp.mean(v, axis=-1, keepdims=True)
        vc = v - mu
        var = jnp.mean(vc * vc, axis=-1, keepdims=True)
        vn = (vc * lax.rsqrt(var + EPS) * lng + lnb).astype(BF16)
        for g in range(GMLP_GROUPS):
            cols = slice(g * GMLP_GROUP_DIM, (g + 1) * GMLP_GROUP_DIM)
            mixed = _dot(ws_ref[g].astype(BF16), vn[:, cols]) + bs_ref[:, g:g + 1]
            a_ref[rows, cols] = (u_ref[rows, cols].astype(F32) * mixed).astype(a_ref.dtype)
        return carry

    lax.fori_loop(0, u_ref.shape[0] // CHUNK, chunk, 0)
    y = _dot(a_ref[...], w_ref[...].astype(BF16))
    o_ref[...] = x_ref[...] + mod_ref[2:3, :] * y


def _gmlp_out(z, ln_g, ln_b, w_s, b_s, w_out, x, mod, layer, j):
    tm = GMLP_OUT_TM
    return pl.pallas_call(
        _gmlp_out_kernel,
        out_shape=jax.ShapeDtypeStruct((N_ROWS, D_MODEL), F32),
        grid=(N_ROWS // tm,),
        in_specs=[
            pl.BlockSpec((tm, GMLP_DIM), lambda i: (i, 0)),
            pl.BlockSpec((tm, GMLP_DIM), lambda i: (i, 1)),
            pl.BlockSpec((None, 1, GMLP_DIM), lambda i: (j, 0, 0)),
            pl.BlockSpec((None, 1, GMLP_DIM), lambda i: (j, 0, 0)),
            pl.BlockSpec((None, GMLP_GROUPS, CHUNK, CHUNK), lambda i: (j, 0, 0, 0)),
            pl.BlockSpec((None, CHUNK, GMLP_GROUPS), lambda i: (j, 0, 0)),
            pl.BlockSpec((None, GMLP_DIM, D_MODEL), lambda i: (j, 0, 0)),
            pl.BlockSpec((tm, D_MODEL), lambda i: (i, 0)),
            _mod_spec(layer, tm),
        ],
        out_specs=pl.BlockSpec((tm, D_MODEL), lambda i: (i, 0)),
        scratch_shapes=[pltpu.VMEM((tm, GMLP_DIM), BF16)],
        compiler_params=_params(1),
        name="gmlp_out",
    )(z, z, ln_g.reshape(-1, 1, GMLP_DIM), ln_b.reshape(-1, 1, GMLP_DIM), w_s,
      jnp.swapaxes(b_s, 1, 2), w_out, x, mod)


QKV_TM = 1024
ATTN_TQ = 1024
ATTN_PROMPT_SEQS = 4


def _rope_tables(tm):
    pos = np.arange(DEC_SEQ)
    row = (pos // GRID_W).astype(np.float32)
    col = (pos % GRID_W).astype(np.float32)
    n_freq = HEAD_DIM // 4
    inv_freq = np.power(np.float32(ROPE_THETA), -np.arange(n_freq, dtype=np.float32) / n_freq)
    ang = np.concatenate([row[:, None] * inv_freq, col[:, None] * inv_freq], axis=-1)
    cos = np.repeat(np.cos(ang), 2, axis=-1)
    sign = np.tile(np.array([-1.0, 1.0], np.float32), HEAD_DIM // 2)
    sin = np.repeat(np.sin(ang), 2, axis=-1) * sign
    cos = np.concatenate([np.ones((tm, HEAD_DIM), np.float32), cos], axis=0)
    sin = np.concatenate([np.zeros((tm, HEAD_DIM), np.float32), sin], axis=0)
    return jnp.asarray(cos, F32), jnp.asarray(sin, F32)


def _swap_pairs(x):
    even = (lax.broadcasted_iota(jnp.int32, x.shape, 1) & 1) == 0
    return jnp.where(even, pltpu.roll(x, HEAD_DIM - 1, 1), pltpu.roll(x, 1, 1))


def _qkv_kernel(x_ref, mod_ref, g_ref, w_ref, qg_ref, kg_ref, cos_ref, sin_ref, o_ref,
                h_ref, wsw_ref):
    n_rot_heads = N_HEADS + N_KV_HEADS

    @pl.when(pl.program_id(0) == 0)
    def _():
        for hd in range(n_rot_heads):
            cols = slice(hd * HEAD_DIM, (hd + 1) * HEAD_DIM)
            wsw_ref[:, cols] = _swap_pairs(w_ref[:, cols]).astype(BF16)

    _norm_mod_to(x_ref, g_ref, mod_ref[1:2, :], mod_ref[0:1, :], h_ref)
    y = _dot(h_ref[...], w_ref[...].astype(BF16))
    y_sw = _dot(h_ref[...], wsw_ref[...])
    cos = cos_ref[...]
    sin = sin_ref[...]
    for hd in range(n_rot_heads):
        cols = slice(hd * HEAD_DIM, (hd + 1) * HEAD_DIM)
        yh = y[:, cols]
        gain = qg_ref[...] if hd < N_HEADS else kg_ref[...]
        inv = lax.rsqrt(jnp.mean(yh * yh, axis=-1, keepdims=True) + EPS)
        o_ref[:, cols] = (yh * (gain * cos) + y_sw[:, cols] * (_swap_pairs(gain) * sin)) * inv
    v_cols = slice(n_rot_heads * HEAD_DIM, QKV_OUT)
    o_ref[:, v_cols] = y[:, v_cols]


def _qkv(x, mod, g_mix, w_qkv, q_g, k_g, layer, j):
    tm = QKV_TM
    cos, sin = _rope_tables(tm)
    n_prompt = N_PROMPT_ROWS // tm
    per_seq = DEC_SEQ // tm
    tbl = lambda i: (jnp.where(i < n_prompt, 0, 1 + (i - n_prompt) % per_seq), 0)
    return pl.pallas_call(
        _qkv_kernel,
        out_shape=jax.ShapeDtypeStruct((N_ROWS, QKV_OUT), F32),
        grid=(N_ROWS // tm,),
        in_specs=[
            pl.BlockSpec((tm, D_MODEL), lambda i: (i, 0)),
            _mod_spec(layer, tm),
            _gain_spec(layer),
            pl.BlockSpec((None, D_MODEL, QKV_OUT), lambda i: (j, 0, 0)),
            pl.BlockSpec((None, 1, HEAD_DIM), lambda i: (j, 0, 0)),
            pl.BlockSpec((None, 1, HEAD_DIM), lambda i: (j, 0, 0)),
            pl.BlockSpec((tm, HEAD_DIM), tbl),
            pl.BlockSpec((tm, HEAD_DIM), tbl),
        ],
        out_specs=pl.BlockSpec((tm, QKV_OUT), lambda i: (i, 0)),
        scratch_shapes=[pltpu.VMEM((tm, D_MODEL), BF16),
                        pltpu.VMEM((D_MODEL, (N_HEADS + N_KV_HEADS) * HEAD_DIM), BF16)],
        compiler_params=_params(1),
        name="qkv",
    )(x, mod, g_mix, w_qkv, q_g.reshape(-1, 1, HEAD_DIM), k_g.reshape(-1, 1, HEAD_DIM), cos, sin)


def _attn_kernel(*refs, has_ctx, n_seq):
    if has_ctx:
        q_ref, k_ref, v_ref, ck_ref, cv_ref, o_ref = refs
    else:
        q_ref, k_ref, v_ref, o_ref = refs
    nt = (((1,), (1,)), ((), ()))
    if has_ctx:
        ck = ck_ref[...].astype(BF16)
        cv = cv_ref[...].astype(BF16)
    q_len = q_ref.shape[0] // n_seq
    k_len = k_ref.shape[0] // n_seq
    for sq in range(n_seq):
        q_rows = slice(sq * q_len, (sq + 1) * q_len)
        k_rows = slice(sq * k_len, (sq + 1) * k_len)
        k = k_ref[k_rows, :].astype(BF16)
        v = v_ref[k_rows, :].astype(BF16)
        for g in range(Q_PER_KV):
            cols = slice(g * HEAD_DIM, (g + 1) * HEAD_DIM)
            q = (q_ref[q_rows, cols] * (HEAD_DIM ** -0.5)).astype(BF16)
            s = lax.dot_general(q, k, nt, preferred_element_type=F32)
            m = jnp.max(s, axis=-1, keepdims=True)
            if has_ctx:
                sc = lax.dot_general(q, ck, nt, preferred_element_type=F32)
                m = jnp.maximum(m, jnp.max(sc, axis=-1, keepdims=True))
            p = jnp.exp(s - m)
            den = jnp.sum(p, axis=-1, keepdims=True)
            o = _dot(p.astype(BF16), v)
            if has_ctx:
                pc = jnp.exp(sc - m)
                den = den + jnp.sum(pc, axis=-1, keepdims=True)
                o = o + _dot(pc.astype(BF16), cv)
            o_ref[q_rows, cols] = (o / den).astype(o_ref.dtype)


def _attention(qkv, ctx_k, ctx_v):
    group_w = Q_PER_KV * HEAD_DIM
    k_col = N_HEADS
    v_col = N_HEADS + N_KV_HEADS

    def call(n_batch, seq, tq, row0, ctx, n_seq):
        q_blocks = seq // tq
        q_rows, kv_rows = n_seq * tq, n_seq * seq
        o_row = lambda b, qi: b * q_blocks + qi
        kv_row = lambda b: row0 // kv_rows + b
        in_specs = [
            pl.BlockSpec((q_rows, group_w), lambda b, kh, qi: (row0 // q_rows + o_row(b, qi), kh)),
            pl.BlockSpec((kv_rows, HEAD_DIM), lambda b, kh, qi: (kv_row(b), k_col + kh)),
            pl.BlockSpec((kv_rows, HEAD_DIM), lambda b, kh, qi: (kv_row(b), v_col + kh)),
        ]
        args = [qkv, qkv, qkv]
        if ctx is not None:
            ctx_spec = pl.BlockSpec((None, PAST_LEN, HEAD_DIM), lambda b, kh, qi: (b, 0, kh))
            in_specs += [ctx_spec, ctx_spec]
            args += list(ctx)
        return pl.pallas_call(
            functools.partial(_attn_kernel, has_ctx=ctx is not None, n_seq=n_seq),
            out_shape=jax.ShapeDtypeStruct((n_batch * seq, N_HEADS * HEAD_DIM), BF16),
            grid=(n_batch // n_seq, N_KV_HEADS, q_blocks),
            in_specs=in_specs,
            out_specs=pl.BlockSpec((q_rows, group_w), lambda b, kh, qi: (o_row(b, qi), kh)),
            compiler_params=_params(3),
            name="attention",
        )(*args)

    o_prompt = call(BATCH, SEQ, SEQ, 0, None, ATTN_PROMPT_SEQS)
    o_sample = call(DEC_BATCH, DEC_SEQ, ATTN_TQ, N_PROMPT_ROWS, (ctx_k, ctx_v), 1)
    return o_prompt, o_sample


FFN_TM = 1024
FFN_TF = 256
TOP_K = 2
ROUTER_TM = 1024
ROUTE_LANES = LANES
MOE_TF = 512
MOE_TILE = 1024
MOE_SUB = 128
MOE_MAX_TILES = TOP_K * N_ROWS // MOE_TILE + N_EXPERTS
MOE_ROWS = MOE_MAX_TILES * MOE_TILE
SCATTER_TM = 512
COMBINE_TM = 256
WIN_CHUNK = 128
WIN_CHUNKS = COMBINE_TM // WIN_CHUNK + 1
WIN_SLOTS = TOP_K * COMBINE_TM // WIN_CHUNK + 2 * N_EXPERTS
WIN_SLOT_STEPS = (12, 16, WIN_SLOTS)


def _swiglu_hidden(h, wg, wu):
    gate = _dot(h, wg)
    return (gate * jax.nn.sigmoid(gate) * _dot(h, wu)).astype(BF16)


def _ffn_kernel(x_ref, mod_ref, g_ref, wg_ref, wu_ref, wd_ref, o_ref, h_ref):
    f = pl.program_id(1)

    @pl.when(f == 0)
    def _():
        _norm_mod_to(x_ref, g_ref, mod_ref[4:5, :], mod_ref[3:4, :], h_ref)
        o_ref[...] = jnp.zeros_like(o_ref)

    a = _swiglu_hidden(h_ref[...], wg_ref[...].astype(BF16), wu_ref[...].astype(BF16))
    o_ref[...] += _dot(a, wd_ref[...].astype(BF16))

    @pl.when(f == pl.num_programs(1) - 1)
    def _():
        o_ref[...] = x_ref[...] + mod_ref[5:6, :] * o_ref[...]


def _ffn(x, mod, g_ffn, w_gate, w_up, w_down, layer, li):
    tm, tf = FFN_TM, FFN_TF
    return pl.pallas_call(
        _ffn_kernel,
        out_shape=jax.ShapeDtypeStruct((N_ROWS, D_MODEL), F32),
        grid=(N_ROWS // tm, D_FF // tf),
        in_specs=[
            pl.BlockSpec((tm, D_MODEL), lambda i, f: (i, 0)),
            _mod_spec(layer, tm),
            _gain_spec(layer),
            pl.BlockSpec((None, D_MODEL, tf), lambda i, f: (li, 0, f)),
            pl.BlockSpec((None, D_MODEL, tf), lambda i, f: (li, 0, f)),
            pl.BlockSpec((None, tf, D_MODEL), lambda i, f: (li, f, 0)),
        ],
        out_specs=pl.BlockSpec((tm, D_MODEL), lambda i, f: (i, 0)),
        scratch_shapes=[pltpu.VMEM((tm, D_MODEL), BF16)],
        compiler_params=_params(2),
        name="ffn",
    )(x, mod, g_ffn, w_gate, w_up, w_down)


def _split_bf16(x):
    hi = x.astype(BF16)
    return hi, (x - hi.astype(F32)).astype(BF16)


def _router_kernel(x_ref, mod_ref, g_ref, w_ref, route_ref, route_t_ref, cnt_ref, base_ref,
                   h_ref, seen_ref):
    @pl.when(pl.program_id(0) == 0)
    def _():
        seen_ref[...] = jnp.zeros_like(seen_ref)

    _norm_mod_to(x_ref, g_ref, mod_ref[4:5, :], mod_ref[3:4, :], h_ref)
    h_hi, h_lo = _split_bf16(h_ref[...])
    w_hi, w_lo = _split_bf16(w_ref[...])
    logits = _dot(h_hi, w_hi) + (_dot(h_hi, w_lo) + _dot(h_lo, w_hi))
    lane = lax.broadcasted_iota(jnp.int32, logits.shape, 1)
    logits = jnp.where(lane < N_EXPERTS, logits, -jnp.inf)
    m1 = jnp.max(logits, axis=-1, keepdims=True)
    i1 = jnp.min(jnp.where(logits == m1, lane, N_EXPERTS), axis=-1, keepdims=True)
    rest = jnp.where(lane == i1, -jnp.inf, logits)
    m2 = jnp.max(rest, axis=-1, keepdims=True)
    i2 = jnp.min(jnp.where(rest == m2, lane, N_EXPERTS), axis=-1, keepdims=True)
    e2 = jnp.exp(m2 - m1)
    w1 = 1.0 / (1.0 + e2)
    chosen = jnp.where((lane == i1) | (lane == i2), 1.0, 0.0)
    tm = chosen.shape[0]
    blk = COMBINE_TM
    earlier = (lax.broadcasted_iota(jnp.int32, (blk, blk), 1)
               < lax.broadcasted_iota(jnp.int32, (blk, blk), 0))
    earlier = jnp.where(earlier, 1.0, 0.0).astype(BF16)
    base_ref[...] = jnp.zeros_like(base_ref)
    seen = seen_ref[...]
    priors = []
    for k in range(tm // blk):
        rows = slice(k * blk, (k + 1) * blk)
        base_ref[k:k + 1, :] = seen
        priors.append(_dot(earlier, chosen[rows].astype(BF16)) + seen)
        seen = seen + jnp.sum(chosen[rows], axis=0, keepdims=True)
    seen_ref[...] = seen
    prior = jnp.concatenate(priors, axis=0)
    rank1 = jnp.sum(jnp.where(lane == i1, prior, 0.0), axis=-1, keepdims=True)
    rank2 = jnp.sum(jnp.where(lane == i2, prior, 0.0), axis=-1, keepdims=True)
    route = jnp.zeros_like(logits)
    for k, field in enumerate((w1, e2 * w1, i1.astype(F32), i2.astype(F32), rank1, rank2)):
        route = jnp.where(lane == k, field, route)
    route_ref[...] = route
    route_t_ref[...] = jnp.transpose(route)[:SUBLANES, :]
    cnt_ref[...] = jnp.broadcast_to(seen, cnt_ref.shape)


def _router(x, mod, g_ffn, w_router, layer, li):
    tm = ROUTER_TM
    w_pad = jnp.pad(w_router, ((0, 0), (0, 0), (0, ROUTE_LANES - N_EXPERTS)))
    return pl.pallas_call(
        _router_kernel,
        out_shape=(jax.ShapeDtypeStruct((N_ROWS, ROUTE_LANES), F32),
                   jax.ShapeDtypeStruct((SUBLANES, N_ROWS), F32),
                   jax.ShapeDtypeStruct((SUBLANES, ROUTE_LANES), F32),
                   jax.ShapeDtypeStruct((N_ROWS // tm, SUBLANES, ROUTE_LANES), F32)),
        grid=(N_ROWS // tm,),
        in_specs=[
            pl.BlockSpec((tm, D_MODEL), lambda i: (i, 0)),
            _mod_spec(layer, tm),
            _gain_spec(layer),
            pl.BlockSpec((None, D_MODEL, ROUTE_LANES), lambda i: (li, 0, 0)),
        ],
        out_specs=(pl.BlockSpec((tm, ROUTE_LANES), lambda i: (i, 0)),
                   pl.BlockSpec((SUBLANES, tm), lambda i: (0, i)),
                   pl.BlockSpec((SUBLANES, ROUTE_LANES), lambda i: (0, 0)),
                   pl.BlockSpec((None, SUBLANES, ROUTE_LANES), lambda i: (i, 0, 0))),
        scratch_shapes=[pltpu.VMEM((tm, D_MODEL), F32), pltpu.VMEM((1, ROUTE_LANES), F32)],
        compiler_params=_params(1),
        name="router",
    )(x, mod, g_ffn, w_pad)


def _dispatch_plan(route_t, cnt, tile_base):
    counts = cnt[0, :N_EXPERTS].astype(jnp.int32)
    e1, e2, r1, r2 = (route_t[k].astype(jnp.int32) for k in (2, 3, 4, 5))
    n_tiles = (counts + MOE_TILE - 1) // MOE_TILE
    tile_end = jnp.cumsum(n_tiles)
    tile_start = tile_end - n_tiles
    row_start = tile_start * MOE_TILE
    pos = jnp.concatenate([row_start[e1] + r1, row_start[e2] + r2])
    n_active = tile_end[-1]
    t = jnp.arange(MOE_MAX_TILES, dtype=jnp.int32)
    tile_block = jnp.minimum(t, n_active - 1)
    tile_expert = jnp.sum(tile_block[:, None] >= tile_end[None, :], axis=1).astype(jnp.int32)
    valid = counts[tile_expert] - (tile_block - tile_start[tile_expert]) * MOE_TILE
    n_sub = (jnp.clip(valid, 0, MOE_TILE) + MOE_SUB - 1) // MOE_SUB
    tile_nsub = jnp.where(t < n_active, n_sub, 0).astype(jnp.int32)
    pad = jnp.concatenate([row_start + counts, (-counts) % MOE_SUB]).astype(jnp.int32)
    base = tile_base[:, :ROUTER_TM // COMBINE_TM, :N_EXPERTS].astype(jnp.int32)
    base = base.reshape(N_ROWS // COMBINE_TM, N_EXPERTS)
    lo = row_start[None, :] + base
    hi = row_start[None, :] + jnp.concatenate([base[1:], counts[None, :]], axis=0)
    chunk0 = lo // WIN_CHUNK
    n_chunks = jnp.where(hi > lo, (hi + WIN_CHUNK - 1) // WIN_CHUNK - chunk0, 0)
    j = jnp.arange(WIN_CHUNKS, dtype=jnp.int32)
    needed = (j[None, None, :] < n_chunks[:, :, None]).reshape(-1, N_EXPERTS * WIN_CHUNKS)
    chunk = (chunk0[:, :, None] + j[None, None, :]).reshape(-1, N_EXPERTS * WIN_CHUNKS)
    order = jnp.argsort(jnp.logical_not(needed), axis=1, stable=True)[:, :WIN_SLOTS]
    win_count = jnp.sum(needed, axis=1).astype(jnp.int32)
    win_list = jnp.where(jnp.arange(WIN_SLOTS)[None, :] < win_count[:, None],
                         jnp.take_along_axis(chunk, order, axis=1), -1)
    return dict(pos=pos, pad=pad, tile_expert=tile_expert, tile_nsub=tile_nsub,
                tile_block=tile_block, row_start=row_start.astype(jnp.int32),
                win_list=win_list.reshape(-1).astype(jnp.int32), win_count=win_count)


def _for_rows(n_rows, body):
    def group(j, carry):
        r0 = pl.multiple_of(j * SUBLANES, SUBLANES)
        for u in range(SUBLANES):
            body(r0 + u)
        return carry
    lax.fori_loop(0, n_rows // SUBLANES, group, 0)


def _scatter_kernel(pos_ref, pad_ref, tn_ref, x_ref, mod_ref, g_ref, xs_ref, h_ref, zero_ref,
                    sem, fill_sem):
    i = pl.program_id(0)
    ts = x_ref.shape[0]
    slot = lax.rem(i, 2)
    _norm_mod_to(x_ref, g_ref, mod_ref[4:5, :], mod_ref[3:4, :], h_ref.at[slot])

    def for_row_copies(step, step_slot, act):
        first = step * ts

        def row(r):
            for k in range(TOP_K):
                dst = pos_ref[k * N_ROWS + first + r]
                act(pltpu.make_async_copy(h_ref.at[step_slot, pl.ds(r, 1)],
                                          xs_ref.at[pl.ds(dst, 1)], sem.at[step_slot]), k)
        _for_rows(ts, row)

    for_row_copies(i, slot, lambda cp, k: cp.start(priority=k))

    @pl.when(i == 0)
    def _():
        zero_ref[...] = jnp.zeros_like(zero_ref)

        def fills(act):
            for e in range(N_EXPERTS):
                def row_body(r, carry, e=e):
                    act(pltpu.make_async_copy(zero_ref.at[pl.ds(0, 1)],
                                              xs_ref.at[pl.ds(pad_ref[e] + r, 1)], fill_sem))
                    return carry
                lax.fori_loop(0, pad_ref[N_EXPERTS + e], row_body, 0)

            def tile_body(t, carry):
                def sub_body(s, carry):
                    row = pl.multiple_of(t * MOE_TILE + s * MOE_SUB, MOE_SUB)
                    act(pltpu.make_async_copy(zero_ref, xs_ref.at[pl.ds(row, MOE_SUB)], fill_sem))
                    return carry
                return lax.fori_loop(tn_ref[t], MOE_TILE // MOE_SUB, sub_body, carry)
            lax.fori_loop(0, MOE_MAX_TILES, tile_body, 0)

        fills(lambda cp: cp.start())
        fills(lambda cp: cp.wait())

    @pl.when(i > 0)
    def _():
        for_row_copies(i - 1, 1 - slot, lambda cp, k: cp.wait())

    @pl.when(i == pl.num_programs(0) - 1)
    def _():
        for_row_copies(i, slot, lambda cp, k: cp.wait())


def _moe_scatter(x, mod, g_ffn, plan, layer):
    ts = SCATTER_TM
    grid_spec = pltpu.PrefetchScalarGridSpec(
        num_scalar_prefetch=3,
        grid=(N_ROWS // ts,),
        in_specs=[
            pl.BlockSpec((ts, D_MODEL), lambda i, *_: (i, 0)),
            _mod_spec(layer, ts),
            _gain_spec(layer),
        ],
        out_specs=pl.BlockSpec(memory_space=pl.ANY),
        scratch_shapes=[pltpu.VMEM((2, ts, D_MODEL), F32),
                        pltpu.VMEM((MOE_SUB, D_MODEL), F32),
                        pltpu.SemaphoreType.DMA((2,)), pltpu.SemaphoreType.DMA(())],
    )
    return pl.pallas_call(
        _scatter_kernel,
        out_shape=jax.ShapeDtypeStruct((MOE_ROWS, D_MODEL), F32),
        grid_spec=grid_spec,
        compiler_params=_params(1),
        name="moe_scatter",
    )(plan["pos"], plan["pad"], plan["tile_nsub"], x, mod, g_ffn)


def _moe_kernel(te_ref, tn_ref, tb_ref, x_ref, wg_ref, wu_ref, wd_ref, o_ref,
                xb_ref, acc_ref):
    n_sub = tn_ref[pl.program_id(0)]
    first_f = pl.program_id(1) == 0

    @pl.when(first_f)
    def _():
        acc_ref[...] = jnp.zeros_like(acc_ref)

    for n in range(1, MOE_TILE // MOE_SUB + 1):
        @pl.when(n_sub == n)
        def _(n=n):
            rows = slice(0, n * MOE_SUB)

            @pl.when(first_f)
            def _():
                xb_ref[rows, :] = x_ref[rows, :].astype(BF16)

            a = _swiglu_hidden(xb_ref[rows, :], wg_ref[...].astype(BF16),
                               wu_ref[...].astype(BF16))
            acc_ref[rows, :] += _dot(a, wd_ref[...].astype(BF16))

    @pl.when(pl.program_id(1) == pl.num_programs(1) - 1)
    def _():
        o_ref[...] = acc_ref[...].astype(o_ref.dtype)


def _moe_experts(xs, plan, w_gate, w_up, w_down, li):
    tile, tf = MOE_TILE, MOE_TF
    nf = D_FF_EXPERT // tf
    frozen_f = lambda i, f, tn: jnp.where(tn[i] > 0, f, nf - 1)
    grid_spec = pltpu.PrefetchScalarGridSpec(
        num_scalar_prefetch=3,
        grid=(MOE_MAX_TILES, nf),
        in_specs=[
            pl.BlockSpec((tile, D_MODEL), lambda i, f, te, tn, tb: (tb[i], 0)),
            pl.BlockSpec((None, None, D_MODEL, tf),
                         lambda i, f, te, tn, tb: (li, te[i], 0, frozen_f(i, f, tn))),
            pl.BlockSpec((None, None, D_MODEL, tf),
                         lambda i, f, te, tn, tb: (li, te[i], 0, frozen_f(i, f, tn))),
            pl.BlockSpec((None, None, tf, D_MODEL),
                         lambda i, f, te, tn, tb: (li, te[i], frozen_f(i, f, tn), 0)),
        ],
        out_specs=pl.BlockSpec((tile, D_MODEL), lambda i, f, te, tn, tb: (i, 0)),
        scratch_shapes=[pltpu.VMEM((tile, D_MODEL), BF16), pltpu.VMEM((tile, D_MODEL), F32)],
    )
    return pl.pallas_call(
        _moe_kernel,
        out_shape=jax.ShapeDtypeStruct((MOE_ROWS, D_MODEL), BF16),
        grid_spec=grid_spec,
        compiler_params=_params(2),
        name="moe_experts",
    )(plan["tile_expert"], plan["tile_nsub"], plan["tile_block"], xs, w_gate, w_up, w_down)


def _combine_kernel(start_ref, list_ref, n_ref, x_ref, mod_ref, route_ref, ys_ref, *refs):
    *o_refs, win_ref, sem = refs
    i = pl.program_id(0)
    slot = lax.rem(i, 2)

    def window_copies(tile, tile_slot, act):
        def body(s, carry):
            chunk = list_ref[tile * WIN_SLOTS + s]
            src = ys_ref.at[pl.ds(pl.multiple_of(chunk * WIN_CHUNK, WIN_CHUNK), WIN_CHUNK)]
            dst = win_ref.at[tile_slot,
                             pl.ds(pl.multiple_of(s * WIN_CHUNK, WIN_CHUNK), WIN_CHUNK)]
            act(pltpu.make_async_copy(src, dst, sem.at[tile_slot]))
            return carry
        lax.fori_loop(0, n_ref[tile], body, 0)

    @pl.when(i == 0)
    def _():
        win_ref[...] = jnp.zeros_like(win_ref)
        window_copies(i, slot, lambda cp: cp.start())

    @pl.when(i + 1 < pl.num_programs(0))
    def _():
        window_copies(i + 1, 1 - slot, lambda cp: cp.start())

    window_copies(i, slot, lambda cp: cp.wait())

    w1, w2, e1, e2, pos1, pos2 = (route_ref[:, k:k + 1] for k in range(6))
    for e in range(N_EXPERTS):
        first_row = start_ref[e].astype(F32)
        pos1 = pos1 + jnp.where(e1 == e, first_row, 0.0)
        pos2 = pos2 + jnp.where(e2 == e, first_row, 0.0)
    lane = lax.broadcasted_iota(jnp.int32, (x_ref.shape[0], WIN_CHUNK), 1).astype(F32)

    def combine(n_slots):
        blocks = []
        for s in range(n_slots):
            rows = lane + (list_ref[i * WIN_SLOTS + s] * WIN_CHUNK).astype(F32)
            weight = jnp.where(pos1 == rows, w1, 0.0) + jnp.where(pos2 == rows, w2, 0.0)
            blocks.append(weight.astype(BF16))
        y = _dot(jnp.concatenate(blocks, axis=1), win_ref[slot, :n_slots * WIN_CHUNK, :])
        out = x_ref[...] + mod_ref[5:6, :] * y
        if len(o_refs) == 1:
            o_refs[0][...] = out
        else:
            is_prompt = i < N_PROMPT_ROWS // x_ref.shape[0]

            @pl.when(is_prompt)
            def _():
                o_refs[0][...] = out

            @pl.when(jnp.logical_not(is_prompt))
            def _():
                o_refs[1][...] = out

    n_chunks = n_ref[i]
    smaller = -1
    for n_slots in WIN_SLOT_STEPS:
        @pl.when((n_chunks > smaller) & (n_chunks <= n_slots))
        def _(n_slots=n_slots):
            combine(n_slots)
        smaller = n_slots


def _moe_combine(x, mod, route, ys, plan, layer, split_out):
    tc = COMBINE_TM
    if split_out:
        half = jax.ShapeDtypeStruct((N_PROMPT_ROWS, D_MODEL), F32)
        out_shape = (half, jax.ShapeDtypeStruct((N_SAMPLE_ROWS, D_MODEL), F32))
        out_specs = tuple(_row_specs((half, half), tc))
    else:
        out_shape = jax.ShapeDtypeStruct((N_ROWS, D_MODEL), F32)
        out_specs = pl.BlockSpec((tc, D_MODEL), lambda i, *_: (i, 0))
    grid_spec = pltpu.PrefetchScalarGridSpec(
        num_scalar_prefetch=3,
        grid=(N_ROWS // tc,),
        in_specs=[
            pl.BlockSpec((tc, D_MODEL), lambda i, *_: (i, 0)),
            _mod_spec(layer, tc),
            pl.BlockSpec((tc, ROUTE_LANES), lambda i, *_: (i, 0)),
            pl.BlockSpec(memory_space=pl.ANY),
        ],
        out_specs=out_specs,
        scratch_shapes=[pltpu.VMEM((2, WIN_SLOTS * WIN_CHUNK, D_MODEL), BF16),
                        pltpu.SemaphoreType.DMA((2,))],
    )
    return pl.pallas_call(
        _combine_kernel,
        out_shape=out_shape,
        grid_spec=grid_spec,
        compiler_params=_params(1),
        name="moe_combine",
    )(plan["row_start"], plan["win_list"], plan["win_count"], x, mod, route, ys)


def kernel(x_prompt, x_sample, cache_k, cache_v, c, c_ctx, w_mod, b_mod, g_mix, g_ffn,
           conv_w_in, conv_w, conv_w_out,
           gmlp_w_in, gmlp_b_in, gmlp_ln_g, gmlp_ln_b, gmlp_w_s, gmlp_b_s, gmlp_w_out,
           attn_w_qkv, attn_q_g, attn_k_g, attn_w_o,
           ffn_w_gate, ffn_w_up, ffn_w_down,
           moe_w_router, moe_w_gate, moe_w_up, moe_w_down):
    x = (x_prompt.reshape(N_PROMPT_ROWS, D_MODEL), x_sample.reshape(N_SAMPLE_ROWS, D_MODEL))
    cond = jnp.zeros((N_COND_ROWS, D_MODEL), F32).at[0].set(c_ctx).at[1:1 + DEC_BATCH].set(c)
    mod = _modulation(cond, w_mod, b_mod)
    g_mix3 = g_mix.reshape(DEPTH, 1, D_MODEL)
    g_ffn3 = g_ffn.reshape(DEPTH, 1, D_MODEL)
    n_attn = cache_k.shape[1]
    ctx_k = cache_k.reshape(DEC_BATCH, n_attn, PAST_LEN, N_KV_HEADS * HEAD_DIM)
    ctx_v = cache_v.reshape(DEC_BATCH, n_attn, PAST_LEN, N_KV_HEADS * HEAD_DIM)

    new_k, new_v = [], []
    for layer in range(DEPTH):
        kind, j = layer % N_MIXERS, layer // N_MIXERS
        if kind == 0:
            a = _conv_in(x, mod, g_mix3, conv_w_in, conv_w, layer, j)
            x = _proj_res(a, conv_w_out, j, x, mod, layer)
        elif kind == 1:
            z = _gmlp_in(x, mod, g_mix3, gmlp_w_in, gmlp_b_in, layer, j)
            x = _gmlp_out(z, gmlp_ln_g, gmlp_ln_b, gmlp_w_s, gmlp_b_s, gmlp_w_out, x, mod,
                          layer, j)
        else:
            qkv = _qkv(x, mod, g_mix3, attn_w_qkv, attn_q_g, attn_k_g, layer, j)
            k_cols = slice(N_HEADS * HEAD_DIM, (N_HEADS + N_KV_HEADS) * HEAD_DIM)
            v_cols = slice((N_HEADS + N_KV_HEADS) * HEAD_DIM, QKV_OUT)
            new_k.append(qkv[:N_PROMPT_ROWS, k_cols].reshape(BATCH, SEQ, N_KV_HEADS, HEAD_DIM))
            new_v.append(qkv[:N_PROMPT_ROWS, v_cols].reshape(BATCH, SEQ, N_KV_HEADS, HEAD_DIM))
            a = _attention(qkv, ctx_k[:, j], ctx_v[:, j])
            x = _proj_res(a, attn_w_o, j, x, mod, layer)
        li = layer // 2
        if layer % 2 == 0:
            x = _ffn(x, mod, g_ffn3, ffn_w_gate, ffn_w_up, ffn_w_down, layer, li)
        else:
            route, route_t, cnt, tile_base = _router(x, mod, g_ffn3, moe_w_router, layer, li)
            plan = _dispatch_plan(route_t, cnt, tile_base)
            xs = _moe_scatter(x, mod, g_ffn3, plan, layer)
            ys = _moe_experts(xs, plan, moe_w_gate, moe_w_up, moe_w_down, li)
            x = _moe_combine(x, mod, route, ys, plan, layer, split_out=layer == DEPTH - 1)

    if DEPTH % 2 == 0:
        y_prompt, y_sample = x
    else:
        y_prompt, y_sample = x[:N_PROMPT_ROWS], x[N_PROMPT_ROWS:]
    return (y_prompt.reshape(BATCH, SEQ, D_MODEL), y_sample.reshape(DEC_BATCH, DEC_SEQ, D_MODEL),
            jnp.stack(new_k, axis=1), jnp.stack(new_v, axis=1))
```

```python
import functools

import jax
import jax.numpy as jnp
import numpy as np
from jax import lax
from jax.experimental import pallas as pl
from jax.experimental.pallas import tpu as pltpu

F32 = jnp.float32
BF16 = jnp.bfloat16

D_MODEL = 1024
BATCH = 16
SEQ = 256
DEPTH = 4
DEC_BATCH = 2
DEC_SEQ = 2048
PAST_LEN = 256
GRID_W = 64
N_MIXERS = 3
N_HEADS = 8
N_KV_HEADS = 2
HEAD_DIM = 128
Q_PER_KV = N_HEADS // N_KV_HEADS
ROPE_THETA = 10000.0
CONV_W = 3
CHUNK = 128
GMLP_DIM = 2 * D_MODEL
GMLP_GROUPS = 8
GMLP_GROUP_DIM = GMLP_DIM // GMLP_GROUPS
D_FF = 2816
N_EXPERTS = 8
D_FF_EXPERT = 3584
EPS = 1e-6
QKV_OUT = (N_HEADS + 2 * N_KV_HEADS) * HEAD_DIM

N_PROMPT_ROWS = BATCH * SEQ
N_SAMPLE_ROWS = DEC_BATCH * DEC_SEQ
N_ROWS = N_PROMPT_ROWS + N_SAMPLE_ROWS
SUBLANES = 8
LANES = 128
N_COND_ROWS = SUBLANES

V7X_VMEM_BYTES = 64 * 1024 * 1024
VMEM_LIMIT_BYTES = V7X_VMEM_BYTES - 8 * 1024 * 1024
NORM_ROWS = 256


def _params(n_axes):
    return pltpu.CompilerParams(
        dimension_semantics=("arbitrary",) * n_axes,
        vmem_limit_bytes=VMEM_LIMIT_BYTES)


def _dot(a, b):
    return jnp.dot(a, b, preferred_element_type=F32)


def _cond_row(i, tm):
    n_prompt = N_PROMPT_ROWS // tm
    per_seq = DEC_SEQ // tm
    return jnp.where(i < n_prompt, 0, 1 + (i - n_prompt) // per_seq)


def _mod_spec(layer, tm):
    return pl.BlockSpec((None, None, 6, D_MODEL),
                        lambda i, *_: (layer, _cond_row(i, tm), 0, 0))


def _gain_spec(layer):
    return pl.BlockSpec((None, 1, D_MODEL), lambda *_: (layer, 0, 0))


def _row_parts(x):
    return tuple(x) if isinstance(x, (tuple, list)) else (x,)


def _row_specs(x, tm):
    parts = _row_parts(x)
    block = (tm, parts[0].shape[1])
    if len(parts) == 1:
        return [pl.BlockSpec(block, lambda i, *_: (i, 0))]
    n_prompt = N_PROMPT_ROWS // tm
    return [pl.BlockSpec(block, lambda i, *_: (jnp.minimum(i, n_prompt - 1), 0)),
            pl.BlockSpec(block, lambda i, *_: (jnp.maximum(i - n_prompt, 0), 0))]


def _row_loader(refs, tm):
    if len(refs) == 1:
        return lambda rows: refs[0][rows, :]
    is_prompt = pl.program_id(0) < N_PROMPT_ROWS // tm
    return lambda rows: jnp.where(is_prompt, refs[0][rows, :], refs[1][rows, :])


def _norm_mod_rows(load, n_rows, g_ref, scale, shift, h_ref):
    gs = g_ref[...] * (1.0 + scale)

    def body(r, carry):
        rows = pl.ds(pl.multiple_of(r * NORM_ROWS, NORM_ROWS), NORM_ROWS)
        x = load(rows)
        ms = jnp.mean(x * x, axis=-1, keepdims=True)
        h_ref[rows, :] = (x * lax.rsqrt(ms + EPS) * gs + shift).astype(h_ref.dtype)
        return carry

    lax.fori_loop(0, n_rows // NORM_ROWS, body, 0)


def _norm_mod_to(x_ref, g_ref, scale, shift, h_ref):
    _norm_mod_rows(lambda rows: x_ref[rows, :], x_ref.shape[0], g_ref, scale, shift, h_ref)


def _mod_kernel(c_ref, w_ref, b_ref, o_ref):
    cnd = c_ref[...]
    s = (cnd * jax.nn.sigmoid(cnd)).astype(BF16)
    o_ref[...] = _dot(s, w_ref[...].astype(BF16)) + b_ref[...]


def _modulation(cond, w_mod, b_mod):
    tn = 1536
    n_out = 6 * D_MODEL
    out = pl.pallas_call(
        _mod_kernel,
        out_shape=jax.ShapeDtypeStruct((DEPTH, N_COND_ROWS, n_out), F32),
        grid=(DEPTH, n_out // tn),
        in_specs=[
            pl.BlockSpec((N_COND_ROWS, D_MODEL), lambda l, j: (0, 0)),
            pl.BlockSpec((None, D_MODEL, tn), lambda l, j: (l, 0, j)),
            pl.BlockSpec((None, 1, tn), lambda l, j: (l, 0, j)),
        ],
        out_specs=pl.BlockSpec((None, N_COND_ROWS, tn), lambda l, j: (l, 0, j)),
        compiler_params=_params(2),
        name="modulation",
    )(cond, w_mod, b_mod.reshape(DEPTH, 1, n_out))
    return out.reshape(DEPTH, N_COND_ROWS, 6, D_MODEL)


CONV_TM = 2048
CONV_TN = 256


def _conv_in_kernel(*refs, n_x):
    x_refs = refs[:n_x]
    mod_ref, g_ref, wb_ref, wc_ref, wx_ref, cw_ref, o_ref, h_ref = refs[n_x:]
    i = pl.program_id(0)

    @pl.when(pl.program_id(1) == 0)
    def _():
        _norm_mod_rows(_row_loader(x_refs, CONV_TM), CONV_TM, g_ref, mod_ref[1:2, :],
                       mod_ref[0:1, :], h_ref)

    h = h_ref[...]
    b = _dot(h, wb_ref[...].astype(BF16))
    z = _dot(h, wc_ref[...].astype(BF16)) * _dot(h, wx_ref[...].astype(BF16))
    seq_len = jnp.where(i < N_PROMPT_ROWS // CONV_TM, SEQ, DEC_SEQ)
    pos = lax.broadcasted_iota(jnp.int32, z.shape, 0) & (seq_len - 1)
    z_prev = jnp.where(pos == 0, 0.0, pltpu.roll(z, 1, 0))
    z_next = jnp.where(pos == seq_len - 1, 0.0, pltpu.roll(z, CONV_TM - 1, 0))
    conv = z_prev * cw_ref[0:1, :] + z * cw_ref[1:2, :] + z_next * cw_ref[2:3, :]
    o_ref[...] = (b * conv).astype(o_ref.dtype)


def _conv_in(x, mod, g_mix, conv_w_in, conv_w, layer, j):
    tm, tn = CONV_TM, CONV_TN
    nb = D_MODEL // tn
    w_spec = lambda part: pl.BlockSpec((None, D_MODEL, tn),
                                       lambda i, n: (j, 0, part * nb + n))
    x_parts = _row_parts(x)
    return pl.pallas_call(
        functools.partial(_conv_in_kernel, n_x=len(x_parts)),
        out_shape=jax.ShapeDtypeStruct((N_ROWS, D_MODEL), BF16),
        grid=(N_ROWS // tm, nb),
        in_specs=_row_specs(x, tm) + [
            _mod_spec(layer, tm),
            _gain_spec(layer),
            w_spec(0), w_spec(1), w_spec(2),
            pl.BlockSpec((None, CONV_W, tn), lambda i, n: (j, 0, n)),
        ],
        out_specs=pl.BlockSpec((tm, tn), lambda i, n: (i, n)),
        scratch_shapes=[pltpu.VMEM((tm, D_MODEL), BF16)],
        compiler_params=_params(2),
        name="conv_in",
    )(*x_parts, mod, g_mix, conv_w_in, conv_w_in, conv_w_in, conv_w)


PROJ_TM = 1024


def _proj_res_kernel(*refs, n_a, n_x):
    a_refs, w_ref, x_refs = refs[:n_a], refs[n_a], refs[n_a + 1:n_a + 1 + n_x]
    mod_ref, o_ref = refs[n_a + 1 + n_x:]
    whole = slice(None)
    y = _dot(_row_loader(a_refs, PROJ_TM)(whole), w_ref[...].astype(BF16))
    o_ref[...] = _row_loader(x_refs, PROJ_TM)(whole) + mod_ref[2:3, :] * y


def _proj_res(a, w, j, x, mod, layer):
    tm = PROJ_TM
    a_parts, x_parts = _row_parts(a), _row_parts(x)
    k = a_parts[0].shape[1]
    return pl.pallas_call(
        functools.partial(_proj_res_kernel, n_a=len(a_parts), n_x=len(x_parts)),
        out_shape=jax.ShapeDtypeStruct((N_ROWS, D_MODEL), F32),
        grid=(N_ROWS // tm,),
        in_specs=(_row_specs(a, tm) + [pl.BlockSpec((None, k, D_MODEL), lambda i: (j, 0, 0))]
                  + _row_specs(x, tm) + [_mod_spec(layer, tm)]),
        out_specs=pl.BlockSpec((tm, D_MODEL), lambda i: (i, 0)),
        compiler_params=_params(1),
        name="proj_res",
    )(*a_parts, w, *x_parts, mod)


GMLP_TM = 1024
GMLP_TN = 1024
GMLP_OUT_TM = 512


def _gelu_tanh(x):
    c = 0.7978845608028654
    return 0.5 * x * (1.0 + jnp.tanh(c * (x + 0.044715 * (x * x * x))))


def _gmlp_in_kernel(x_ref, mod_ref, g_ref, w_ref, b_ref, o_ref, h_ref):
    @pl.when(pl.program_id(1) == 0)
    def _():
        _norm_mod_to(x_ref, g_ref, mod_ref[1:2, :], mod_ref[0:1, :], h_ref)

    z = _dot(h_ref[...], w_ref[...].astype(BF16)) + b_ref[...]
    o_ref[...] = _gelu_tanh(z).astype(o_ref.dtype)


def _gmlp_in(x, mod, g_mix, w_in, b_in, layer, j):
    tm, tn = GMLP_TM, GMLP_TN
    n_out = 2 * GMLP_DIM
    return pl.pallas_call(
        _gmlp_in_kernel,
        out_shape=jax.ShapeDtypeStruct((N_ROWS, n_out), BF16),
        grid=(N_ROWS // tm, n_out // tn),
        in_specs=[
            pl.BlockSpec((tm, D_MODEL), lambda i, n: (i, 0)),
            _mod_spec(layer, tm),
            _gain_spec(layer),
            pl.BlockSpec((None, D_MODEL, tn), lambda i, n: (j, 0, n)),
            pl.BlockSpec((None, 1, tn), lambda i, n: (j, 0, n)),
        ],
        out_specs=pl.BlockSpec((tm, tn), lambda i, n: (i, n)),
        scratch_shapes=[pltpu.VMEM((tm, D_MODEL), BF16)],
        compiler_params=_params(2),
        name="gmlp_in",
    )(x, mod, g_mix, w_in, b_in.reshape(-1, 1, n_out))


def _gmlp_out_kernel(u_ref, v_ref, lng_ref, lnb_ref, ws_ref, bs_ref, w_ref, x_ref, mod_ref,
                     o_ref, a_ref):
    lng = lng_ref[...]
    lnb = lnb_ref[...]

    def chunk(c, carry):
        rows = pl.ds(pl.multiple_of(c * CHUNK, CHUNK), CHUNK)
        v = v_ref[rows, :].astype(F32)
        mu = jnp.mean(v, axis=-1, keepdims=True)
        vc = v - mu
        var = jnp.mean(vc * vc, axis=-1, keepdims=True)
        vn = (vc * lax.rsqrt(var + EPS) * lng + lnb).astype(BF16)
        for g in range(GMLP_GROUPS):
            cols = slice(g * GMLP_GROUP_DIM, (g + 1) * GMLP_GROUP_DIM)
            mixed = _dot(ws_ref[g].astype(BF16), vn[:, cols]) + bs_ref[:, g:g + 1]
            a_ref[rows, cols] = (u_ref[rows, cols].astype(F32) * mixed).astype(a_ref.dtype)
        return carry

    lax.fori_loop(0, u_ref.shape[0] // CHUNK, chunk, 0)
    y = _dot(a_ref[...], w_ref[...].astype(BF16))
    o_ref[...] = x_ref[...] + mod_ref[2:3, :] * y


def _gmlp_out(z, ln_g, ln_b, w_s, b_s, w_out, x, mod, layer, j):
    tm = GMLP_OUT_TM
    return pl.pallas_call(
        _gmlp_out_kernel,
        out_shape=jax.ShapeDtypeStruct((N_ROWS, D_MODEL), F32),
        grid=(N_ROWS // tm,),
        in_specs=[
            pl.BlockSpec((tm, GMLP_DIM), lambda i: (i, 0)),
            pl.BlockSpec((tm, GMLP_DIM), lambda i: (i, 1)),
            pl.BlockSpec((None, 1, GMLP_DIM), lambda i: (j, 0, 0)),
            pl.BlockSpec((None, 1, GMLP_DIM), lambda i: (j, 0, 0)),
            pl.BlockSpec((None, GMLP_GROUPS, CHUNK, CHUNK), lambda i: (j, 0, 0, 0)),
            pl.BlockSpec((None, CHUNK, GMLP_GROUPS), lambda i: (j, 0, 0)),
            pl.BlockSpec((None, GMLP_DIM, D_MODEL), lambda i: (j, 0, 0)),
            pl.BlockSpec((tm, D_MODEL), lambda i: (i, 0)),
            _mod_spec(layer, tm),
        ],
        out_specs=pl.BlockSpec((tm, D_MODEL), lambda i: (i, 0)),
        scratch_shapes=[pltpu.VMEM((tm, GMLP_DIM), BF16)],
        compiler_params=_params(1),
        name="gmlp_out",
    )(z, z, ln_g.reshape(-1, 1, GMLP_DIM), ln_b.reshape(-1, 1, GMLP_DIM), w_s,
      jnp.swapaxes(b_s, 1, 2), w_out, x, mod)


QKV_TM = 1024
ATTN_TQ = 1024
ATTN_PROMPT_SEQS = 4


def _rope_tables():
    pos = np.arange(DEC_SEQ)
    row = (pos // GRID_W).astype(np.float32)
    col = (pos % GRID_W).astype(np.float32)
    n_freq = HEAD_DIM // 4
    inv_freq = np.power(np.float32(ROPE_THETA), -np.arange(n_freq, dtype=np.float32) / n_freq)
    ang = np.concatenate([row[:, None] * inv_freq, col[:, None] * inv_freq], axis=-1)
    cos = np.repeat(np.cos(ang), 2, axis=-1)
    sign = np.tile(np.array([-1.0, 1.0], np.float32), HEAD_DIM // 2)
    sin = np.repeat(np.sin(ang), 2, axis=-1) * sign
    return jnp.asarray(cos, F32), jnp.asarray(sin, F32)


def _swap_pairs(x):
    even = (lax.broadcasted_iota(jnp.int32, x.shape, 1) & 1) == 0
    return jnp.where(even, pltpu.roll(x, HEAD_DIM - 1, 1), pltpu.roll(x, 1, 1))


def _qkv_kernel(x_ref, mod_ref, g_ref, w_ref, qg_ref, kg_ref, cos_ref, sin_ref, o_ref,
                h_ref, wb_ref, wsw_ref):
    i = pl.program_id(0)
    n_rot_heads = N_HEADS + N_KV_HEADS
    head_cols = [slice(hd * HEAD_DIM, (hd + 1) * HEAD_DIM) for hd in range(n_rot_heads)]

    @pl.when(i == 0)
    def _():
        wb_ref[...] = w_ref[...].astype(BF16)
        for cols in head_cols:
            wsw_ref[:, cols] = _swap_pairs(w_ref[:, cols]).astype(BF16)

    _norm_mod_to(x_ref, g_ref, mod_ref[1:2, :], mod_ref[0:1, :], h_ref)
    y = _dot(h_ref[...], wb_ref[...])
    v_cols = slice(n_rot_heads * HEAD_DIM, QKV_OUT)
    o_ref[:, v_cols] = y[:, v_cols]

    def head_norm(cols):
        yh = y[:, cols]
        gain = qg_ref[...] if cols.start < N_HEADS * HEAD_DIM else kg_ref[...]
        return yh, gain, lax.rsqrt(jnp.mean(yh * yh, axis=-1, keepdims=True) + EPS)

    is_latent = i >= N_PROMPT_ROWS // x_ref.shape[0]

    @pl.when(jnp.logical_not(is_latent))
    def _():
        for cols in head_cols:
            yh, gain, inv = head_norm(cols)
            o_ref[:, cols] = yh * gain * inv

    @pl.when(is_latent)
    def _():
        y_sw = _dot(h_ref[...], wsw_ref[...])
        cos = cos_ref[...]
        sin = sin_ref[...]
        for cols in head_cols:
            yh, gain, inv = head_norm(cols)
            o_ref[:, cols] = (yh * (gain * cos) + y_sw[:, cols] * (_swap_pairs(gain) * sin)) * inv


def _qkv(x, mod, g_mix, w_qkv, q_g, k_g, layer, j):
    tm = QKV_TM
    cos, sin = _rope_tables()
    n_prompt = N_PROMPT_ROWS // tm
    per_seq = DEC_SEQ // tm
    tbl = lambda i: (jnp.where(i < n_prompt, 0, (i - n_prompt) % per_seq), 0)
    return pl.pallas_call(
        _qkv_kernel,
        out_shape=jax.ShapeDtypeStruct((N_ROWS, QKV_OUT), F32),
        grid=(N_ROWS // tm,),
        in_specs=[
            pl.BlockSpec((tm, D_MODEL), lambda i: (i, 0)),
            _mod_spec(layer, tm),
            _gain_spec(layer),
            pl.BlockSpec((None, D_MODEL, QKV_OUT), lambda i: (j, 0, 0)),
            pl.BlockSpec((None, 1, HEAD_DIM), lambda i: (j, 0, 0)),
            pl.BlockSpec((None, 1, HEAD_DIM), lambda i: (j, 0, 0)),
            pl.BlockSpec((tm, HEAD_DIM), tbl),
            pl.BlockSpec((tm, HEAD_DIM), tbl),
        ],
        out_specs=pl.BlockSpec((tm, QKV_OUT), lambda i: (i, 0)),
        scratch_shapes=[pltpu.VMEM((tm, D_MODEL), BF16),
                        pltpu.VMEM((D_MODEL, QKV_OUT), BF16),
                        pltpu.VMEM((D_MODEL, (N_HEADS + N_KV_HEADS) * HEAD_DIM), BF16)],
        compiler_params=_params(1),
        name="qkv",
    )(x, mod, g_mix, w_qkv, q_g.reshape(-1, 1, HEAD_DIM), k_g.reshape(-1, 1, HEAD_DIM), cos, sin)


def _attn_kernel(*refs, has_ctx, n_seq):
    if has_ctx:
        q_ref, k_ref, v_ref, ck_ref, cv_ref, o_ref = refs
    else:
        q_ref, k_ref, v_ref, o_ref = refs
    nt = (((1,), (1,)), ((), ()))
    if has_ctx:
        ck = ck_ref[...].astype(BF16)
        cv = cv_ref[...].astype(BF16)
    q_len = q_ref.shape[0] // n_seq
    k_len = k_ref.shape[0] // n_seq
    for sq in range(n_seq):
        q_rows = slice(sq * q_len, (sq + 1) * q_len)
        k_rows = slice(sq * k_len, (sq + 1) * k_len)
        k = k_ref[k_rows, :].astype(BF16)
        v = v_ref[k_rows, :].astype(BF16)
        for g in range(Q_PER_KV):
            cols = slice(g * HEAD_DIM, (g + 1) * HEAD_DIM)
            q = (q_ref[q_rows, cols] * (HEAD_DIM ** -0.5)).astype(BF16)
            s = lax.dot_general(q, k, nt, preferred_element_type=F32)
            m = jnp.max(s, axis=-1, keepdims=True)
            if has_ctx:
                sc = lax.dot_general(q, ck, nt, preferred_element_type=F32)
                m = jnp.maximum(m, jnp.max(sc, axis=-1, keepdims=True))
            p = jnp.exp(s - m)
            den = jnp.sum(p, axis=-1, keepdims=True)
            o = _dot(p.astype(BF16), v)
            if has_ctx:
                pc = jnp.exp(sc - m)
                den = den + jnp.sum(pc, axis=-1, keepdims=True)
                o = o + _dot(pc.astype(BF16), cv)
            o_ref[q_rows, cols] = (o / den).astype(o_ref.dtype)


def _attention(qkv, ctx_k, ctx_v):
    group_w = Q_PER_KV * HEAD_DIM
    k_col = N_HEADS
    v_col = N_HEADS + N_KV_HEADS

    def call(n_batch, seq, tq, row0, ctx, n_seq):
        q_blocks = seq // tq
        q_rows, kv_rows = n_seq * tq, n_seq * seq
        o_row = lambda b, qi: b * q_blocks + qi
        kv_row = lambda b: row0 // kv_rows + b
        in_specs = [
            pl.BlockSpec((q_rows, group_w), lambda b, kh, qi: (row0 // q_rows + o_row(b, qi), kh)),
            pl.BlockSpec((kv_rows, HEAD_DIM), lambda b, kh, qi: (kv_row(b), k_col + kh)),
            pl.BlockSpec((kv_rows, HEAD_DIM), lambda b, kh, qi: (kv_row(b), v_col + kh)),
        ]
        args = [qkv, qkv, qkv]
        if ctx is not None:
            ctx_spec = pl.BlockSpec((None, PAST_LEN, HEAD_DIM), lambda b, kh, qi: (b, 0, kh))
            in_specs += [ctx_spec, ctx_spec]
            args += list(ctx)
        return pl.pallas_call(
            functools.partial(_attn_kernel, has_ctx=ctx is not None, n_seq=n_seq),
            out_shape=jax.ShapeDtypeStruct((n_batch * seq, N_HEADS * HEAD_DIM), BF16),
            grid=(n_batch // n_seq, N_KV_HEADS, q_blocks),
            in_specs=in_specs,
            out_specs=pl.BlockSpec((q_rows, group_w), lambda b, kh, qi: (o_row(b, qi), kh)),
            compiler_params=_params(3),
            name="attention",
        )(*args)

    o_prompt = call(BATCH, SEQ, SEQ, 0, None, ATTN_PROMPT_SEQS)
    o_sample = call(DEC_BATCH, DEC_SEQ, ATTN_TQ, N_PROMPT_ROWS, (ctx_k, ctx_v), 1)
    return o_prompt, o_sample


FFN_TM = 1024
FFN_TF = 256
TOP_K = 2
ROUTER_TM = 1024
ROUTE_LANES = LANES
MOE_TF = 512
MOE_TILE = 1024
MOE_SUB = 128
MOE_MAX_TILES = TOP_K * N_ROWS // MOE_TILE + N_EXPERTS
MOE_ROWS = MOE_MAX_TILES * MOE_TILE
SCATTER_TM = 512
COMBINE_TM = 256
WIN_CHUNK = 128
WIN_CHUNKS = COMBINE_TM // WIN_CHUNK + 1
WIN_SLOTS = TOP_K * COMBINE_TM // WIN_CHUNK + 2 * N_EXPERTS
WIN_SLOT_STEPS = (12, 16, WIN_SLOTS)


def _swiglu_hidden(h, wg, wu):
    gate = _dot(h, wg)
    return (gate * jax.nn.sigmoid(gate) * _dot(h, wu)).astype(BF16)


def _ffn_kernel(x_ref, mod_ref, g_ref, wg_ref, wu_ref, wd_ref, o_ref, h_ref):
    f = pl.program_id(1)

    @pl.when(f == 0)
    def _():
        _norm_mod_to(x_ref, g_ref, mod_ref[4:5, :], mod_ref[3:4, :], h_ref)
        o_ref[...] = jnp.zeros_like(o_ref)

    a = _swiglu_hidden(h_ref[...], wg_ref[...].astype(BF16), wu_ref[...].astype(BF16))
    o_ref[...] += _dot(a, wd_ref[...].astype(BF16))

    @pl.when(f == pl.num_programs(1) - 1)
    def _():
        o_ref[...] = x_ref[...] + mod_ref[5:6, :] * o_ref[...]


def _ffn(x, mod, g_ffn, w_gate, w_up, w_down, layer, li):
    tm, tf = FFN_TM, FFN_TF
    return pl.pallas_call(
        _ffn_kernel,
        out_shape=jax.ShapeDtypeStruct((N_ROWS, D_MODEL), F32),
        grid=(N_ROWS // tm, D_FF // tf),
        in_specs=[
            pl.BlockSpec((tm, D_MODEL), lambda i, f: (i, 0)),
            _mod_spec(layer, tm),
            _gain_spec(layer),
            pl.BlockSpec((None, D_MODEL, tf), lambda i, f: (li, 0, f)),
            pl.BlockSpec((None, D_MODEL, tf), lambda i, f: (li, 0, f)),
            pl.BlockSpec((None, tf, D_MODEL), lambda i, f: (li, f, 0)),
        ],
        out_specs=pl.BlockSpec((tm, D_MODEL), lambda i, f: (i, 0)),
        scratch_shapes=[pltpu.VMEM((tm, D_MODEL), BF16)],
        compiler_params=_params(2),
        name="ffn",
    )(x, mod, g_ffn, w_gate, w_up, w_down)


def _split_bf16(x):
    hi = x.astype(BF16)
    return hi, (x - hi.astype(F32)).astype(BF16)


def _router_kernel(x_ref, mod_ref, g_ref, w_ref, route_ref, route_t_ref, cnt_ref, base_ref,
                   h_ref, seen_ref):
    @pl.when(pl.program_id(0) == 0)
    def _():
        seen_ref[...] = jnp.zeros_like(seen_ref)

    _norm_mod_to(x_ref, g_ref, mod_ref[4:5, :], mod_ref[3:4, :], h_ref)
    h_hi, h_lo = _split_bf16(h_ref[...])
    w_hi, w_lo = _split_bf16(w_ref[...])
    logits = _dot(h_hi, w_hi) + (_dot(h_hi, w_lo) + _dot(h_lo, w_hi))
    lane = lax.broadcasted_iota(jnp.int32, logits.shape, 1)
    logits = jnp.where(lane < N_EXPERTS, logits, -jnp.inf)
    m1 = jnp.max(logits, axis=-1, keepdims=True)
    i1 = jnp.min(jnp.where(logits == m1, lane, N_EXPERTS), axis=-1, keepdims=True)
    rest = jnp.where(lane == i1, -jnp.inf, logits)
    m2 = jnp.max(rest, axis=-1, keepdims=True)
    i2 = jnp.min(jnp.where(rest == m2, lane, N_EXPERTS), axis=-1, keepdims=True)
    e2 = jnp.exp(m2 - m1)
    w1 = 1.0 / (1.0 + e2)
    chosen = jnp.where((lane == i1) | (lane == i2), 1.0, 0.0)
    tm = chosen.shape[0]
    blk = COMBINE_TM
    earlier = (lax.broadcasted_iota(jnp.int32, (blk, blk), 1)
               < lax.broadcasted_iota(jnp.int32, (blk, blk), 0))
    earlier = jnp.where(earlier, 1.0, 0.0).astype(BF16)
    base_ref[...] = jnp.zeros_like(base_ref)
    seen = seen_ref[...]
    priors = []
    for k in range(tm // blk):
        rows = slice(k * blk, (k + 1) * blk)
        base_ref[k:k + 1, :] = seen
        priors.append(_dot(earlier, chosen[rows].astype(BF16)) + seen)
        seen = seen + jnp.sum(chosen[rows], axis=0, keepdims=True)
    seen_ref[...] = seen
    prior = jnp.concatenate(priors, axis=0)
    rank1 = jnp.sum(jnp.where(lane == i1, prior, 0.0), axis=-1, keepdims=True)
    rank2 = jnp.sum(jnp.where(lane == i2, prior, 0.0), axis=-1, keepdims=True)
    route = jnp.zeros_like(logits)
    for k, field in enumerate((w1, e2 * w1, i1.astype(F32), i2.astype(F32), rank1, rank2)):
        route = jnp.where(lane == k, field, route)
    route_ref[...] = route
    route_t_ref[...] = jnp.transpose(route)[:SUBLANES, :]
    cnt_ref[...] = jnp.broadcast_to(seen, cnt_ref.shape)


def _router(x, mod, g_ffn, w_router, layer, li):
    tm = ROUTER_TM
    w_pad = jnp.pad(w_router, ((0, 0), (0, 0), (0, ROUTE_LANES - N_EXPERTS)))
    return pl.pallas_call(
        _router_kernel,
        out_shape=(jax.ShapeDtypeStruct((N_ROWS, ROUTE_LANES), F32),
                   jax.ShapeDtypeStruct((SUBLANES, N_ROWS), F32),
                   jax.ShapeDtypeStruct((SUBLANES, ROUTE_LANES), F32),
                   jax.ShapeDtypeStruct((N_ROWS // tm, SUBLANES, ROUTE_LANES), F32)),
        grid=(N_ROWS // tm,),
        in_specs=[
            pl.BlockSpec((tm, D_MODEL), lambda i: (i, 0)),
            _mod_spec(layer, tm),
            _gain_spec(layer),
            pl.BlockSpec((None, D_MODEL, ROUTE_LANES), lambda i: (li, 0, 0)),
        ],
        out_specs=(pl.BlockSpec((tm, ROUTE_LANES), lambda i: (i, 0)),
                   pl.BlockSpec((SUBLANES, tm), lambda i: (0, i)),
                   pl.BlockSpec((SUBLANES, ROUTE_LANES), lambda i: (0, 0)),
                   pl.BlockSpec((None, SUBLANES, ROUTE_LANES), lambda i: (i, 0, 0))),
        scratch_shapes=[pltpu.VMEM((tm, D_MODEL), F32), pltpu.VMEM((1, ROUTE_LANES), F32)],
        compiler_params=_params(1),
        name="router",
    )(x, mod, g_ffn, w_pad)


def _dispatch_plan(route_t, cnt, tile_base):
    counts = cnt[0, :N_EXPERTS].astype(jnp.int32)
    e1, e2, r1, r2 = (route_t[k].astype(jnp.int32) for k in (2, 3, 4, 5))
    n_tiles = (counts + MOE_TILE - 1) // MOE_TILE
    tile_end = jnp.cumsum(n_tiles)
    tile_start = tile_end - n_tiles
    row_start = tile_start * MOE_TILE
    pos = jnp.concatenate([row_start[e1] + r1, row_start[e2] + r2])
    n_active = tile_end[-1]
    t = jnp.arange(MOE_MAX_TILES, dtype=jnp.int32)
    tile_block = jnp.minimum(t, n_active - 1)
    tile_expert = jnp.sum(tile_block[:, None] >= tile_end[None, :], axis=1).astype(jnp.int32)
    valid = counts[tile_expert] - (tile_block - tile_start[tile_expert]) * MOE_TILE
    n_sub = (jnp.clip(valid, 0, MOE_TILE) + MOE_SUB - 1) // MOE_SUB
    tile_nsub = jnp.where(t < n_active, n_sub, 0).astype(jnp.int32)
    pad = jnp.concatenate([row_start + counts, (-counts) % MOE_SUB]).astype(jnp.int32)
    base = tile_base[:, :ROUTER_TM // COMBINE_TM, :N_EXPERTS].astype(jnp.int32)
    base = base.reshape(N_ROWS // COMBINE_TM, N_EXPERTS)
    lo = row_start[None, :] + base
    hi = row_start[None, :] + jnp.concatenate([base[1:], counts[None, :]], axis=0)
    chunk0 = lo // WIN_CHUNK
    n_chunks = jnp.where(hi > lo, (hi + WIN_CHUNK - 1) // WIN_CHUNK - chunk0, 0)
    j = jnp.arange(WIN_CHUNKS, dtype=jnp.int32)
    needed = (j[None, None, :] < n_chunks[:, :, None]).reshape(-1, N_EXPERTS * WIN_CHUNKS)
    chunk = (chunk0[:, :, None] + j[None, None, :]).reshape(-1, N_EXPERTS * WIN_CHUNKS)
    order = jnp.argsort(jnp.logical_not(needed), axis=1, stable=True)[:, :WIN_SLOTS]
    win_count = jnp.sum(needed, axis=1).astype(jnp.int32)
    win_list = jnp.where(jnp.arange(WIN_SLOTS)[None, :] < win_count[:, None],
                         jnp.take_along_axis(chunk, order, axis=1), -1)
    return dict(pos=pos, pad=pad, tile_expert=tile_expert, tile_nsub=tile_nsub,
                tile_block=tile_block, row_start=row_start.astype(jnp.int32),
                win_list=win_list.reshape(-1).astype(jnp.int32), win_count=win_count)


def _for_rows(n_rows, body):
    def group(j, carry):
        r0 = pl.multiple_of(j * SUBLANES, SUBLANES)
        for u in range(SUBLANES):
            body(r0 + u)
        return carry
    lax.fori_loop(0, n_rows // SUBLANES, group, 0)


def _scatter_kernel(pos_ref, pad_ref, tn_ref, x_ref, mod_ref, g_ref, xs_ref, h_ref, zero_ref,
                    sem, fill_sem):
    i = pl.program_id(0)
    ts = x_ref.shape[0]
    slot = lax.rem(i, 2)
    _norm_mod_to(x_ref, g_ref, mod_ref[4:5, :], mod_ref[3:4, :], h_ref.at[slot])

    def for_row_copies(step, step_slot, act):
        first = step * ts

        def row(r):
            for k in range(TOP_K):
                dst = pos_ref[k * N_ROWS + first + r]
                act(pltpu.make_async_copy(h_ref.at[step_slot, pl.ds(r, 1)],
                                          xs_ref.at[pl.ds(dst, 1)], sem.at[step_slot]), k)
        _for_rows(ts, row)

    for_row_copies(i, slot, lambda cp, k: cp.start(priority=k))

    @pl.when(i == 0)
    def _():
        zero_ref[...] = jnp.zeros_like(zero_ref)

        def fills(act):
            for e in range(N_EXPERTS):
                def row_body(r, carry, e=e):
                    act(pltpu.make_async_copy(zero_ref.at[pl.ds(0, 1)],
                                              xs_ref.at[pl.ds(pad_ref[e] + r, 1)], fill_sem))
                    return carry
                lax.fori_loop(0, pad_ref[N_EXPERTS + e], row_body, 0)

            def tile_body(t, carry):
                def sub_body(s, carry):
                    row = pl.multiple_of(t * MOE_TILE + s * MOE_SUB, MOE_SUB)
                    act(pltpu.make_async_copy(zero_ref, xs_ref.at[pl.ds(row, MOE_SUB)], fill_sem))
                    return carry
                return lax.fori_loop(tn_ref[t], MOE_TILE // MOE_SUB, sub_body, carry)
            lax.fori_loop(0, MOE_MAX_TILES, tile_body, 0)

        fills(lambda cp: cp.start())
        fills(lambda cp: cp.wait())

    @pl.when(i > 0)
    def _():
        for_row_copies(i - 1, 1 - slot, lambda cp, k: cp.wait())

    @pl.when(i == pl.num_programs(0) - 1)
    def _():
        for_row_copies(i, slot, lambda cp, k: cp.wait())


def _moe_scatter(x, mod, g_ffn, plan, layer):
    ts = SCATTER_TM
    grid_spec = pltpu.PrefetchScalarGridSpec(
        num_scalar_prefetch=3,
        grid=(N_ROWS // ts,),
        in_specs=[
            pl.BlockSpec((ts, D_MODEL), lambda i, *_: (i, 0)),
            _mod_spec(layer, ts),
            _gain_spec(layer),
        ],
        out_specs=pl.BlockSpec(memory_space=pl.ANY),
        scratch_shapes=[pltpu.VMEM((2, ts, D_MODEL), F32),
                        pltpu.VMEM((MOE_SUB, D_MODEL), F32),
                        pltpu.SemaphoreType.DMA((2,)), pltpu.SemaphoreType.DMA(())],
    )
    return pl.pallas_call(
        _scatter_kernel,
        out_shape=jax.ShapeDtypeStruct((MOE_ROWS, D_MODEL), F32),
        grid_spec=grid_spec,
        compiler_params=_params(1),
        name="moe_scatter",
    )(plan["pos"], plan["pad"], plan["tile_nsub"], x, mod, g_ffn)


def _moe_kernel(te_ref, tn_ref, tb_ref, x_ref, wg_ref, wu_ref, wd_ref, o_ref,
                xb_ref, acc_ref):
    n_sub = tn_ref[pl.program_id(0)]
    first_f = pl.program_id(1) == 0

    @pl.when(first_f)
    def _():
        acc_ref[...] = jnp.zeros_like(acc_ref)

    for n in range(1, MOE_TILE // MOE_SUB + 1):
        @pl.when(n_sub == n)
        def _(n=n):
            rows = slice(0, n * MOE_SUB)

            @pl.when(first_f)
            def _():
                xb_ref[rows, :] = x_ref[rows, :].astype(BF16)

            a = _swiglu_hidden(xb_ref[rows, :], wg_ref[...].astype(BF16),
                               wu_ref[...].astype(BF16))
            acc_ref[rows, :] += _dot(a, wd_ref[...].astype(BF16))

    @pl.when(pl.program_id(1) == pl.num_programs(1) - 1)
    def _():
        o_ref[...] = acc_ref[...].astype(o_ref.dtype)


def _moe_experts(xs, plan, w_gate, w_up, w_down, li):
    tile, tf = MOE_TILE, MOE_TF
    nf = D_FF_EXPERT // tf
    frozen_f = lambda i, f, tn: jnp.where(tn[i] > 0, f, nf - 1)
    grid_spec = pltpu.PrefetchScalarGridSpec(
        num_scalar_prefetch=3,
        grid=(MOE_MAX_TILES, nf),
        in_specs=[
            pl.BlockSpec((tile, D_MODEL), lambda i, f, te, tn, tb: (tb[i], 0)),
            pl.BlockSpec((None, None, D_MODEL, tf),
                         lambda i, f, te, tn, tb: (li, te[i], 0, frozen_f(i, f, tn))),
            pl.BlockSpec((None, None, D_MODEL, tf),
                         lambda i, f, te, tn, tb: (li, te[i], 0, frozen_f(i, f, tn))),
            pl.BlockSpec((None, None, tf, D_MODEL),
                         lambda i, f, te, tn, tb: (li, te[i], frozen_f(i, f, tn), 0)),
        ],
        out_specs=pl.BlockSpec((tile, D_MODEL), lambda i, f, te, tn, tb: (i, 0)),
        scratch_shapes=[pltpu.VMEM((tile, D_MODEL), BF16), pltpu.VMEM((tile, D_MODEL), F32)],
    )
    return pl.pallas_call(
        _moe_kernel,
        out_shape=jax.ShapeDtypeStruct((MOE_ROWS, D_MODEL), BF16),
        grid_spec=grid_spec,
        compiler_params=_params(2),
        name="moe_experts",
    )(plan["tile_expert"], plan["tile_nsub"], plan["tile_block"], xs, w_gate, w_up, w_down)


def _combine_kernel(start_ref, list_ref, n_ref, x_ref, mod_ref, route_ref, ys_ref, *refs):
    *o_refs, win_ref, sem = refs
    i = pl.program_id(0)
    slot = lax.rem(i, 2)

    def window_copies(tile, tile_slot, act):
        def body(s, carry):
            chunk = list_ref[tile * WIN_SLOTS + s]
            src = ys_ref.at[pl.ds(pl.multiple_of(chunk * WIN_CHUNK, WIN_CHUNK), WIN_CHUNK)]
            dst = win_ref.at[tile_slot,
                             pl.ds(pl.multiple_of(s * WIN_CHUNK, WIN_CHUNK), WIN_CHUNK)]
            act(pltpu.make_async_copy(src, dst, sem.at[tile_slot]))
            return carry
        lax.fori_loop(0, n_ref[tile], body, 0)

    @pl.when(i == 0)
    def _():
        win_ref[...] = jnp.zeros_like(win_ref)
        window_copies(i, slot, lambda cp: cp.start())

    @pl.when(i + 1 < pl.num_programs(0))
    def _():
        window_copies(i + 1, 1 - slot, lambda cp: cp.start())

    window_copies(i, slot, lambda cp: cp.wait())

    w1, w2, e1, e2, pos1, pos2 = (route_ref[:, k:k + 1] for k in range(6))
    for e in range(N_EXPERTS):
        first_row = start_ref[e].astype(F32)
        pos1 = pos1 + jnp.where(e1 == e, first_row, 0.0)
        pos2 = pos2 + jnp.where(e2 == e, first_row, 0.0)
    lane = lax.broadcasted_iota(jnp.int32, (x_ref.shape[0], WIN_CHUNK), 1).astype(F32)

    def combine(n_slots):
        blocks = []
        for s in range(n_slots):
            rows = lane + (list_ref[i * WIN_SLOTS + s] * WIN_CHUNK).astype(F32)
            weight = jnp.where(pos1 == rows, w1, 0.0) + jnp.where(pos2 == rows, w2, 0.0)
            blocks.append(weight.astype(BF16))
        y = _dot(jnp.concatenate(blocks, axis=1), win_ref[slot, :n_slots * WIN_CHUNK, :])
        out = x_ref[...] + mod_ref[5:6, :] * y
        if len(o_refs) == 1:
            o_refs[0][...] = out
        else:
            is_prompt = i < N_PROMPT_ROWS // x_ref.shape[0]

            @pl.when(is_prompt)
            def _():
                o_refs[0][...] = out

            @pl.when(jnp.logical_not(is_prompt))
            def _():
                o_refs[1][...] = out

    n_chunks = n_ref[i]
    smaller = -1
    for n_slots in WIN_SLOT_STEPS:
        @pl.when((n_chunks > smaller) & (n_chunks <= n_slots))
        def _(n_slots=n_slots):
            combine(n_slots)
        smaller = n_slots


def _moe_combine(x, mod, route, ys, plan, layer, split_out):
    tc = COMBINE_TM
    if split_out:
        half = jax.ShapeDtypeStruct((N_PROMPT_ROWS, D_MODEL), F32)
        out_shape = (half, jax.ShapeDtypeStruct((N_SAMPLE_ROWS, D_MODEL), F32))
        out_specs = tuple(_row_specs((half, half), tc))
    else:
        out_shape = jax.ShapeDtypeStruct((N_ROWS, D_MODEL), F32)
        out_specs = pl.BlockSpec((tc, D_MODEL), lambda i, *_: (i, 0))
    grid_spec = pltpu.PrefetchScalarGridSpec(
        num_scalar_prefetch=3,
        grid=(N_ROWS // tc,),
        in_specs=[
            pl.BlockSpec((tc, D_MODEL), lambda i, *_: (i, 0)),
            _mod_spec(layer, tc),
            pl.BlockSpec((tc, ROUTE_LANES), lambda i, *_: (i, 0)),
            pl.BlockSpec(memory_space=pl.ANY),
        ],
        out_specs=out_specs,
        scratch_shapes=[pltpu.VMEM((2, WIN_SLOTS * WIN_CHUNK, D_MODEL), BF16),
                        pltpu.SemaphoreType.DMA((2,))],
    )
    return pl.pallas_call(
        _combine_kernel,
        out_shape=out_shape,
        grid_spec=grid_spec,
        compiler_params=_params(1),
        name="moe_combine",
    )(plan["row_start"], plan["win_list"], plan["win_count"], x, mod, route, ys)


def kernel(x_prompt, x_sample, cache_k, cache_v, c, c_ctx, w_mod, b_mod, g_mix, g_ffn,
           conv_w_in, conv_w, conv_w_out,
           gmlp_w_in, gmlp_b_in, gmlp_ln_g, gmlp_ln_b, gmlp_w_s, gmlp_b_s, gmlp_w_out,
           attn_w_qkv, attn_q_g, attn_k_g, attn_w_o,
           ffn_w_gate, ffn_w_up, ffn_w_down,
           moe_w_router, moe_w_gate, moe_w_up, moe_w_down):
    x = (x_prompt.reshape(N_PROMPT_ROWS, D_MODEL), x_sample.reshape(N_SAMPLE_ROWS, D_MODEL))
    cond = jnp.zeros((N_COND_ROWS, D_MODEL), F32).at[0].set(c_ctx).at[1:1 + DEC_BATCH].set(c)
    mod = _modulation(cond, w_mod, b_mod)
    g_mix3 = g_mix.reshape(DEPTH, 1, D_MODEL)
    g_ffn3 = g_ffn.reshape(DEPTH, 1, D_MODEL)
    n_attn = cache_k.shape[1]
    ctx_k = cache_k.reshape(DEC_BATCH, n_attn, PAST_LEN, N_KV_HEADS * HEAD_DIM)
    ctx_v = cache_v.reshape(DEC_BATCH, n_attn, PAST_LEN, N_KV_HEADS * HEAD_DIM)

    new_k, new_v = [], []
    for layer in range(DEPTH):
        kind, j = layer % N_MIXERS, layer // N_MIXERS
        if kind == 0:
            a = _conv_in(x, mod, g_mix3, conv_w_in, conv_w, layer, j)
            x = _proj_res(a, conv_w_out, j, x, mod, layer)
        elif kind == 1:
            z = _gmlp_in(x, mod, g_mix3, gmlp_w_in, gmlp_b_in, layer, j)
            x = _gmlp_out(z, gmlp_ln_g, gmlp_ln_b, gmlp_w_s, gmlp_b_s, gmlp_w_out, x, mod,
                          layer, j)
        else:
            qkv = _qkv(x, mod, g_mix3, attn_w_qkv, attn_q_g, attn_k_g, layer, j)
            k_cols = slice(N_HEADS * HEAD_DIM, (N_HEADS + N_KV_HEADS) * HEAD_DIM)
            v_cols = slice((N_HEADS + N_KV_HEADS) * HEAD_DIM, QKV_OUT)
            new_k.append(qkv[:N_PROMPT_ROWS, k_cols].reshape(BATCH, SEQ, N_KV_HEADS, HEAD_DIM))
            new_v.append(qkv[:N_PROMPT_ROWS, v_cols].reshape(BATCH, SEQ, N_KV_HEADS, HEAD_DIM))
            a = _attention(qkv, ctx_k[:, j], ctx_v[:, j])
            x = _proj_res(a, attn_w_o, j, x, mod, layer)
        li = layer // 2
        if layer % 2 == 0:
            x = _ffn(x, mod, g_ffn3, ffn_w_gate, ffn_w_up, ffn_w_down, layer, li)
        else:
            route, route_t, cnt, tile_base = _router(x, mod, g_ffn3, moe_w_router, layer, li)
            plan = _dispatch_plan(route_t, cnt, tile_base)
            xs = _moe_scatter(x, mod, g_ffn3, plan, layer)
            ys = _moe_experts(xs, plan, moe_w_gate, moe_w_up, moe_w_down, li)
            x = _moe_combine(x, mod, route, ys, plan, layer, split_out=layer == DEPTH - 1)

    if DEPTH % 2 == 0:
        y_prompt, y_sample = x
    else:
        y_prompt, y_sample = x[:N_PROMPT_ROWS], x[N_PROMPT_ROWS:]
    return (y_prompt.reshape(BATCH, SEQ, D_MODEL), y_sample.reshape(DEC_BATCH, DEC_SEQ, D_MODEL),
            jnp.stack(new_k, axis=1), jnp.stack(new_v, axis=1))
```

```python
import functools

import jax
import jax.numpy as jnp
import numpy as np
from jax import lax
from jax.experimental import pallas as pl
from jax.experimental.pallas import tpu as pltpu

F32 = jnp.float32
BF16 = jnp.bfloat16

D_MODEL = 1024
BATCH = 16
SEQ = 256
DEPTH = 4
DEC_BATCH = 2
DEC_SEQ = 2048
PAST_LEN = 256
GRID_W = 64
N_MIXERS = 3
N_HEADS = 8
N_KV_HEADS = 2
HEAD_DIM = 128
Q_PER_KV = N_HEADS // N_KV_HEADS
ROPE_THETA = 10000.0
CONV_W = 3
CHUNK = 128
GMLP_DIM = 2 * D_MODEL
GMLP_GROUPS = 8
GMLP_GROUP_DIM = GMLP_DIM // GMLP_GROUPS
D_FF = 2816
N_EXPERTS = 8
D_FF_EXPERT = 3584
EPS = 1e-6
QKV_OUT = (N_HEADS + 2 * N_KV_HEADS) * HEAD_DIM

N_PROMPT_ROWS = BATCH * SEQ
N_SAMPLE_ROWS = DEC_BATCH * DEC_SEQ
N_ROWS = N_PROMPT_ROWS + N_SAMPLE_ROWS
SUBLANES = 8
LANES = 128
N_COND_ROWS = SUBLANES

V7X_VMEM_BYTES = 64 * 1024 * 1024
VMEM_LIMIT_BYTES = V7X_VMEM_BYTES - 8 * 1024 * 1024
NORM_ROWS = 256


def _params(n_axes):
    return pltpu.CompilerParams(
        dimension_semantics=("arbitrary",) * n_axes,
        vmem_limit_bytes=VMEM_LIMIT_BYTES)


def _dot(a, b):
    return jnp.dot(a, b, preferred_element_type=F32)


def _cond_row(i, tm):
    n_prompt = N_PROMPT_ROWS // tm
    per_seq = DEC_SEQ // tm
    return jnp.where(i < n_prompt, 0, 1 + (i - n_prompt) // per_seq)


def _mod_spec(layer, tm):
    return pl.BlockSpec((None, None, 6, D_MODEL),
                        lambda i, *_: (layer, _cond_row(i, tm), 0, 0))


def _gain_spec(layer):
    return pl.BlockSpec((None, 1, D_MODEL), lambda *_: (layer, 0, 0))


def _row_parts(x):
    return tuple(x) if isinstance(x, (tuple, list)) else (x,)


def _row_specs(x, tm):
    parts = _row_parts(x)
    block = (tm, parts[0].shape[1])
    if len(parts) == 1:
        return [pl.BlockSpec(block, lambda i, *_: (i, 0))]
    n_prompt = N_PROMPT_ROWS // tm
    return [pl.BlockSpec(block, lambda i, *_: (jnp.minimum(i, n_prompt - 1), 0)),
            pl.BlockSpec(block, lambda i, *_: (jnp.maximum(i - n_prompt, 0), 0))]


def _row_loader(refs, tm):
    if len(refs) == 1:
        return lambda rows: refs[0][rows, :]
    is_prompt = pl.program_id(0) < N_PROMPT_ROWS // tm
    return lambda rows: jnp.where(is_prompt, refs[0][rows, :], refs[1][rows, :])


def _norm_mod_rows(load, n_rows, g_ref, scale, shift, h_ref):
    gs = g_ref[...] * (1.0 + scale)

    def body(r, carry):
        rows = pl.ds(pl.multiple_of(r * NORM_ROWS, NORM_ROWS), NORM_ROWS)
        x = load(rows)
        ms = jnp.mean(x * x, axis=-1, keepdims=True)
        h_ref[rows, :] = (x * lax.rsqrt(ms + EPS) * gs + shift).astype(h_ref.dtype)
        return carry

    lax.fori_loop(0, n_rows // NORM_ROWS, body, 0)


def _norm_mod_to(x_ref, g_ref, scale, shift, h_ref):
    _norm_mod_rows(lambda rows: x_ref[rows, :], x_ref.shape[0], g_ref, scale, shift, h_ref)


def _mod_kernel(c_ref, w_ref, b_ref, o_ref):
    cnd = c_ref[...]
    s = (cnd * jax.nn.sigmoid(cnd)).astype(BF16)
    o_ref[...] = _dot(s, w_ref[...].astype(BF16)) + b_ref[...]


def _modulation(cond, w_mod, b_mod):
    tn = 1536
    n_out = 6 * D_MODEL
    out = pl.pallas_call(
        _mod_kernel,
        out_shape=jax.ShapeDtypeStruct((DEPTH, N_COND_ROWS, n_out), F32),
        grid=(DEPTH, n_out // tn),
        in_specs=[
            pl.BlockSpec((N_COND_ROWS, D_MODEL), lambda l, j: (0, 0)),
            pl.BlockSpec((None, D_MODEL, tn), lambda l, j: (l, 0, j)),
            pl.BlockSpec((None, 1, tn), lambda l, j: (l, 0, j)),
        ],
        out_specs=pl.BlockSpec((None, N_COND_ROWS, tn), lambda l, j: (l, 0, j)),
        compiler_params=_params(2),
        name="modulation",
    )(cond, w_mod, b_mod.reshape(DEPTH, 1, n_out))
    return out.reshape(DEPTH, N_COND_ROWS, 6, D_MODEL)


CONV_TM = 2048
CONV_TN = 256


def _conv_in_kernel(*refs, n_x):
    x_refs = refs[:n_x]
    mod_ref, g_ref, wb_ref, wc_ref, wx_ref, cw_ref, o_ref, h_ref = refs[n_x:]
    i = pl.program_id(0)

    @pl.when(pl.program_id(1) == 0)
    def _():
        _norm_mod_rows(_row_loader(x_refs, CONV_TM), CONV_TM, g_ref, mod_ref[1:2, :],
                       mod_ref[0:1, :], h_ref)

    h = h_ref[...]
    b = _dot(h, wb_ref[...].astype(BF16))
    z = _dot(h, wc_ref[...].astype(BF16)) * _dot(h, wx_ref[...].astype(BF16))
    seq_len = jnp.where(i < N_PROMPT_ROWS // CONV_TM, SEQ, DEC_SEQ)
    pos = lax.broadcasted_iota(jnp.int32, z.shape, 0) & (seq_len - 1)
    z_prev = jnp.where(pos == 0, 0.0, pltpu.roll(z, 1, 0))
    z_next = jnp.where(pos == seq_len - 1, 0.0, pltpu.roll(z, CONV_TM - 1, 0))
    conv = z_prev * cw_ref[0:1, :] + z * cw_ref[1:2, :] + z_next * cw_ref[2:3, :]
    o_ref[...] = (b * conv).astype(o_ref.dtype)


def _conv_in(x, mod, g_mix, conv_w_in, conv_w, layer, j):
    tm, tn = CONV_TM, CONV_TN
    nb = D_MODEL // tn
    w_spec = lambda part: pl.BlockSpec((None, D_MODEL, tn),
                                       lambda i, n: (j, 0, part * nb + n))
    x_parts = _row_parts(x)
    return pl.pallas_call(
        functools.partial(_conv_in_kernel, n_x=len(x_parts)),
        out_shape=jax.ShapeDtypeStruct((N_ROWS, D_MODEL), BF16),
        grid=(N_ROWS // tm, nb),
        in_specs=_row_specs(x, tm) + [
            _mod_spec(layer, tm),
            _gain_spec(layer),
            w_spec(0), w_spec(1), w_spec(2),
            pl.BlockSpec((None, CONV_W, tn), lambda i, n: (j, 0, n)),
        ],
        out_specs=pl.BlockSpec((tm, tn), lambda i, n: (i, n)),
        scratch_shapes=[pltpu.VMEM((tm, D_MODEL), BF16)],
        compiler_params=_params(2),
        name="conv_in",
    )(*x_parts, mod, g_mix, conv_w_in, conv_w_in, conv_w_in, conv_w)


PROJ_TM = 1024


def _proj_res_kernel(*refs, n_a, n_x):
    a_refs, w_ref, x_refs = refs[:n_a], refs[n_a], refs[n_a + 1:n_a + 1 + n_x]
    mod_ref, o_ref = refs[n_a + 1 + n_x:]
    whole = slice(None)
    y = _dot(_row_loader(a_refs, PROJ_TM)(whole), w_ref[...].astype(BF16))
    o_ref[...] = _row_loader(x_refs, PROJ_TM)(whole) + mod_ref[2:3, :] * y


def _proj_res(a, w, j, x, mod, layer):
    tm = PROJ_TM
    a_parts, x_parts = _row_parts(a), _row_parts(x)
    k = a_parts[0].shape[1]
    return pl.pallas_call(
        functools.partial(_proj_res_kernel, n_a=len(a_parts), n_x=len(x_parts)),
        out_shape=jax.ShapeDtypeStruct((N_ROWS, D_MODEL), F32),
        grid=(N_ROWS // tm,),
        in_specs=(_row_specs(a, tm) + [pl.BlockSpec((None, k, D_MODEL), lambda i: (j, 0, 0))]
                  + _row_specs(x, tm) + [_mod_spec(layer, tm)]),
        out_specs=pl.BlockSpec((tm, D_MODEL), lambda i: (i, 0)),
        compiler_params=_params(1),
        name="proj_res",
    )(*a_parts, w, *x_parts, mod)


GMLP_TM = 1024
GMLP_TN = 1024
GMLP_OUT_TM = 512


def _gelu_tanh(x):
    c = 0.7978845608028654
    return 0.5 * x * (1.0 + jnp.tanh(c * (x + 0.044715 * (x * x * x))))


def _gmlp_in_kernel(x_ref, mod_ref, g_ref, w_ref, b_ref, o_ref, h_ref):
    @pl.when(pl.program_id(1) == 0)
    def _():
        _norm_mod_to(x_ref, g_ref, mod_ref[1:2, :], mod_ref[0:1, :], h_ref)

    z = _dot(h_ref[...], w_ref[...].astype(BF16)) + b_ref[...]
    o_ref[...] = _gelu_tanh(z).astype(o_ref.dtype)


def _gmlp_in(x, mod, g_mix, w_in, b_in, layer, j):
    tm, tn = GMLP_TM, GMLP_TN
    n_out = 2 * GMLP_DIM
    return pl.pallas_call(
        _gmlp_in_kernel,
        out_shape=jax.ShapeDtypeStruct((N_ROWS, n_out), BF16),
        grid=(N_ROWS // tm, n_out // tn),
        in_specs=[
            pl.BlockSpec((tm, D_MODEL), lambda i, n: (i, 0)),
            _mod_spec(layer, tm),
            _gain_spec(layer),
            pl.BlockSpec((None, D_MODEL, tn), lambda i, n: (j, 0, n)),
            pl.BlockSpec((None, 1, tn), lambda i, n: (j, 0, n)),
        ],
        out_specs=pl.BlockSpec((tm, tn), lambda i, n: (i, n)),
        scratch_shapes=[pltpu.VMEM((tm, D_MODEL), BF16)],
        compiler_params=_params(2),
        name="gmlp_in",
    )(x, mod, g_mix, w_in, b_in.reshape(-1, 1, n_out))


def _gmlp_out_kernel(u_ref, v_ref, lng_ref, lnb_ref, ws_ref, bs_ref, w_ref, x_ref, mod_ref,
                     o_ref, a_ref):
    lng = lng_ref[...]
    lnb = lnb_ref[...]

    def chunk(c, carry):
        rows = pl.ds(pl.multiple_of(c * CHUNK, CHUNK), CHUNK)
        v = v_ref[rows, :].astype(F32)
        mu = jnp.mean(v, axis=-1, keepdims=True)
        vc = v - mu
        var = jnp.mean(vc * vc, axis=-1, keepdims=True)
        vn = (vc * lax.rsqrt(var + EPS) * lng + lnb).astype(BF16)
        for g in range(GMLP_GROUPS):
            cols = slice(g * GMLP_GROUP_DIM, (g + 1) * GMLP_GROUP_DIM)
            mixed = _dot(ws_ref[g].astype(BF16), vn[:, cols]) + bs_ref[:, g:g + 1]
            a_ref[rows, cols] = (u_ref[rows, cols].astype(F32) * mixed).astype(a_ref.dtype)
        return carry

    lax.fori_loop(0, u_ref.shape[0] // CHUNK, chunk, 0)
    y = _dot(a_ref[...], w_ref[...].astype(BF16))
    o_ref[...] = x_ref[...] + mod_ref[2:3, :] * y


def _gmlp_out(z, ln_g, ln_b, w_s, b_s, w_out, x, mod, layer, j):
    tm = GMLP_OUT_TM
    return pl.pallas_call(
        _gmlp_out_kernel,
        out_shape=jax.ShapeDtypeStruct((N_ROWS, D_MODEL), F32),
        grid=(N_ROWS // tm,),
        in_specs=[
            pl.BlockSpec((tm, GMLP_DIM), lambda i: (i, 0)),
            pl.BlockSpec((tm, GMLP_DIM), lambda i: (i, 1)),
            pl.BlockSpec((None, 1, GMLP_DIM), lambda i: (j, 0, 0)),
            pl.BlockSpec((None, 1, GMLP_DIM), lambda i: (j, 0, 0)),
            pl.BlockSpec((None, GMLP_GROUPS, CHUNK, CHUNK), lambda i: (j, 0, 0, 0)),
            pl.BlockSpec((None, CHUNK, GMLP_GROUPS), lambda i: (j, 0, 0)),
            pl.BlockSpec((None, GMLP_DIM, D_MODEL), lambda i: (j, 0, 0)),
            pl.BlockSpec((tm, D_MODEL), lambda i: (i, 0)),
            _mod_spec(layer, tm),
        ],
        out_specs=pl.BlockSpec((tm, D_MODEL), lambda i: (i, 0)),
        scratch_shapes=[pltpu.VMEM((tm, GMLP_DIM), BF16)],
        compiler_params=_params(1),
        name="gmlp_out",
    )(z, z, ln_g.reshape(-1, 1, GMLP_DIM), ln_b.reshape(-1, 1, GMLP_DIM), w_s,
      jnp.swapaxes(b_s, 1, 2), w_out, x, mod)


QKV_TM = 1024
ATTN_TQ = 1024
ATTN_PROMPT_SEQS = 4


def _rope_tables(tm):
    pos = np.arange(DEC_SEQ)
    row = (pos // GRID_W).astype(np.float32)
    col = (pos % GRID_W).astype(np.float32)
    n_freq = HEAD_DIM // 4
    inv_freq = np.power(np.float32(ROPE_THETA), -np.arange(n_freq, dtype=np.float32) / n_freq)
    ang = np.concatenate([row[:, None] * inv_freq, col[:, None] * inv_freq], axis=-1)
    cos = np.repeat(np.cos(ang), 2, axis=-1)
    sign = np.tile(np.array([-1.0, 1.0], np.float32), HEAD_DIM // 2)
    sin = np.repeat(np.sin(ang), 2, axis=-1) * sign
    cos = np.concatenate([np.ones((tm, HEAD_DIM), np.float32), cos], axis=0)
    sin = np.concatenate([np.zeros((tm, HEAD_DIM), np.float32), sin], axis=0)
    return jnp.asarray(cos, F32), jnp.asarray(sin, F32)


def _swap_pairs(x):
    even = (lax.broadcasted_iota(jnp.int32, x.shape, 1) & 1) == 0
    return jnp.where(even, pltpu.roll(x, HEAD_DIM - 1, 1), pltpu.roll(x, 1, 1))


def _qkv_kernel(x_ref, mod_ref, g_ref, w_ref, qg_ref, kg_ref, cos_ref, sin_ref, o_ref,
                h_ref, wsw_ref):
    n_rot_heads = N_HEADS + N_KV_HEADS

    @pl.when(pl.program_id(0) == 0)
    def _():
        for hd in range(n_rot_heads):
            cols = slice(hd * HEAD_DIM, (hd + 1) * HEAD_DIM)
            wsw_ref[:, cols] = _swap_pairs(w_ref[:, cols]).astype(BF16)

    _norm_mod_to(x_ref, g_ref, mod_ref[1:2, :], mod_ref[0:1, :], h_ref)
    y = _dot(h_ref[...], w_ref[...].astype(BF16))
    y_sw = _dot(h_ref[...], wsw_ref[...])
    cos = cos_ref[...]
    sin = sin_ref[...]
    for hd in range(n_rot_heads):
        cols = slice(hd * HEAD_DIM, (hd + 1) * HEAD_DIM)
        yh = y[:, cols]
        gain = qg_ref[...] if hd < N_HEADS else kg_ref[...]
        inv = lax.rsqrt(jnp.mean(yh * yh, axis=-1, keepdims=True) + EPS)
        o_ref[:, cols] = (yh * (gain * cos) + y_sw[:, cols] * (_swap_pairs(gain) * sin)) * inv
    v_cols = slice(n_rot_heads * HEAD_DIM, QKV_OUT)
    o_ref[:, v_cols] = y[:, v_cols]


def _qkv(x, mod, g_mix, w_qkv, q_g, k_g, layer, j):
    tm = QKV_TM
    cos, sin = _rope_tables(tm)
    n_prompt = N_PROMPT_ROWS // tm
    per_seq = DEC_SEQ // tm
    tbl = lambda i: (jnp.where(i < n_prompt, 0, 1 + (i - n_prompt) % per_seq), 0)
    return pl.pallas_call(
        _qkv_kernel,
        out_shape=jax.ShapeDtypeStruct((N_ROWS, QKV_OUT), F32),
        grid=(N_ROWS // tm,),
        in_specs=[
            pl.BlockSpec((tm, D_MODEL), lambda i: (i, 0)),
            _mod_spec(layer, tm),
            _gain_spec(layer),
            pl.BlockSpec((None, D_MODEL, QKV_OUT), lambda i: (j, 0, 0)),
            pl.BlockSpec((None, 1, HEAD_DIM), lambda i: (j, 0, 0)),
            pl.BlockSpec((None, 1, HEAD_DIM), lambda i: (j, 0, 0)),
            pl.BlockSpec((tm, HEAD_DIM), tbl),
            pl.BlockSpec((tm, HEAD_DIM), tbl),
        ],
        out_specs=pl.BlockSpec((tm, QKV_OUT), lambda i: (i, 0)),
        scratch_shapes=[pltpu.VMEM((tm, D_MODEL), BF16),
                        pltpu.VMEM((D_MODEL, (N_HEADS + N_KV_HEADS) * HEAD_DIM), BF16)],
        compiler_params=_params(1),
        name="qkv",
    )(x, mod, g_mix, w_qkv, q_g.reshape(-1, 1, HEAD_DIM), k_g.reshape(-1, 1, HEAD_DIM), cos, sin)


def _attn_kernel(*refs, has_ctx, n_seq):
    if has_ctx:
        q_ref, k_ref, v_ref, ck_ref, cv_ref, o_ref = refs
    else:
        q_ref, k_ref, v_ref, o_ref = refs
    nt = (((1,), (1,)), ((), ()))
    if has_ctx:
        ck = ck_ref[...].astype(BF16)
        cv = cv_ref[...].astype(BF16)
    q_len = q_ref.shape[0] // n_seq
    k_len = k_ref.shape[0] // n_seq
    for sq in range(n_seq):
        q_rows = slice(sq * q_len, (sq + 1) * q_len)
        k_rows = slice(sq * k_len, (sq + 1) * k_len)
        k = k_ref[k_rows, :].astype(BF16)
        v = v_ref[k_rows, :].astype(BF16)
        for g in range(Q_PER_KV):
            cols = slice(g * HEAD_DIM, (g + 1) * HEAD_DIM)
            q = (q_ref[q_rows, cols] * (HEAD_DIM ** -0.5)).astype(BF16)
            s = lax.dot_general(q, k, nt, preferred_element_type=F32)
            m = jnp.max(s, axis=-1, keepdims=True)
            if has_ctx:
                sc = lax.dot_general(q, ck, nt, preferred_element_type=F32)
                m = jnp.maximum(m, jnp.max(sc, axis=-1, keepdims=True))
            p = jnp.exp(s - m)
            den = jnp.sum(p, axis=-1, keepdims=True)
            o = _dot(p.astype(BF16), v)
            if has_ctx:
                pc = jnp.exp(sc - m)
                den = den + jnp.sum(pc, axis=-1, keepdims=True)
                o = o + _dot(pc.astype(BF16), cv)
            o_ref[q_rows, cols] = (o / den).astype(o_ref.dtype)


def _attention(qkv, ctx_k, ctx_v):
    group_w = Q_PER_KV * HEAD_DIM
    k_col = N_HEADS
    v_col = N_HEADS + N_KV_HEADS

    def call(n_batch, seq, tq, row0, ctx, n_seq):
        q_blocks = seq // tq
        q_rows, kv_rows = n_seq * tq, n_seq * seq
        o_row = lambda b, qi: b * q_blocks + qi
        kv_row = lambda b: row0 // kv_rows + b
        in_specs = [
            pl.BlockSpec((q_rows, group_w), lambda b, kh, qi: (row0 // q_rows + o_row(b, qi), kh)),
            pl.BlockSpec((kv_rows, HEAD_DIM), lambda b, kh, qi: (kv_row(b), k_col + kh)),
            pl.BlockSpec((kv_rows, HEAD_DIM), lambda b, kh, qi: (kv_row(b), v_col + kh)),
        ]
        args = [qkv, qkv, qkv]
        if ctx is not None:
            ctx_spec = pl.BlockSpec((None, PAST_LEN, HEAD_DIM), lambda b, kh, qi: (b, 0, kh))
            in_specs += [ctx_spec, ctx_spec]
            args += list(ctx)
        return pl.pallas_call(
            functools.partial(_attn_kernel, has_ctx=ctx is not None, n_seq=n_seq),
            out_shape=jax.ShapeDtypeStruct((n_batch * seq, N_HEADS * HEAD_DIM), BF16),
            grid=(n_batch // n_seq, N_KV_HEADS, q_blocks),
            in_specs=in_specs,
            out_specs=pl.BlockSpec((q_rows, group_w), lambda b, kh, qi: (o_row(b, qi), kh)),
            compiler_params=_params(3),
            name="attention",
        )(*args)

    o_prompt = call(BATCH, SEQ, SEQ, 0, None, ATTN_PROMPT_SEQS)
    o_sample = call(DEC_BATCH, DEC_SEQ, ATTN_TQ, N_PROMPT_ROWS, (ctx_k, ctx_v), 1)
    return o_prompt, o_sample


FFN_TM = 1024
FFN_TF = 256
TOP_K = 2
ROUTER_TM = 1024
ROUTE_LANES = LANES
MOE_TF = 512
MOE_TILE = 1024
MOE_SUB = 128
MOE_MAX_TILES = TOP_K * N_ROWS // MOE_TILE + N_EXPERTS
MOE_ROWS = MOE_MAX_TILES * MOE_TILE
SCATTER_TM = 512
COMBINE_TM = 256
WIN_CHUNK = 128
WIN_CHUNKS = COMBINE_TM // WIN_CHUNK + 1
WIN_SLOTS = TOP_K * COMBINE_TM // WIN_CHUNK + 2 * N_EXPERTS
WIN_SLOT_STEPS = (12, 16, WIN_SLOTS)


def _swiglu_hidden(h, wg, wu):
    gate = _dot(h, wg)
    return (gate * jax.nn.sigmoid(gate) * _dot(h, wu)).astype(BF16)


def _ffn_kernel(*refs, n_a):
    if n_a:
        a_refs, wp_ref, refs = refs[:n_a], refs[n_a], refs[n_a + 1:]
        x_ref, mod_ref, g_ref, wg_ref, wu_ref, wd_ref, o_ref, h_ref, x1_ref = refs
    else:
        x_ref, mod_ref, g_ref, wg_ref, wu_ref, wd_ref, o_ref, h_ref = refs
        x1_ref = x_ref
    f = pl.program_id(1)

    @pl.when(f == 0)
    def _():
        if n_a:
            y = _dot(_row_loader(a_refs, FFN_TM)(slice(None)), wp_ref[...].astype(BF16))
            x1_ref[...] = x_ref[...] + mod_ref[2:3, :] * y
        _norm_mod_to(x1_ref, g_ref, mod_ref[4:5, :], mod_ref[3:4, :], h_ref)
        o_ref[...] = jnp.zeros_like(o_ref)

    a = _swiglu_hidden(h_ref[...], wg_ref[...].astype(BF16), wu_ref[...].astype(BF16))
    o_ref[...] += _dot(a, wd_ref[...].astype(BF16))

    @pl.when(f == pl.num_programs(1) - 1)
    def _():
        o_ref[...] = x1_ref[...] + mod_ref[5:6, :] * o_ref[...]


def _ffn(x, mod, g_ffn, w_gate, w_up, w_down, layer, li, proj=None):
    tm, tf = FFN_TM, FFN_TF
    in_specs, args, scratch = [], [], [pltpu.VMEM((tm, D_MODEL), BF16)]
    a_parts = ()
    if proj is not None:
        a, w_proj, j = proj
        a_parts = _row_parts(a)
        k = a_parts[0].shape[1]
        in_specs = _row_specs(a, tm) + [pl.BlockSpec((None, k, D_MODEL), lambda i, f: (j, 0, 0))]
        args = [*a_parts, w_proj]
        scratch.append(pltpu.VMEM((tm, D_MODEL), F32))
    return pl.pallas_call(
        functools.partial(_ffn_kernel, n_a=len(a_parts)),
        out_shape=jax.ShapeDtypeStruct((N_ROWS, D_MODEL), F32),
        grid=(N_ROWS // tm, D_FF // tf),
        in_specs=in_specs + [
            pl.BlockSpec((tm, D_MODEL), lambda i, f: (i, 0)),
            _mod_spec(layer, tm),
            _gain_spec(layer),
            pl.BlockSpec((None, D_MODEL, tf), lambda i, f: (li, 0, f)),
            pl.BlockSpec((None, D_MODEL, tf), lambda i, f: (li, 0, f)),
            pl.BlockSpec((None, tf, D_MODEL), lambda i, f: (li, f, 0)),
        ],
        out_specs=pl.BlockSpec((tm, D_MODEL), lambda i, f: (i, 0)),
        scratch_shapes=scratch,
        compiler_params=_params(2),
        name="ffn",
    )(*args, x, mod, g_ffn, w_gate, w_up, w_down)


def _split_bf16(x):
    hi = x.astype(BF16)
    return hi, (x - hi.astype(F32)).astype(BF16)


def _router_kernel(x_ref, mod_ref, g_ref, w_ref, route_ref, route_t_ref, cnt_ref, base_ref,
                   h_ref, seen_ref):
    @pl.when(pl.program_id(0) == 0)
    def _():
        seen_ref[...] = jnp.zeros_like(seen_ref)

    _norm_mod_to(x_ref, g_ref, mod_ref[4:5, :], mod_ref[3:4, :], h_ref)
    h_hi, h_lo = _split_bf16(h_ref[...])
    w_hi, w_lo = _split_bf16(w_ref[...])
    logits = _dot(h_hi, w_hi) + (_dot(h_hi, w_lo) + _dot(h_lo, w_hi))
    lane = lax.broadcasted_iota(jnp.int32, logits.shape, 1)
    logits = jnp.where(lane < N_EXPERTS, logits, -jnp.inf)
    m1 = jnp.max(logits, axis=-1, keepdims=True)
    i1 = jnp.min(jnp.where(logits == m1, lane, N_EXPERTS), axis=-1, keepdims=True)
    rest = jnp.where(lane == i1, -jnp.inf, logits)
    m2 = jnp.max(rest, axis=-1, keepdims=True)
    i2 = jnp.min(jnp.where(rest == m2, lane, N_EXPERTS), axis=-1, keepdims=True)
    e2 = jnp.exp(m2 - m1)
    w1 = 1.0 / (1.0 + e2)
    chosen = jnp.where((lane == i1) | (lane == i2), 1.0, 0.0)
    tm = chosen.shape[0]
    blk = COMBINE_TM
    earlier = (lax.broadcasted_iota(jnp.int32, (blk, blk), 1)
               < lax.broadcasted_iota(jnp.int32, (blk, blk), 0))
    earlier = jnp.where(earlier, 1.0, 0.0).astype(BF16)
    base_ref[...] = jnp.zeros_like(base_ref)
    seen = seen_ref[...]
    priors = []
    for k in range(tm // blk):
        rows = slice(k * blk, (k + 1) * blk)
        base_ref[k:k + 1, :] = seen
        priors.append(_dot(earlier, chosen[rows].astype(BF16)) + seen)
        seen = seen + jnp.sum(chosen[rows], axis=0, keepdims=True)
    seen_ref[...] = seen
    prior = jnp.concatenate(priors, axis=0)
    rank1 = jnp.sum(jnp.where(lane == i1, prior, 0.0), axis=-1, keepdims=True)
    rank2 = jnp.sum(jnp.where(lane == i2, prior, 0.0), axis=-1, keepdims=True)
    route = jnp.zeros_like(logits)
    for k, field in enumerate((w1, e2 * w1, i1.astype(F32), i2.astype(F32), rank1, rank2)):
        route = jnp.where(lane == k, field, route)
    route_ref[...] = route
    route_t_ref[...] = jnp.transpose(route)[:SUBLANES, :]
    cnt_ref[...] = jnp.broadcast_to(seen, cnt_ref.shape)


def _router(x, mod, g_ffn, w_router, layer, li):
    tm = ROUTER_TM
    w_pad = jnp.pad(w_router, ((0, 0), (0, 0), (0, ROUTE_LANES - N_EXPERTS)))
    return pl.pallas_call(
        _router_kernel,
        out_shape=(jax.ShapeDtypeStruct((N_ROWS, ROUTE_LANES), F32),
                   jax.ShapeDtypeStruct((SUBLANES, N_ROWS), F32),
                   jax.ShapeDtypeStruct((SUBLANES, ROUTE_LANES), F32),
                   jax.ShapeDtypeStruct((N_ROWS // tm, SUBLANES, ROUTE_LANES), F32)),
        grid=(N_ROWS // tm,),
        in_specs=[
            pl.BlockSpec((tm, D_MODEL), lambda i: (i, 0)),
            _mod_spec(layer, tm),
            _gain_spec(layer),
            pl.BlockSpec((None, D_MODEL, ROUTE_LANES), lambda i: (li, 0, 0)),
        ],
        out_specs=(pl.BlockSpec((tm, ROUTE_LANES), lambda i: (i, 0)),
                   pl.BlockSpec((SUBLANES, tm), lambda i: (0, i)),
                   pl.BlockSpec((SUBLANES, ROUTE_LANES), lambda i: (0, 0)),
                   pl.BlockSpec((None, SUBLANES, ROUTE_LANES), lambda i: (i, 0, 0))),
        scratch_shapes=[pltpu.VMEM((tm, D_MODEL), F32), pltpu.VMEM((1, ROUTE_LANES), F32)],
        compiler_params=_params(1),
        name="router",
    )(x, mod, g_ffn, w_pad)


def _dispatch_plan(route_t, cnt, tile_base):
    counts = cnt[0, :N_EXPERTS].astype(jnp.int32)
    e1, e2, r1, r2 = (route_t[k].astype(jnp.int32) for k in (2, 3, 4, 5))
    n_tiles = (counts + MOE_TILE - 1) // MOE_TILE
    tile_end = jnp.cumsum(n_tiles)
    tile_start = tile_end - n_tiles
    row_start = tile_start * MOE_TILE
    pos = jnp.concatenate([row_start[e1] + r1, row_start[e2] + r2])
    n_active = tile_end[-1]
    t = jnp.arange(MOE_MAX_TILES, dtype=jnp.int32)
    tile_block = jnp.minimum(t, n_active - 1)
    tile_expert = jnp.sum(tile_block[:, None] >= tile_end[None, :], axis=1).astype(jnp.int32)
    valid = counts[tile_expert] - (tile_block - tile_start[tile_expert]) * MOE_TILE
    n_sub = (jnp.clip(valid, 0, MOE_TILE) + MOE_SUB - 1) // MOE_SUB
    tile_nsub = jnp.where(t < n_active, n_sub, 0).astype(jnp.int32)
    pad = jnp.concatenate([row_start + counts, (-counts) % MOE_SUB]).astype(jnp.int32)
    base = tile_base[:, :ROUTER_TM // COMBINE_TM, :N_EXPERTS].astype(jnp.int32)
    base = base.reshape(N_ROWS // COMBINE_TM, N_EXPERTS)
    lo = row_start[None, :] + base
    hi = row_start[None, :] + jnp.concatenate([base[1:], counts[None, :]], axis=0)
    chunk0 = lo // WIN_CHUNK
    n_chunks = jnp.where(hi > lo, (hi + WIN_CHUNK - 1) // WIN_CHUNK - chunk0, 0)
    j = jnp.arange(WIN_CHUNKS, dtype=jnp.int32)
    needed = (j[None, None, :] < n_chunks[:, :, None]).reshape(-1, N_EXPERTS * WIN_CHUNKS)
    chunk = (chunk0[:, :, None] + j[None, None, :]).reshape(-1, N_EXPERTS * WIN_CHUNKS)
    order = jnp.argsort(jnp.logical_not(needed), axis=1, stable=True)[:, :WIN_SLOTS]
    win_count = jnp.sum(needed, axis=1).astype(jnp.int32)
    win_list = jnp.where(jnp.arange(WIN_SLOTS)[None, :] < win_count[:, None],
                         jnp.take_along_axis(chunk, order, axis=1), -1)
    return dict(pos=pos, pad=pad, tile_expert=tile_expert, tile_nsub=tile_nsub,
                tile_block=tile_block, row_start=row_start.astype(jnp.int32),
                win_list=win_list.reshape(-1).astype(jnp.int32), win_count=win_count)


def _for_rows(n_rows, body):
    def group(j, carry):
        r0 = pl.multiple_of(j * SUBLANES, SUBLANES)
        for u in range(SUBLANES):
            body(r0 + u)
        return carry
    lax.fori_loop(0, n_rows // SUBLANES, group, 0)


def _scatter_kernel(pos_ref, pad_ref, tn_ref, x_ref, mod_ref, g_ref, xs_ref, h_ref, zero_ref,
                    sem, fill_sem):
    i = pl.program_id(0)
    ts = x_ref.shape[0]
    slot = lax.rem(i, 2)
    _norm_mod_to(x_ref, g_ref, mod_ref[4:5, :], mod_ref[3:4, :], h_ref.at[slot])

    def for_row_copies(step, step_slot, act):
        first = step * ts

        def row(r):
            for k in range(TOP_K):
                dst = pos_ref[k * N_ROWS + first + r]
                act(pltpu.make_async_copy(h_ref.at[step_slot, pl.ds(r, 1)],
                                          xs_ref.at[pl.ds(dst, 1)], sem.at[step_slot]), k)
        _for_rows(ts, row)

    for_row_copies(i, slot, lambda cp, k: cp.start(priority=k))

    @pl.when(i == 0)
    def _():
        zero_ref[...] = jnp.zeros_like(zero_ref)

        def fills(act):
            for e in range(N_EXPERTS):
                def row_body(r, carry, e=e):
                    act(pltpu.make_async_copy(zero_ref.at[pl.ds(0, 1)],
                                              xs_ref.at[pl.ds(pad_ref[e] + r, 1)], fill_sem))
                    return carry
                lax.fori_loop(0, pad_ref[N_EXPERTS + e], row_body, 0)

            def tile_body(t, carry):
                def sub_body(s, carry):
                    row = pl.multiple_of(t * MOE_TILE + s * MOE_SUB, MOE_SUB)
                    act(pltpu.make_async_copy(zero_ref, xs_ref.at[pl.ds(row, MOE_SUB)], fill_sem))
                    return carry
                return lax.fori_loop(tn_ref[t], MOE_TILE // MOE_SUB, sub_body, carry)
            lax.fori_loop(0, MOE_MAX_TILES, tile_body, 0)

        fills(lambda cp: cp.start())
        fills(lambda cp: cp.wait())

    @pl.when(i > 0)
    def _():
        for_row_copies(i - 1, 1 - slot, lambda cp, k: cp.wait())

    @pl.when(i == pl.num_programs(0) - 1)
    def _():
        for_row_copies(i, slot, lambda cp, k: cp.wait())


def _moe_scatter(x, mod, g_ffn, plan, layer):
    ts = SCATTER_TM
    grid_spec = pltpu.PrefetchScalarGridSpec(
        num_scalar_prefetch=3,
        grid=(N_ROWS // ts,),
        in_specs=[
            pl.BlockSpec((ts, D_MODEL), lambda i, *_: (i, 0)),
            _mod_spec(layer, ts),
            _gain_spec(layer),
        ],
        out_specs=pl.BlockSpec(memory_space=pl.ANY),
        scratch_shapes=[pltpu.VMEM((2, ts, D_MODEL), F32),
                        pltpu.VMEM((MOE_SUB, D_MODEL), F32),
                        pltpu.SemaphoreType.DMA((2,)), pltpu.SemaphoreType.DMA(())],
    )
    return pl.pallas_call(
        _scatter_kernel,
        out_shape=jax.ShapeDtypeStruct((MOE_ROWS, D_MODEL), F32),
        grid_spec=grid_spec,
        compiler_params=_params(1),
        name="moe_scatter",
    )(plan["pos"], plan["pad"], plan["tile_nsub"], x, mod, g_ffn)


def _moe_kernel(te_ref, tn_ref, tb_ref, x_ref, wg_ref, wu_ref, wd_ref, o_ref,
                xb_ref, acc_ref):
    n_sub = tn_ref[pl.program_id(0)]
    first_f = pl.program_id(1) == 0

    @pl.when(first_f)
    def _():
        acc_ref[...] = jnp.zeros_like(acc_ref)

    for n in range(1, MOE_TILE // MOE_SUB + 1):
        @pl.when(n_sub == n)
        def _(n=n):
            rows = slice(0, n * MOE_SUB)

            @pl.when(first_f)
            def _():
                xb_ref[rows, :] = x_ref[rows, :].astype(BF16)

            a = _swiglu_hidden(xb_ref[rows, :], wg_ref[...].astype(BF16),
                               wu_ref[...].astype(BF16))
            acc_ref[rows, :] += _dot(a, wd_ref[...].astype(BF16))

    @pl.when(pl.program_id(1) == pl.num_programs(1) - 1)
    def _():
        o_ref[...] = acc_ref[...].astype(o_ref.dtype)


def _moe_experts(xs, plan, w_gate, w_up, w_down, li):
    tile, tf = MOE_TILE, MOE_TF
    nf = D_FF_EXPERT // tf
    frozen_f = lambda i, f, tn: jnp.where(tn[i] > 0, f, nf - 1)
    grid_spec = pltpu.PrefetchScalarGridSpec(
        num_scalar_prefetch=3,
        grid=(MOE_MAX_TILES, nf),
        in_specs=[
            pl.BlockSpec((tile, D_MODEL), lambda i, f, te, tn, tb: (tb[i], 0)),
            pl.BlockSpec((None, None, D_MODEL, tf),
                         lambda i, f, te, tn, tb: (li, te[i], 0, frozen_f(i, f, tn))),
            pl.BlockSpec((None, None, D_MODEL, tf),
                         lambda i, f, te, tn, tb: (li, te[i], 0, frozen_f(i, f, tn))),
            pl.BlockSpec((None, None, tf, D_MODEL),
                         lambda i, f, te, tn, tb: (li, te[i], frozen_f(i, f, tn), 0)),
        ],
        out_specs=pl.BlockSpec((tile, D_MODEL), lambda i, f, te, tn, tb: (i, 0)),
        scratch_shapes=[pltpu.VMEM((tile, D_MODEL), BF16), pltpu.VMEM((tile, D_MODEL), F32)],
    )
    return pl.pallas_call(
        _moe_kernel,
        out_shape=jax.ShapeDtypeStruct((MOE_ROWS, D_MODEL), BF16),
        grid_spec=grid_spec,
        compiler_params=_params(2),
        name="moe_experts",
    )(plan["tile_expert"], plan["tile_nsub"], plan["tile_block"], xs, w_gate, w_up, w_down)


def _combine_kernel(start_ref, list_ref, n_ref, x_ref, mod_ref, route_ref, ys_ref, *refs):
    *o_refs, win_ref, sem = refs
    i = pl.program_id(0)
    slot = lax.rem(i, 2)

    def window_copies(tile, tile_slot, act):
        def body(s, carry):
            chunk = list_ref[tile * WIN_SLOTS + s]
            src = ys_ref.at[pl.ds(pl.multiple_of(chunk * WIN_CHUNK, WIN_CHUNK), WIN_CHUNK)]
            dst = win_ref.at[tile_slot,
                             pl.ds(pl.multiple_of(s * WIN_CHUNK, WIN_CHUNK), WIN_CHUNK)]
            act(pltpu.make_async_copy(src, dst, sem.at[tile_slot]))
            return carry
        lax.fori_loop(0, n_ref[tile], body, 0)

    @pl.when(i == 0)
    def _():
        win_ref[...] = jnp.zeros_like(win_ref)
        window_copies(i, slot, lambda cp: cp.start())

    @pl.when(i + 1 < pl.num_programs(0))
    def _():
        window_copies(i + 1, 1 - slot, lambda cp: cp.start())

    window_copies(i, slot, lambda cp: cp.wait())

    w1, w2, e1, e2, pos1, pos2 = (route_ref[:, k:k + 1] for k in range(6))
    for e in range(N_EXPERTS):
        first_row = start_ref[e].astype(F32)
        pos1 = pos1 + jnp.where(e1 == e, first_row, 0.0)
        pos2 = pos2 + jnp.where(e2 == e, first_row, 0.0)
    lane = lax.broadcasted_iota(jnp.int32, (x_ref.shape[0], WIN_CHUNK), 1).astype(F32)

    def combine(n_slots):
        blocks = []
        for s in range(n_slots):
            rows = lane + (list_ref[i * WIN_SLOTS + s] * WIN_CHUNK).astype(F32)
            weight = jnp.where(pos1 == rows, w1, 0.0) + jnp.where(pos2 == rows, w2, 0.0)
            blocks.append(weight.astype(BF16))
        y = _dot(jnp.concatenate(blocks, axis=1), win_ref[slot, :n_slots * WIN_CHUNK, :])
        out = x_ref[...] + mod_ref[5:6, :] * y
        if len(o_refs) == 1:
            o_refs[0][...] = out
        else:
            is_prompt = i < N_PROMPT_ROWS // x_ref.shape[0]

            @pl.when(is_prompt)
            def _():
                o_refs[0][...] = out

            @pl.when(jnp.logical_not(is_prompt))
            def _():
                o_refs[1][...] = out

    n_chunks = n_ref[i]
    smaller = -1
    for n_slots in WIN_SLOT_STEPS:
        @pl.when((n_chunks > smaller) & (n_chunks <= n_slots))
        def _(n_slots=n_slots):
            combine(n_slots)
        smaller = n_slots


def _moe_combine(x, mod, route, ys, plan, layer, split_out):
    tc = COMBINE_TM
    if split_out:
        half = jax.ShapeDtypeStruct((N_PROMPT_ROWS, D_MODEL), F32)
        out_shape = (half, jax.ShapeDtypeStruct((N_SAMPLE_ROWS, D_MODEL), F32))
        out_specs = tuple(_row_specs((half, half), tc))
    else:
        out_shape = jax.ShapeDtypeStruct((N_ROWS, D_MODEL), F32)
        out_specs = pl.BlockSpec((tc, D_MODEL), lambda i, *_: (i, 0))
    grid_spec = pltpu.PrefetchScalarGridSpec(
        num_scalar_prefetch=3,
        grid=(N_ROWS // tc,),
        in_specs=[
            pl.BlockSpec((tc, D_MODEL), lambda i, *_: (i, 0)),
            _mod_spec(layer, tc),
            pl.BlockSpec((tc, ROUTE_LANES), lambda i, *_: (i, 0)),
            pl.BlockSpec(memory_space=pl.ANY),
        ],
        out_specs=out_specs,
        scratch_shapes=[pltpu.VMEM((2, WIN_SLOTS * WIN_CHUNK, D_MODEL), BF16),
                        pltpu.SemaphoreType.DMA((2,))],
    )
    return pl.pallas_call(
        _combine_kernel,
        out_shape=out_shape,
        grid_spec=grid_spec,
        compiler_params=_params(1),
        name="moe_combine",
    )(plan["row_start"], plan["win_list"], plan["win_count"], x, mod, route, ys)


def kernel(x_prompt, x_sample, cache_k, cache_v, c, c_ctx, w_mod, b_mod, g_mix, g_ffn,
           conv_w_in, conv_w, conv_w_out,
           gmlp_w_in, gmlp_b_in, gmlp_ln_g, gmlp_ln_b, gmlp_w_s, gmlp_b_s, gmlp_w_out,
           attn_w_qkv, attn_q_g, attn_k_g, attn_w_o,
           ffn_w_gate, ffn_w_up, ffn_w_down,
           moe_w_router, moe_w_gate, moe_w_up, moe_w_down):
    x = (x_prompt.reshape(N_PROMPT_ROWS, D_MODEL), x_sample.reshape(N_SAMPLE_ROWS, D_MODEL))
    cond = jnp.zeros((N_COND_ROWS, D_MODEL), F32).at[0].set(c_ctx).at[1:1 + DEC_BATCH].set(c)
    mod = _modulation(cond, w_mod, b_mod)
    g_mix3 = g_mix.reshape(DEPTH, 1, D_MODEL)
    g_ffn3 = g_ffn.reshape(DEPTH, 1, D_MODEL)
    n_attn = cache_k.shape[1]
    ctx_k = cache_k.reshape(DEC_BATCH, n_attn, PAST_LEN, N_KV_HEADS * HEAD_DIM)
    ctx_v = cache_v.reshape(DEC_BATCH, n_attn, PAST_LEN, N_KV_HEADS * HEAD_DIM)

    new_k, new_v = [], []
    pending = None
    for layer in range(DEPTH):
        kind, j = layer % N_MIXERS, layer // N_MIXERS
        if kind == 0:
            a = _conv_in(x, mod, g_mix3, conv_w_in, conv_w, layer, j)
            x = _proj_res(a, conv_w_out, j, x, mod, layer)
        elif kind == 1:
            z = _gmlp_in(x, mod, g_mix3, gmlp_w_in, gmlp_b_in, layer, j)
            x = _gmlp_out(z, gmlp_ln_g, gmlp_ln_b, gmlp_w_s, gmlp_b_s, gmlp_w_out, x, mod,
                          layer, j)
        else:
            qkv = _qkv(x, mod, g_mix3, attn_w_qkv, attn_q_g, attn_k_g, layer, j)
            k_cols = slice(N_HEADS * HEAD_DIM, (N_HEADS + N_KV_HEADS) * HEAD_DIM)
            v_cols = slice((N_HEADS + N_KV_HEADS) * HEAD_DIM, QKV_OUT)
            new_k.append(qkv[:N_PROMPT_ROWS, k_cols].reshape(BATCH, SEQ, N_KV_HEADS, HEAD_DIM))
            new_v.append(qkv[:N_PROMPT_ROWS, v_cols].reshape(BATCH, SEQ, N_KV_HEADS, HEAD_DIM))
            a = _attention(qkv, ctx_k[:, j], ctx_v[:, j])
            if layer % 2 == 0:
                pending = (a, attn_w_o, j)
            else:
                x = _proj_res(a, attn_w_o, j, x, mod, layer)
        li = layer // 2
        if layer % 2 == 0:
            x = _ffn(x, mod, g_ffn3, ffn_w_gate, ffn_w_up, ffn_w_down, layer, li, proj=pending)
            pending = None
        else:
            route, route_t, cnt, tile_base = _router(x, mod, g_ffn3, moe_w_router, layer, li)
            plan = _dispatch_plan(route_t, cnt, tile_base)
            xs = _moe_scatter(x, mod, g_ffn3, plan, layer)
            ys = _moe_experts(xs, plan, moe_w_gate, moe_w_up, moe_w_down, li)
            x = _moe_combine(x, mod, route, ys, plan, layer, split_out=layer == DEPTH - 1)

    if DEPTH % 2 == 0:
        y_prompt, y_sample = x
    else:
        y_prompt, y_sample = x[:N_PROMPT_ROWS], x[N_PROMPT_ROWS:]
    return (y_prompt.reshape(BATCH, SEQ, D_MODEL), y_sample.reshape(DEC_BATCH, DEC_SEQ, D_MODEL),
            jnp.stack(new_k, axis=1), jnp.stack(new_v, axis=1))
```
